```python
import jax
import jax.numpy as jnp
from jax import lax
import numpy as np

D_MODEL = 2048
BATCH = 2
SEQ = 16384
DEPTH = 2

F32 = jnp.float32
CHUNK = 64
HEAD_DIM = 64
N_HEADS = D_MODEL // 128
N_KV_HEADS = N_HEADS // 4
ATTN_WIDTH = N_HEADS * HEAD_DIM
KV_WIDTH = N_KV_HEADS * HEAD_DIM
WINDOW = 128
WINDOW_CHUNKS = WINDOW // CHUNK
ROPE_THETA = 10000.0
POOL_WINDOWS = (2, 4, 8, 16)
POOL_WIDTH = D_MODEL // 2
POOL_GROUP = POOL_WIDTH // len(POOL_WINDOWS)
LRU_WIDTH = D_MODEL // 2
LRU_BLOCKS = N_HEADS
LRU_BLOCK = LRU_WIDTH // LRU_BLOCKS
CONV_WIDTH = 4
LRU_C = 8.0
N_BRANCH = 3
BRANCH_WIDTH = D_MODEL // 2
IN_SPLITS = (ATTN_WIDTH, KV_WIDTH, KV_WIDTH, POOL_WIDTH, LRU_WIDTH, LRU_WIDTH, N_BRANCH * D_MODEL)
IN_WIDTH = sum(IN_SPLITS)
N_EXPERTS = 64
TOP_K = 8
D_EXPERT = D_MODEL // 4
ROUTED_SCALE = 2.5
EXPERT_BLOCK = 128
LN_EPS = 1e-5
DEEPNORM_ALPHA = (2 * DEPTH) ** 0.25
DEEPNORM_BETA = (8 * DEPTH) ** -0.25

kernel_name = 'hybrid_swa_pool_rglru_moe_deepnorm'


def layer_norm(x, g, b):
    x32 = x.astype(F32)
    mu = x32.mean(-1, keepdims=True)
    var = jnp.square(x32 - mu).mean(-1, keepdims=True)
    return ((x32 - mu) * lax.rsqrt(var + LN_EPS) * g.astype(F32) + b.astype(F32)).astype(x.dtype)


def rope_tables(positions):
    inv_freq = 1.0 / (ROPE_THETA ** (jnp.arange(0, HEAD_DIM, 2, dtype=F32) / HEAD_DIM))
    ang = positions.astype(F32)[..., None] * inv_freq
    return jnp.cos(ang), jnp.sin(ang)


def apply_rope(x, cos, sin):
    x1, x2 = jnp.split(x.astype(F32), 2, axis=-1)
    c = cos[:, :, None, :]
    s = sin[:, :, None, :]
    return jnp.concatenate([x1 * c - x2 * s, x2 * c + x1 * s], axis=-1).astype(x.dtype)


def sliding_window_attention(q, k, v, sinks, cos, sin):
    B, S = q.shape[:2]
    n_chunks = S // CHUNK
    n_groups = N_HEADS // N_KV_HEADS
    span = WINDOW_CHUNKS + 1
    q = apply_rope(q, cos, sin)
    k = apply_rope(k, cos, sin)
    qc = q.reshape(B, n_chunks, CHUNK, N_KV_HEADS, n_groups, HEAD_DIM).astype(F32)
    pad = ((0, 0), (WINDOW_CHUNKS * CHUNK, 0), (0, 0), (0, 0))
    kp = jnp.pad(k, pad).reshape(B, n_chunks + WINDOW_CHUNKS, CHUNK, N_KV_HEADS, HEAD_DIM)
    vp = jnp.pad(v, pad).reshape(B, n_chunks + WINDOW_CHUNKS, CHUNK, N_KV_HEADS, HEAD_DIM)
    kw = jnp.concatenate([kp[:, i:i + n_chunks] for i in range(span)], axis=2).astype(F32)
    vw = jnp.concatenate([vp[:, i:i + n_chunks] for i in range(span)], axis=2).astype(F32)
    scores = jnp.einsum('bcqkgd,bcskd->bckgqs', qc, kw) * (HEAD_DIM ** -0.5)
    key_chunk = jnp.arange(n_chunks)[:, None] + jnp.arange(span * CHUNK)[None, :] // CHUNK - WINDOW_CHUNKS
    valid = key_chunk >= 0
    scores = jnp.where(valid[None, :, None, None, None, :], scores, jnp.finfo(F32).min)
    sink = sinks.astype(F32).reshape(N_KV_HEADS, n_groups)[None, None, :, :, None, None]
    m = jnp.maximum(scores.max(-1, keepdims=True), sink)
    p = jnp.exp(scores - m)
    probs = p / (p.sum(-1, keepdims=True) + jnp.exp(sink - m))
    out = jnp.einsum('bckgqs,bcskd->bcqkgd', probs, vw)
    return out.reshape(B, S, ATTN_WIDTH).astype(q.dtype)


def multiscale_pool(u, pool_w, pool_scale):
    B, S, _ = u.shape
    groups = jnp.split(u.astype(F32), len(POOL_WINDOWS), axis=-1)
    t = jnp.arange(S, dtype=F32)
    outs = []
    for g, w in enumerate(POOL_WINDOWS):
        ug = groups[g]
        cs = jnp.cumsum(jnp.pad(ug, ((0, 0), (w, 0), (0, 0))), axis=1)
        window_sum = cs[:, w:] - cs[:, :S]
        count = jnp.minimum(t + 1.0, float(w))[None, :, None]
        pooled = window_sum / count - ug
        outs.append(jnp.einsum('bsc,ce->bse', pooled, pool_w[g].astype(F32)))
    return (jnp.concatenate(outs, axis=-1) * pool_scale.astype(F32)).astype(u.dtype)


def rg_lru_branch(u_lru, u_gelu, conv_w, conv_b, lru_wa, lru_ba, lru_wx, lru_bx, lru_lambda):
    B, S, _ = u_lru.shape
    xp = jnp.pad(u_lru.astype(F32), ((0, 0), (CONV_WIDTH - 1, 0), (0, 0)))
    xc = conv_b.astype(F32) + sum(xp[:, j:j + S] * conv_w[j].astype(F32) for j in range(CONV_WIDTH))
    xb = xc.reshape(B, S, LRU_BLOCKS, LRU_BLOCK)
    r = jax.nn.sigmoid(jnp.einsum('bshi,hij->bshj', xb, lru_wa.astype(F32)).reshape(B, S, LRU_WIDTH) + lru_ba.astype(F32))
    i = jax.nn.sigmoid(jnp.einsum('bshi,hij->bshj', xb, lru_wx.astype(F32)).reshape(B, S, LRU_WIDTH) + lru_bx.astype(F32))
    log_a = -LRU_C * r * jax.nn.softplus(-lru_lambda.astype(F32))
    a = jnp.exp(log_a)
    b = jnp.sqrt(-jnp.expm1(2.0 * log_a)) * (i * xc)

    def combine(left, right):
        a1, b1 = left
        a2, b2 = right
        return a1 * a2, a2 * b1 + b2

    _, h = lax.associative_scan(combine, (a, b), axis=1)
    return (h * jax.nn.gelu(u_gelu.astype(F32))).astype(u_lru.dtype)


def hybrid_mixer(x, cos, sin, w_in, b_gate, sinks, pool_w, pool_scale, conv_w, conv_b,
                 lru_wa, lru_ba, lru_wx, lru_bx, lru_lambda, w_branch, w_out):
    B, S, _ = x.shape
    offsets = np.cumsum(IN_SPLITS)[:-1].tolist()
    proj = jnp.einsum('bsd,de->bse', x, w_in)
    q, k, v, u_pool, u_lru, u_gelu, g_lin = jnp.split(proj, offsets, axis=-1)
    y_attn = sliding_window_attention(q.reshape(B, S, N_HEADS, HEAD_DIM),
                                      k.reshape(B, S, N_KV_HEADS, HEAD_DIM),
                                      v.reshape(B, S, N_KV_HEADS, HEAD_DIM), sinks, cos, sin)
    y_pool = multiscale_pool(u_pool, pool_w, pool_scale)
    y_lru = rg_lru_branch(u_lru, u_gelu, conv_w, conv_b, lru_wa, lru_ba, lru_wx, lru_bx, lru_lambda)
    gates = jax.nn.sigmoid((g_lin + b_gate).astype(F32)).reshape(B, S, N_BRANCH, D_MODEL)
    merged = (gates[:, :, 0] * jnp.einsum('bsc,cd->bsd', y_attn, w_branch[0])
              + gates[:, :, 1] * jnp.einsum('bsc,cd->bsd', y_pool, w_branch[1])
              + gates[:, :, 2] * jnp.einsum('bsc,cd->bsd', y_lru, w_branch[2]))
    return jnp.einsum('bsd,de->bse', merged.astype(x.dtype), w_out)


def moe_ffn(x, router_w, router_bias, w1, w3, w2, sh_w1, sh_w3, sh_w2):
    B, S, D = x.shape
    T = B * S
    xt = x.reshape(T, D)
    scores = jax.nn.sigmoid(jnp.einsum('td,de->te', xt.astype(F32), router_w.astype(F32)))
    _, top_idx = lax.top_k(scores + router_bias.astype(F32), TOP_K)
    top_s = jnp.take_along_axis(scores, top_idx, axis=-1)
    gates = top_s / top_s.sum(-1, keepdims=True) * ROUTED_SCALE
    n_assign = T * TOP_K
    e_flat = top_idx.reshape(-1).astype(jnp.int32)
    tok_flat = jnp.repeat(jnp.arange(T, dtype=jnp.int32), TOP_K)
    g_flat = gates.reshape(-1)
    order = jnp.argsort(e_flat)
    e_sorted = e_flat[order]
    tok_sorted = tok_flat[order]
    g_sorted = g_flat[order]
    counts = jnp.bincount(e_flat, length=N_EXPERTS).astype(jnp.int32)
    padded = (counts + EXPERT_BLOCK - 1) // EXPERT_BLOCK * EXPERT_BLOCK
    pad_end = jnp.cumsum(padded)
    pad_start = pad_end - padded
    raw_start = jnp.cumsum(counts) - counts
    dest = pad_start[e_sorted] + (jnp.arange(n_assign, dtype=jnp.int32) - raw_start[e_sorted])
    n_rows = -(-(n_assign + N_EXPERTS * (EXPERT_BLOCK - 1)) // EXPERT_BLOCK) * EXPERT_BLOCK
    n_blocks = n_rows // EXPERT_BLOCK
    row_tok = jnp.full((n_rows,), T, jnp.int32).at[dest].set(tok_sorted)
    row_gate = jnp.zeros((n_rows,), F32).at[dest].set(g_sorted)
    block_start = jnp.arange(n_blocks, dtype=jnp.int32) * EXPERT_BLOCK
    block_exp = jnp.minimum(jnp.searchsorted(pad_end, block_start, side='right'), N_EXPERTS - 1)
    x_pad = jnp.concatenate([xt, jnp.zeros((1, D), xt.dtype)], axis=0)

    def expert_block(y, blk):
        e, rows, g = blk
        xb = x_pad[rows]
        h = jax.nn.silu(xb @ w1[e]) * (xb @ w3[e])
        out = (h @ w2[e]).astype(F32) * g[:, None]
        return y.at[rows].add(out), None

    y, _ = lax.scan(expert_block, jnp.zeros((T + 1, D), F32),
                    (block_exp, row_tok.reshape(n_blocks, EXPERT_BLOCK), row_gate.reshape(n_blocks, EXPERT_BLOCK)))
    shared = (jax.nn.silu(xt @ sh_w1) * (xt @ sh_w3)) @ sh_w2
    return (y[:T] + shared.astype(F32)).reshape(B, S, D).astype(x.dtype)


def setup_inputs(seed: int = 0) -> dict:
    key = jax.random.key(seed)
    ks = jax.random.split(key, 28)
    L = DEPTH

    def nrm(k, shape, scale):
        return jax.random.normal(k, shape, F32) * scale

    x = jax.random.normal(ks[0], (BATCH, SEQ, D_MODEL), F32)
    offset = jax.random.randint(ks[1], (BATCH, 1), 0, 4096, dtype=jnp.int32)
    positions = offset + jnp.arange(SEQ, dtype=jnp.int32)[None, :]
    a_c = jax.random.uniform(ks[2], (L, LRU_WIDTH), F32, 0.9, 0.999)
    a_base = a_c ** (1.0 / LRU_C)
    lru_lambda = jnp.log(a_base) - jnp.log1p(-a_base)
    return {
        'x': x,
        'positions': positions,
        'w_in': nrm(ks[3], (L, D_MODEL, IN_WIDTH), D_MODEL ** -0.5),
        'b_gate': nrm(ks[4], (L, N_BRANCH * D_MODEL), 0.02),
        'sinks': nrm(ks[5], (L, N_HEADS), 0.5),
        'pool_w': nrm(ks[6], (L, len(POOL_WINDOWS), POOL_GROUP, POOL_GROUP), POOL_GROUP ** -0.5),
        'pool_scale': 1.0 + nrm(ks[7], (L, POOL_WIDTH), 0.1),
        'conv_w': nrm(ks[8], (L, CONV_WIDTH, LRU_WIDTH), CONV_WIDTH ** -0.5),
        'conv_b': nrm(ks[9], (L, LRU_WIDTH), 0.02),
        'lru_wa': nrm(ks[10], (L, LRU_BLOCKS, LRU_BLOCK, LRU_BLOCK), LRU_BLOCK ** -0.5),
        'lru_ba': nrm(ks[11], (L, LRU_WIDTH), 0.02),
        'lru_wx': nrm(ks[12], (L, LRU_BLOCKS, LRU_BLOCK, LRU_BLOCK), LRU_BLOCK ** -0.5),
        'lru_bx': nrm(ks[13], (L, LRU_WIDTH), 0.02),
        'lru_lambda': lru_lambda,
        'w_branch': nrm(ks[14], (L, N_BRANCH, BRANCH_WIDTH, D_MODEL), BRANCH_WIDTH ** -0.5),
        'w_out': nrm(ks[15], (L, D_MODEL, D_MODEL), DEEPNORM_BETA * D_MODEL ** -0.5),
        'ln1_g': 1.0 + nrm(ks[16], (L, D_MODEL), 0.05),
        'ln1_b': nrm(ks[17], (L, D_MODEL), 0.02),
        'router_w': nrm(ks[18], (L, D_MODEL, N_EXPERTS), D_MODEL ** -0.5),
        'router_bias': nrm(ks[19], (L, N_EXPERTS), 0.01),
        'exp_w1': nrm(ks[20], (L, N_EXPERTS, D_MODEL, D_EXPERT), D_MODEL ** -0.5),
        'exp_w3': nrm(ks[21], (L, N_EXPERTS, D_MODEL, D_EXPERT), D_MODEL ** -0.5),
        'exp_w2': nrm(ks[22], (L, N_EXPERTS, D_EXPERT, D_MODEL), DEEPNORM_BETA * D_EXPERT ** -0.5),
        'sh_w1': nrm(ks[23], (L, D_MODEL, D_EXPERT), D_MODEL ** -0.5),
        'sh_w3': nrm(ks[24], (L, D_MODEL, D_EXPERT), D_MODEL ** -0.5),
        'sh_w2': nrm(ks[25], (L, D_EXPERT, D_MODEL), DEEPNORM_BETA * D_EXPERT ** -0.5),
        'ln2_g': 1.0 + nrm(ks[26], (L, D_MODEL), 0.05),
        'ln2_b': nrm(ks[27], (L, D_MODEL), 0.02),
    }


def reference(x, positions, w_in, b_gate, sinks, pool_w, pool_scale, conv_w, conv_b,
              lru_wa, lru_ba, lru_wx, lru_bx, lru_lambda, w_branch, w_out, ln1_g, ln1_b,
              router_w, router_bias, exp_w1, exp_w3, exp_w2, sh_w1, sh_w3, sh_w2, ln2_g, ln2_b):
    cos, sin = rope_tables(positions)
    for l in range(DEPTH):
        h = hybrid_mixer(x, cos, sin, w_in[l], b_gate[l], sinks[l], pool_w[l], pool_scale[l],
                         conv_w[l], conv_b[l], lru_wa[l], lru_ba[l], lru_wx[l], lru_bx[l],
                         lru_lambda[l], w_branch[l], w_out[l])
        x = layer_norm(DEEPNORM_ALPHA * x + h, ln1_g[l], ln1_b[l])
        h = moe_ffn(x, router_w[l], router_bias[l], exp_w1[l], exp_w3[l], exp_w2[l],
                    sh_w1[l], sh_w3[l], sh_w2[l])
        x = layer_norm(DEEPNORM_ALPHA * x + h, ln2_g[l], ln2_b[l])
    return x
```

```python
import functools

import jax
import jax.numpy as jnp
from jax import lax
from jax.experimental import pallas as pl
from jax.experimental.pallas import tpu as pltpu

F32 = jnp.float32
BF16 = jnp.bfloat16
I32 = jnp.int32
U32 = jnp.uint32

D_MODEL = 2048
DEPTH = 2
CHUNK = 64
HEAD_DIM = 64
HALF_HEAD = HEAD_DIM // 2
N_HEADS = 16
N_KV_HEADS = 4
KV_GROUP = N_HEADS // N_KV_HEADS
ATTN_WIDTH = N_HEADS * HEAD_DIM
KV_WIDTH = N_KV_HEADS * HEAD_DIM
WINDOW_CHUNKS = 2
ROPE_THETA = 10000.0
POOL_WINDOWS = (2, 4, 8, 16)
POOL_WIDTH = 1024
POOL_GROUP = POOL_WIDTH // len(POOL_WINDOWS)
LRU_WIDTH = 1024
LRU_BLOCKS = 16
LRU_BLOCK = LRU_WIDTH // LRU_BLOCKS
CONV_WIDTH = 4
LRU_C = 8.0
N_BRANCH = 3
GATE_WIDTH = N_BRANCH * D_MODEL
IN_WIDTH = ATTN_WIDTH + 2 * KV_WIDTH + POOL_WIDTH + 2 * LRU_WIDTH + GATE_WIDTH
N_EXPERTS = 64
TOP_K = 8
D_EXPERT = 512
ROUTED_SCALE = 2.5
LN_EPS = 1e-5
DEEPNORM_ALPHA = (2 * DEPTH) ** 0.25

COL_GATE = 0
COL_Q = GATE_WIDTH
COL_POOL = COL_Q + ATTN_WIDTH
COL_LRU = COL_POOL + POOL_WIDTH
COL_GELU = COL_LRU + LRU_WIDTH
COL_K = COL_GELU + LRU_WIDTH
COL_V = COL_K + KV_WIDTH

LANES = 128
SUBLANES = 8
MXU_DIM = 256
VMEM_LIMIT_BYTES = 56 * 1024 * 1024

PROJ_TM = 512
PROJ_TN = 1536
ATTN_TQ = 256
ATTN_HALO = WINDOW_CHUNKS * CHUNK
MIX_TB = 256
POOL_HALO = 16
CONV_HALO = 8
MERGE_TM = 256
ROUTER_TM = 512
DEST_TM = 2048
DISPATCH_TM = 256
EXPERT_BM = 512
COMBINE_TM = 256
PACKED = D_MODEL // 2
NEG_BIG = -1e30


def _cparams(*sem):
    return pltpu.CompilerParams(dimension_semantics=sem, vmem_limit_bytes=VMEM_LIMIT_BYTES)


def _resident(shape, index_map):
    return pl.BlockSpec(shape, index_map, pipeline_mode=pl.Buffered(1))


def _pack_rows(lo, hi):
    lo_u = lax.bitcast_convert_type(lo.astype(BF16).astype(F32), U32)
    hi_u = lax.bitcast_convert_type(hi.astype(BF16).astype(F32), U32)
    return (lo_u >> 16) | (hi_u & jnp.uint32(0xFFFF0000))


def _unpack_rows(w):
    lo = lax.bitcast_convert_type(w << 16, F32)
    hi = lax.bitcast_convert_type(w & jnp.uint32(0xFFFF0000), F32)
    return lo, hi


def _layer_norm(y, g, b):
    mu = jnp.mean(y, axis=-1, keepdims=True)
    d = y - mu
    var = jnp.mean(d * d, axis=-1, keepdims=True)
    return d * lax.rsqrt(var + LN_EPS) * g + b


def _rope_kernel(pos_ref, inv_ref, cos_ref, sin_ref):
    ang = pos_ref[...].astype(F32) * inv_ref[...]
    lane = lax.broadcasted_iota(I32, ang.shape, 1)
    first_half = (lane % HEAD_DIM) < HALF_HEAD
    cos_ref[...] = jnp.cos(ang)
    s = jnp.sin(ang)
    sin_ref[...] = jnp.where(first_half, -s, s)


def _rope_tables(positions):
    t = positions.size
    tm = min(1024, t)
    inv_freq = 1.0 / (ROPE_THETA ** (jnp.arange(0, HEAD_DIM, 2, dtype=F32) / HEAD_DIM))
    inv = jnp.tile(inv_freq, LANES // HALF_HEAD).reshape(1, LANES)
    return pl.pallas_call(
        _rope_kernel,
        grid=(t // tm,),
        in_specs=[pl.BlockSpec((tm, 1), lambda i: (i, 0)),
                  pl.BlockSpec((1, LANES), lambda i: (0, 0))],
        out_specs=[pl.BlockSpec((tm, LANES), lambda i: (i, 0))] * 2,
        out_shape=[jax.ShapeDtypeStruct((t, LANES), F32)] * 2,
        compiler_params=_cparams("parallel"),
        name="rope_tables",
    )(positions.reshape(t, 1), inv)


def _matmul_kernel(x_ref, w_ref, o_ref):
    o_ref[...] = jnp.dot(x_ref[...], w_ref[...], preferred_element_type=F32).astype(o_ref.dtype)


def _in_proj(xb, w):
    t, d = xb.shape
    n = w.shape[1]
    tm = min(PROJ_TM, t)
    return pl.pallas_call(
        _matmul_kernel,
        grid=(n // PROJ_TN, t // tm),
        in_specs=[pl.BlockSpec((tm, d), lambda j, i: (i, 0)),
                  pl.BlockSpec((d, PROJ_TN), lambda j, i: (0, j))],
        out_specs=pl.BlockSpec((tm, PROJ_TN), lambda j, i: (i, j)),
        out_shape=jax.ShapeDtypeStruct((t, n), BF16),
        compiler_params=_cparams("parallel", "parallel"),
        name="in_proj",
    )(xb, w)


def _rope(x, cos, sin):
    w = x.shape[1]
    reps = w // LANES
    c = jnp.concatenate([cos] * reps, axis=1)
    s = jnp.concatenate([sin] * reps, axis=1)
    lane = lax.broadcasted_iota(I32, x.shape, 1)
    first_half = (lane % HEAD_DIM) < HALF_HEAD
    rot = jnp.where(first_half, pltpu.roll(x, w - HALF_HEAD, 1), pltpu.roll(x, HALF_HEAD, 1))
    return x * c + rot * s


def _attn_kernel(sinks_ref, q_ref, k_ref, v_ref, kh_ref, vh_ref, cq_ref, sq_ref, ch_ref, sh_ref,
                 o_ref, *, steps_per_seq):
    first = (pl.program_id(0) % steps_per_seq) == 0
    q = _rope(q_ref[...].astype(F32), cq_ref[...], sq_ref[...]) * (HEAD_DIM ** -0.5)
    q = q.astype(BF16)
    k_own = _rope(k_ref[...].astype(F32), cq_ref[...], sq_ref[...])
    k_halo = _rope(kh_ref[...].astype(F32), ch_ref[...], sh_ref[...])
    k_all = jnp.concatenate([k_halo, k_own], axis=0).astype(BF16)
    v_all = jnp.concatenate([vh_ref[...], v_ref[...]], axis=0)
    tq = q.shape[0]
    tk = k_all.shape[0]
    q_chunk = lax.broadcasted_iota(I32, (tq, tk), 0) // CHUNK
    k_chunk = lax.broadcasted_iota(I32, (tq, tk), 1) // CHUNK - WINDOW_CHUNKS
    oldest = jnp.where(first, 0, -WINDOW_CHUNKS)
    valid = (k_chunk <= q_chunk) & (k_chunk >= jnp.maximum(q_chunk - WINDOW_CHUNKS, oldest))
    outs = []
    for h in range(N_HEADS):
        kv = h // KV_GROUP
        qh = q[:, h * HEAD_DIM:(h + 1) * HEAD_DIM]
        kh = k_all[:, kv * HEAD_DIM:(kv + 1) * HEAD_DIM]
        vh = v_all[:, kv * HEAD_DIM:(kv + 1) * HEAD_DIM]
        s = lax.dot_general(qh, kh, (((1,), (1,)), ((), ())), preferred_element_type=F32)
        s = jnp.where(valid, s, NEG_BIG)
        sink = sinks_ref[h]
        m = jnp.maximum(jnp.max(s, axis=1, keepdims=True), sink)
        p = jnp.exp(s - m)
        denom = jnp.sum(p, axis=1, keepdims=True) + jnp.exp(sink - m)
        o = jnp.dot(p.astype(BF16), vh, preferred_element_type=F32)
        outs.append(o / denom)
    o_ref[...] = jnp.concatenate(outs, axis=1).astype(o_ref.dtype)


def _attention(proj, cos, sin, sinks, seq_len):
    t = proj.shape[0]
    tq = ATTN_TQ
    steps_per_seq = seq_len // tq
    halo_per_tq = tq // ATTN_HALO
    q_blk = COL_Q // ATTN_WIDTH
    k_blk = COL_K // KV_WIDTH
    v_blk = COL_V // KV_WIDTH

    def halo_row(i):
        return jnp.maximum(i * halo_per_tq - 1, 0)

    grid_spec = pltpu.PrefetchScalarGridSpec(
        num_scalar_prefetch=1,
        grid=(t // tq,),
        in_specs=[
            pl.BlockSpec((tq, ATTN_WIDTH), lambda i, s: (i, q_blk)),
            pl.BlockSpec((tq, KV_WIDTH), lambda i, s: (i, k_blk)),
            pl.BlockSpec((tq, KV_WIDTH), lambda i, s: (i, v_blk)),
            pl.BlockSpec((ATTN_HALO, KV_WIDTH), lambda i, s: (halo_row(i), k_blk)),
            pl.BlockSpec((ATTN_HALO, KV_WIDTH), lambda i, s: (halo_row(i), v_blk)),
            pl.BlockSpec((tq, LANES), lambda i, s: (i, 0)),
            pl.BlockSpec((tq, LANES), lambda i, s: (i, 0)),
            pl.BlockSpec((ATTN_HALO, LANES), lambda i, s: (halo_row(i), 0)),
            pl.BlockSpec((ATTN_HALO, LANES), lambda i, s: (halo_row(i), 0)),
        ],
        out_specs=pl.BlockSpec((tq, ATTN_WIDTH), lambda i, s: (i, 0)),
    )
    return pl.pallas_call(
        functools.partial(_attn_kernel, steps_per_seq=steps_per_seq),
        grid_spec=grid_spec,
        out_shape=jax.ShapeDtypeStruct((t, ATTN_WIDTH), BF16),
        compiler_params=_cparams("parallel"),
        name="swa_attention",
    )(sinks, proj, proj, proj, proj, proj, cos, sin, cos, sin)


def _mix_kernel(up_ref, ul_ref, ug_ref, pw_ref, ps_ref, cw_ref, cb_ref, wa_ref, ba_ref, wx_ref,
                bx_ref, lam_ref, yp_ref, yl_ref, halo_p, halo_l, h_carry):
    s = pl.program_id(1)
    tb = up_ref.shape[0]

    @pl.when(s == 0)
    def _():
        halo_p[...] = jnp.zeros_like(halo_p)
        halo_l[...] = jnp.zeros_like(halo_l)
        h_carry[...] = jnp.zeros_like(h_carry)

    u = up_ref[...].astype(F32)
    ext = jnp.concatenate([halo_p[...], u], axis=0)
    s2 = ext + pltpu.roll(ext, 1, 0)
    s4 = s2 + pltpu.roll(s2, 2, 0)
    s8 = s4 + pltpu.roll(s4, 4, 0)
    s16 = s8 + pltpu.roll(s8, 8, 0)
    t_idx = (s * tb + lax.broadcasted_iota(I32, (tb, 1), 0)).astype(F32)
    for g, (win, ws) in enumerate(zip(POOL_WINDOWS, (s2, s4, s8, s16))):
        cols = slice(g * POOL_GROUP, (g + 1) * POOL_GROUP)
        count = jnp.minimum(t_idx + 1.0, float(win))
        pooled = ws[POOL_HALO:, cols] / count - u[:, cols]
        y = jnp.dot(pooled.astype(BF16), pw_ref[g], preferred_element_type=F32)
        yp_ref[:, cols] = (y * ps_ref[:, cols]).astype(yp_ref.dtype)
    halo_p[...] = u[tb - POOL_HALO:, :]

    ul = ul_ref[...].astype(F32)
    extl = jnp.concatenate([halo_l[...], ul], axis=0)
    xc = cb_ref[...] + cw_ref[CONV_WIDTH - 1:CONV_WIDTH, :] * ul
    for j in range(1, CONV_WIDTH):
        w_j = cw_ref[CONV_WIDTH - 1 - j:CONV_WIDTH - j, :]
        xc = xc + w_j * pltpu.roll(extl, j, 0)[CONV_HALO:, :]
    halo_l[...] = ul[tb - CONV_HALO:, :]
    xcb = xc.astype(BF16)
    r_parts, i_parts = [], []
    for j in range(LRU_WIDTH // MXU_DIM):
        cols = slice(j * MXU_DIM, (j + 1) * MXU_DIM)
        r_parts.append(jnp.dot(xcb[:, cols], wa_ref[j], preferred_element_type=F32))
        i_parts.append(jnp.dot(xcb[:, cols], wx_ref[j], preferred_element_type=F32))
    r = jax.nn.sigmoid(jnp.concatenate(r_parts, axis=1) + ba_ref[...])
    ig = jax.nn.sigmoid(jnp.concatenate(i_parts, axis=1) + bx_ref[...])
    z = -lam_ref[...]
    softplus = jnp.maximum(z, 0.0) + jnp.log1p(jnp.exp(-jnp.abs(z)))
    log_a = (-LRU_C) * r * softplus
    a = jnp.exp(log_a)
    b = jnp.sqrt(1.0 - jnp.exp(2.0 * log_a)) * (ig * xc)
    row = lax.broadcasted_iota(I32, a.shape, 0)
    d = 1
    while d < tb:
        keep = row >= d
        a_prev = jnp.where(keep, pltpu.roll(a, d, 0), 1.0)
        b_prev = jnp.where(keep, pltpu.roll(b, d, 0), 0.0)
        b = a * b_prev + b
        a = a * a_prev
        d *= 2
    h = a * h_carry[...] + b
    h_carry[...] = h[tb - 1:tb, :]
    yl_ref[...] = (h * jax.nn.gelu(ug_ref[...].astype(F32))).astype(yl_ref.dtype)


def _block_diag(w):
    per = MXU_DIM // LRU_BLOCK
    w = w.reshape(LRU_BLOCKS // per, per, LRU_BLOCK, LRU_BLOCK)
    eye = jnp.eye(per, dtype=w.dtype)
    out = jnp.einsum("gpij,pq->gpiqj", w, eye)
    return out.reshape(LRU_BLOCKS // per, MXU_DIM, MXU_DIM)


def _mixers(proj, p, batch, seq_len):
    t = proj.shape[0]
    tb = min(MIX_TB, seq_len)
    steps = seq_len // tb

    def rows(bi, si):
        return bi * steps + si

    def col_spec(col):
        blk = col // POOL_WIDTH
        return pl.BlockSpec((tb, POOL_WIDTH), lambda bi, si: (rows(bi, si), blk))

    def const(shape):
        nd = len(shape)
        return pl.BlockSpec(shape, lambda bi, si: (0,) * nd)

    out_spec = pl.BlockSpec((tb, POOL_WIDTH), lambda bi, si: (rows(bi, si), 0))
    return pl.pallas_call(
        _mix_kernel,
        grid=(batch, steps),
        in_specs=[col_spec(COL_POOL), col_spec(COL_LRU), col_spec(COL_GELU),
                  const((len(POOL_WINDOWS), POOL_GROUP, POOL_GROUP)), const((1, POOL_WIDTH)),
                  const((CONV_WIDTH, LRU_WIDTH)), const((1, LRU_WIDTH)),
                  const((LRU_WIDTH // MXU_DIM, MXU_DIM, MXU_DIM)), const((1, LRU_WIDTH)),
                  const((LRU_WIDTH // MXU_DIM, MXU_DIM, MXU_DIM)), const((1, LRU_WIDTH)),
                  const((1, LRU_WIDTH))],
        out_specs=[out_spec, out_spec],
        out_shape=[jax.ShapeDtypeStruct((t, POOL_WIDTH), BF16),
                   jax.ShapeDtypeStruct((t, LRU_WIDTH), BF16)],
        scratch_shapes=[pltpu.VMEM((POOL_HALO, POOL_WIDTH), F32),
                        pltpu.VMEM((CONV_HALO, LRU_WIDTH), F32),
                        pltpu.VMEM((1, LRU_WIDTH), F32)],
        compiler_params=_cparams("arbitrary", "arbitrary"),
        name="pool_rglru",
    )(proj, proj, proj, p["pool_w"], p["pool_scale"], p["conv_w"], p["conv_b"],
      p["lru_wa"], p["lru_ba"], p["lru_wx"], p["lru_bx"], p["lru_lambda"])


def _merge_kernel(gl_ref, ya_ref, yp_ref, yl_ref, x_ref, bg_ref, wb_ref, wo_ref, g_ref, b_ref,
                  xo_ref, xb_ref, xp_ref):
    merged = None
    for j, y_ref in enumerate((ya_ref, yp_ref, yl_ref)):
        cols = slice(j * D_MODEL, (j + 1) * D_MODEL)
        z = jnp.dot(y_ref[...], wb_ref[j], preferred_element_type=F32)
        gate = jax.nn.sigmoid(gl_ref[:, cols].astype(F32) + bg_ref[:, cols])
        merged = gate * z if merged is None else merged + gate * z
    h = jnp.dot(merged.astype(BF16), wo_ref[...], preferred_element_type=F32)
    y = _layer_norm(DEEPNORM_ALPHA * x_ref[...] + h, g_ref[...], b_ref[...])
    xo_ref[...] = y
    xb_ref[...] = y.astype(BF16)
    xp_ref[...] = _pack_rows(y[:, :PACKED], y[:, PACKED:])


def _merge(proj, ya, yp, yl, x, p):
    t = x.shape[0]
    tm = min(MERGE_TM, t)
    row = lambda i: (i, 0)
    return pl.pallas_call(
        _merge_kernel,
        grid=(t // tm,),
        in_specs=[pl.BlockSpec((tm, GATE_WIDTH), row),
                  pl.BlockSpec((tm, ATTN_WIDTH), row),
                  pl.BlockSpec((tm, POOL_WIDTH), row),
                  pl.BlockSpec((tm, LRU_WIDTH), row),
                  pl.BlockSpec((tm, D_MODEL), row),
                  _resident((1, GATE_WIDTH), lambda i: (0, 0)),
                  _resident((N_BRANCH, ATTN_WIDTH, D_MODEL), lambda i: (0, 0, 0)),
                  _resident((D_MODEL, D_MODEL), lambda i: (0, 0)),
                  _resident((1, D_MODEL), lambda i: (0, 0)),
                  _resident((1, D_MODEL), lambda i: (0, 0))],
        out_specs=[pl.BlockSpec((tm, D_MODEL), row), pl.BlockSpec((tm, D_MODEL), row),
                   pl.BlockSpec((tm, PACKED), row)],
        out_shape=[jax.ShapeDtypeStruct((t, D_MODEL), F32),
                   jax.ShapeDtypeStruct((t, D_MODEL), BF16),
                   jax.ShapeDtypeStruct((t, PACKED), U32)],
        compiler_params=_cparams("parallel"),
        name="merge_ln",
    )(proj, ya, yp, yl, x, p["b_gate"], p["w_branch"], p["w_out"], p["ln1_g"], p["ln1_b"])


def _router_kernel(x_ref, rw_ref, rb_ref, idx_ref, gate_ref, rank_ref, cnt_ref, carry):
    @pl.when(pl.program_id(0) == 0)
    def _():
        carry[...] = jnp.zeros_like(carry)

    tm = x_ref.shape[0]
    logits = lax.dot_general(rw_ref[...], x_ref[...], (((1,), (1,)), ((), ())),
                             precision=lax.Precision.HIGHEST, preferred_element_type=F32)
    scores = jax.nn.sigmoid(logits)
    sel = scores + rb_ref[...]
    expert = lax.broadcasted_iota(I32, scores.shape, 0)
    idxs, tops = [], []
    chosen = jnp.zeros(scores.shape, F32)
    for _ in range(TOP_K):
        best = jnp.max(sel, axis=0, keepdims=True)
        ik = jnp.min(jnp.where(sel == best, expert, N_EXPERTS), axis=0, keepdims=True)
        hit = expert == ik
        tops.append(jnp.sum(jnp.where(hit, scores, 0.0), axis=0, keepdims=True))
        idxs.append(ik)
        chosen = jnp.where(hit, 1.0, chosen)
        sel = jnp.where(hit, -jnp.inf, sel)
    top = jnp.concatenate(tops, axis=0)
    idx = jnp.concatenate(idxs, axis=0)
    gate_ref[...] = top / jnp.sum(top, axis=0, keepdims=True) * ROUTED_SCALE
    idx_ref[...] = idx
    earlier = (lax.broadcasted_iota(I32, (tm, tm), 0) < lax.broadcasted_iota(I32, (tm, tm), 1))
    before = jnp.dot(chosen.astype(BF16), jnp.where(earlier, 1.0, 0.0).astype(BF16),
                     preferred_element_type=F32)
    base = before + carry[:, 0:1]
    ranks = [jnp.sum(jnp.where(expert == idxs[k], base, 0.0), axis=0, keepdims=True)
             for k in range(TOP_K)]
    rank_ref[...] = jnp.concatenate(ranks, axis=0).astype(I32)
    carry[...] = carry[...] + jnp.sum(chosen, axis=1, keepdims=True)
    cnt_ref[...] = carry[...].astype(I32)


def _router(x, rw_t, rbias):
    t = x.shape[0]
    tm = min(ROUTER_TM, t)
    tok = lambda i: (0, i)
    return pl.pallas_call(
        _router_kernel,
        grid=(t // tm,),
        in_specs=[pl.BlockSpec((tm, D_MODEL), lambda i: (i, 0)),
                  pl.BlockSpec((N_EXPERTS, D_MODEL), lambda i: (0, 0)),
                  pl.BlockSpec((N_EXPERTS, 1), lambda i: (0, 0))],
        out_specs=[pl.BlockSpec((TOP_K, tm), tok), pl.BlockSpec((TOP_K, tm), tok),
                   pl.BlockSpec((TOP_K, tm), tok),
                   pl.BlockSpec((N_EXPERTS, LANES), lambda i: (0, 0))],
        out_shape=[jax.ShapeDtypeStruct((TOP_K, t), I32), jax.ShapeDtypeStruct((TOP_K, t), F32),
                   jax.ShapeDtypeStruct((TOP_K, t), I32),
                   jax.ShapeDtypeStruct((N_EXPERTS, LANES), I32)],
        scratch_shapes=[pltpu.VMEM((N_EXPERTS, LANES), F32)],
        compiler_params=_cparams("arbitrary"),
        name="router_topk",
    )(x, rw_t, rbias)


def _dest_kernel(start_ref, idx_ref, rank_ref, dest_ref):
    idx = idx_ref[...]
    dest = rank_ref[...]
    for e in range(N_EXPERTS):
        dest = dest + jnp.where(idx == e, start_ref[e], 0)
    dest_ref[...] = dest


def _dest_rows(pad_start, idx, rank):
    t = idx.shape[1]
    tm = min(DEST_TM, t)
    spec = pl.BlockSpec((TOP_K, tm), lambda i, s: (0, i))
    return pl.pallas_call(
        _dest_kernel,
        grid_spec=pltpu.PrefetchScalarGridSpec(
            num_scalar_prefetch=1, grid=(t // tm,), in_specs=[spec, spec], out_specs=spec),
        out_shape=jax.ShapeDtypeStruct((TOP_K, t), I32),
        compiler_params=_cparams("parallel"),
        name="dest_rows",
    )(pad_start, idx, rank)


def _dispatch_kernel(fill_start_ref, fill_len_ref, tail_ref, dest_ref, x_ref, xs_ref, zeros, sem,
                     zsem, *, n_blocks):
    tm = x_ref.shape[0]
    bm = zeros.shape[0]

    def issue(t, c):
        for k in range(TOP_K):
            pltpu.make_async_copy(x_ref.at[pl.ds(t, 1)], xs_ref.at[pl.ds(dest_ref[k, t], 1)],
                                  sem).start()
        return c

    lax.fori_loop(0, tm, issue, 0)

    @pl.when(pl.program_id(0) == 0)
    def _():
        zeros[...] = jnp.zeros_like(zeros)

        def fill(act):
            def body(e, c):
                pos = fill_start_ref[e]
                n = fill_len_ref[e]

                def one_row(r, c2):
                    act(pltpu.make_async_copy(zeros.at[pl.ds(0, 1)],
                                              xs_ref.at[pl.ds(pos + r, 1)], zsem))
                    return c2

                lax.fori_loop(0, n & (SUBLANES - 1), one_row, 0)
                pos = pos + (n & (SUBLANES - 1))
                piece = SUBLANES
                while piece < bm:
                    take = (n & piece) != 0

                    @pl.when(take)
                    def _(pos=pos, piece=piece):
                        at = pl.multiple_of(pos, piece)
                        act(pltpu.make_async_copy(zeros.at[pl.ds(0, piece)],
                                                  xs_ref.at[pl.ds(at, piece)], zsem))

                    pos = pos + jnp.where(take, piece, 0)
                    piece *= 2
                return c

            lax.fori_loop(0, N_EXPERTS, body, 0)

        def tail(act):
            def body(b, c):
                at = pl.multiple_of(b * bm, bm)
                act(pltpu.make_async_copy(zeros, xs_ref.at[pl.ds(at, bm)], zsem))
                return c

            lax.fori_loop(tail_ref[0], n_blocks, body, 0)

        fill(lambda cp: cp.start())
        tail(lambda cp: cp.start())
        fill(lambda cp: cp.wait())
        tail(lambda cp: cp.wait())

    for _ in range(TOP_K):
        pltpu.make_async_copy(x_ref, xs_ref.at[pl.ds(0, tm)], sem).wait()


def _dispatch(fill_start, fill_len, n_used, dest, xp, n_blocks):
    t = xp.shape[0]
    tm = min(DISPATCH_TM, t)
    grid_spec = pltpu.PrefetchScalarGridSpec(
        num_scalar_prefetch=3,
        grid=(t // tm,),
        in_specs=[pl.BlockSpec((TOP_K, tm), lambda i, *_: (0, i), memory_space=pltpu.SMEM),
                  pl.BlockSpec((tm, PACKED), lambda i, *_: (i, 0))],
        out_specs=pl.BlockSpec(memory_space=pl.ANY),
        scratch_shapes=[pltpu.VMEM((EXPERT_BM, PACKED), U32), pltpu.SemaphoreType.DMA,
                        pltpu.SemaphoreType.DMA],
    )
    return pl.pallas_call(
        functools.partial(_dispatch_kernel, n_blocks=n_blocks),
        grid_spec=grid_spec,
        out_shape=jax.ShapeDtypeStruct((n_blocks * EXPERT_BM, PACKED), U32),
        compiler_params=_cparams("arbitrary"),
        name="moe_dispatch",
    )(fill_start, fill_len, n_used, dest, xp)


def _expert_kernel(bexp_ref, nused_ref, xs_ref, w1_ref, w3_ref, w2_ref, ys_ref, w1b, w3b, w2b):
    b = pl.program_id(0)
    used = b < nused_ref[0]
    changed = jnp.logical_or(b == 0, bexp_ref[b] != bexp_ref[jnp.maximum(b - 1, 0)])

    @pl.when(jnp.logical_and(used, changed))
    def _():
        w1b[...] = w1_ref[0].astype(BF16)
        w3b[...] = w3_ref[0].astype(BF16)
        w2b[...] = w2_ref[0].astype(BF16)

    @pl.when(used)
    def _():
        lo, hi = _unpack_rows(xs_ref[...])
        lo = lo.astype(BF16)
        hi = hi.astype(BF16)
        a = (jnp.dot(lo, w1b[:PACKED, :], preferred_element_type=F32)
             + jnp.dot(hi, w1b[PACKED:, :], preferred_element_type=F32))
        g = (jnp.dot(lo, w3b[:PACKED, :], preferred_element_type=F32)
             + jnp.dot(hi, w3b[PACKED:, :], preferred_element_type=F32))
        h = (a * jax.nn.sigmoid(a) * g).astype(BF16)
        y = jnp.dot(h, w2b[...], preferred_element_type=F32)
        ys_ref[...] = _pack_rows(y[:, :PACKED], y[:, PACKED:])

    @pl.when(jnp.logical_not(used))
    def _():
        ys_ref[...] = jnp.zeros_like(ys_ref)


def _experts(block_exp, n_used, xs, w1, w3, w2):
    n_blocks = xs.shape[0] // EXPERT_BM

    def last_used(b, nu):
        return jnp.minimum(b, nu[0] - 1)

    grid_spec = pltpu.PrefetchScalarGridSpec(
        num_scalar_prefetch=2,
        grid=(n_blocks,),
        in_specs=[
            pl.BlockSpec((EXPERT_BM, PACKED), lambda b, be, nu: (last_used(b, nu), 0)),
            pl.BlockSpec((1, D_MODEL, D_EXPERT), lambda b, be, nu: (be[last_used(b, nu)], 0, 0)),
            pl.BlockSpec((1, D_MODEL, D_EXPERT), lambda b, be, nu: (be[last_used(b, nu)], 0, 0)),
            pl.BlockSpec((1, D_EXPERT, D_MODEL), lambda b, be, nu: (be[last_used(b, nu)], 0, 0)),
        ],
        out_specs=pl.BlockSpec((EXPERT_BM, PACKED), lambda b, be, nu: (b, 0)),
        scratch_shapes=[pltpu.VMEM((D_MODEL, D_EXPERT), BF16), pltpu.VMEM((D_MODEL, D_EXPERT), BF16),
                        pltpu.VMEM((D_EXPERT, D_MODEL), BF16)],
    )
    return pl.pallas_call(
        _expert_kernel,
        grid_spec=grid_spec,
        out_shape=jax.ShapeDtypeStruct(xs.shape, U32),
        compiler_params=_cparams("arbitrary"),
        name="moe_experts",
    )(block_exp, n_used, xs, w1, w3, w2)


def _combine_kernel(dest_ref, gate_ref, x_ref, xb_ref, ys_ref, s1_ref, s3_ref, s2_ref, g_ref, b_ref,
                    xo_ref, xbo_ref, buf, sem):
    tm = x_ref.shape[0]

    def issue(t, c):
        for k in range(TOP_K):
            pltpu.make_async_copy(ys_ref.at[pl.ds(dest_ref[k, t], 1)], buf.at[k, pl.ds(t, 1)],
                                  sem).start()
        return c

    lax.fori_loop(0, tm, issue, 0)

    xb = xb_ref[...]
    a = jnp.dot(xb, s1_ref[...], preferred_element_type=F32)
    g = jnp.dot(xb, s3_ref[...], preferred_element_type=F32)
    hid = (a * jax.nn.sigmoid(a) * g).astype(BF16)
    shared = jnp.dot(hid, s2_ref[...], preferred_element_type=F32)

    for k in range(TOP_K):
        pltpu.make_async_copy(ys_ref.at[pl.ds(0, tm)], buf.at[k], sem).wait()

    acc_lo = shared[:, :PACKED]
    acc_hi = shared[:, PACKED:]
    for k in range(TOP_K):
        lo, hi = _unpack_rows(buf[k])
        gk = gate_ref[:, k:k + 1]
        acc_lo = acc_lo + gk * lo
        acc_hi = acc_hi + gk * hi
    h = jnp.concatenate([acc_lo, acc_hi], axis=1)
    y = _layer_norm(DEEPNORM_ALPHA * x_ref[...] + h, g_ref[...], b_ref[...])
    xo_ref[...] = y
    xbo_ref[...] = y.astype(BF16)


def _combine(dest, gate_t, x, xb, ys, p):
    t = x.shape[0]
    tm = min(COMBINE_TM, t)
    row = lambda i: (i, 0)
    return pl.pallas_call(
        _combine_kernel,
        grid=(t // tm,),
        in_specs=[pl.BlockSpec((TOP_K, tm), lambda i: (0, i), memory_space=pltpu.SMEM),
                  pl.BlockSpec((tm, TOP_K), row),
                  pl.BlockSpec((tm, D_MODEL), row),
                  pl.BlockSpec((tm, D_MODEL), row),
                  pl.BlockSpec(memory_space=pl.ANY),
                  _resident((D_MODEL, D_EXPERT), lambda i: (0, 0)),
                  _resident((D_MODEL, D_EXPERT), lambda i: (0, 0)),
                  _resident((D_EXPERT, D_MODEL), lambda i: (0, 0)),
                  _resident((1, D_MODEL), lambda i: (0, 0)),
                  _resident((1, D_MODEL), lambda i: (0, 0))],
        out_specs=[pl.BlockSpec((tm, D_MODEL), row), pl.BlockSpec((tm, D_MODEL), row)],
        out_shape=[jax.ShapeDtypeStruct((t, D_MODEL), F32),
                   jax.ShapeDtypeStruct((t, D_MODEL), BF16)],
        scratch_shapes=[pltpu.VMEM((TOP_K, tm, PACKED), U32), pltpu.SemaphoreType.DMA],
        compiler_params=_cparams("arbitrary"),
        name="moe_combine_ln",
    )(dest, gate_t, x, xb, ys, p["sh_w1"], p["sh_w3"], p["sh_w2"], p["ln2_g"], p["ln2_b"])


def _n_expert_blocks(t):
    return -(-(t * TOP_K + N_EXPERTS * (EXPERT_BM - 1)) // EXPERT_BM)


def _moe(x, xb, xp, p):
    t = x.shape[0]
    n_blocks = _n_expert_blocks(t)
    idx, gate, rank, cnt = _router(x, p["router_w_t"], p["router_bias"])
    counts = cnt[:, 0]
    padded = (counts + EXPERT_BM - 1) // EXPERT_BM * EXPERT_BM
    pad_end = jnp.cumsum(padded)
    pad_start = pad_end - padded
    n_used = (pad_end[-1:] // EXPERT_BM).astype(I32)
    block_first = jnp.arange(n_blocks, dtype=I32) * EXPERT_BM
    block_exp = jnp.minimum(jnp.searchsorted(pad_end, block_first, side="right"),
                            N_EXPERTS - 1).astype(I32)
    dest = _dest_rows(pad_start.astype(I32), idx, rank)
    xs = _dispatch((pad_start + counts).astype(I32), (padded - counts).astype(I32), n_used, dest, xp,
                   n_blocks)
    ys = _experts(block_exp, n_used, xs, p["exp_w1"], p["exp_w3"], p["exp_w2"])
    return _combine(dest, gate.T, x, xb, ys, p)


def _layer_params(l, w_in, b_gate, sinks, pool_w, pool_scale, conv_w, conv_b, lru_wa, lru_ba, lru_wx,
                  lru_bx, lru_lambda, w_branch, w_out, ln1_g, ln1_b, router_w, router_bias, exp_w1,
                  exp_w3, exp_w2, sh_w1, sh_w3, sh_w2, ln2_g, ln2_b):
    q0 = 0
    k0 = q0 + ATTN_WIDTH
    v0 = k0 + KV_WIDTH
    p0 = v0 + KV_WIDTH
    l0 = p0 + POOL_WIDTH
    g0 = l0 + LRU_WIDTH
    gate0 = g0 + LRU_WIDTH
    w = w_in[l]
    w_perm = jnp.concatenate([w[:, gate0:], w[:, q0:k0], w[:, p0:l0], w[:, l0:g0], w[:, g0:gate0],
                              w[:, k0:v0], w[:, v0:p0]], axis=1).astype(BF16)
    row = lambda a: a[l].reshape(1, -1).astype(F32)
    return dict(
        w_in=w_perm, b_gate=row(b_gate), sinks=sinks[l].astype(F32),
        pool_w=pool_w[l].astype(BF16), pool_scale=row(pool_scale),
        conv_w=conv_w[l].astype(F32), conv_b=row(conv_b),
        lru_wa=_block_diag(lru_wa[l]).astype(BF16), lru_ba=row(lru_ba),
        lru_wx=_block_diag(lru_wx[l]).astype(BF16), lru_bx=row(lru_bx),
        lru_lambda=row(lru_lambda),
        w_branch=w_branch[l].astype(BF16), w_out=w_out[l].astype(BF16),
        ln1_g=row(ln1_g), ln1_b=row(ln1_b),
        router_w_t=router_w[l].T.astype(F32), router_bias=router_bias[l].reshape(-1, 1).astype(F32),
        exp_w1=exp_w1[l], exp_w3=exp_w3[l], exp_w2=exp_w2[l],
        sh_w1=sh_w1[l].astype(BF16), sh_w3=sh_w3[l].astype(BF16), sh_w2=sh_w2[l].astype(BF16),
        ln2_g=row(ln2_g), ln2_b=row(ln2_b),
    )


def kernel(x, positions, w_in, b_gate, sinks, pool_w, pool_scale, conv_w, conv_b, lru_wa, lru_ba, lru_wx, lru_bx, lru_lambda, w_branch, w_out, ln1_g, ln1_b, router_w, router_bias, exp_w1, exp_w3, exp_w2, sh_w1, sh_w3, sh_w2, ln2_g, ln2_b):
    batch, seq_len, d = x.shape
    assert d == D_MODEL and seq_len % max(ATTN_TQ, MIX_TB) == 0
    t = batch * seq_len
    cos, sin = _rope_tables(positions)
    xf = x.reshape(t, d).astype(F32)
    xb = xf.astype(BF16)
    for l in range(DEPTH):
        p = _layer_params(l, w_in, b_gate, sinks, pool_w, pool_scale, conv_w, conv_b, lru_wa, lru_ba,
                          lru_wx, lru_bx, lru_lambda, w_branch, w_out, ln1_g, ln1_b, router_w,
                          router_bias, exp_w1, exp_w3, exp_w2, sh_w1, sh_w3, sh_w2, ln2_g, ln2_b)
        proj = _in_proj(xb, p["w_in"])
        ya = _attention(proj, cos, sin, p["sinks"], seq_len)
        yp, yl = _mixers(proj, p, batch, seq_len)
        xf, xb, xp = _merge(proj, ya, yp, yl, xf, p)
        xf, xb = _moe(xf, xb, xp, p)
    return xf.reshape(batch, seq_len, d).astype(x.dtype)
```

```python
import functools

import jax
import jax.numpy as jnp
from jax import lax
from jax.experimental import pallas as pl
from jax.experimental.pallas import tpu as pltpu

F32 = jnp.float32
BF16 = jnp.bfloat16
I32 = jnp.int32
U32 = jnp.uint32

D_MODEL = 2048
DEPTH = 2
CHUNK = 64
HEAD_DIM = 64
HALF_HEAD = HEAD_DIM // 2
N_HEADS = 16
N_KV_HEADS = 4
KV_GROUP = N_HEADS // N_KV_HEADS
ATTN_WIDTH = N_HEADS * HEAD_DIM
KV_WIDTH = N_KV_HEADS * HEAD_DIM
WINDOW_CHUNKS = 2
ROPE_THETA = 10000.0
POOL_WINDOWS = (2, 4, 8, 16)
POOL_WIDTH = 1024
POOL_GROUP = POOL_WIDTH // len(POOL_WINDOWS)
LRU_WIDTH = 1024
LRU_BLOCKS = 16
LRU_BLOCK = LRU_WIDTH // LRU_BLOCKS
CONV_WIDTH = 4
LRU_C = 8.0
N_BRANCH = 3
GATE_WIDTH = N_BRANCH * D_MODEL
IN_WIDTH = ATTN_WIDTH + 2 * KV_WIDTH + POOL_WIDTH + 2 * LRU_WIDTH + GATE_WIDTH
N_EXPERTS = 64
TOP_K = 8
D_EXPERT = 512
ROUTED_SCALE = 2.5
LN_EPS = 1e-5
DEEPNORM_ALPHA = (2 * DEPTH) ** 0.25

COL_GATE = 0
COL_Q = GATE_WIDTH
COL_POOL = COL_Q + ATTN_WIDTH
COL_LRU = COL_POOL + POOL_WIDTH
COL_GELU = COL_LRU + LRU_WIDTH
COL_K = COL_GELU + LRU_WIDTH
COL_V = COL_K + KV_WIDTH

LANES = 128
SUBLANES = 8
MXU_DIM = 256
VMEM_LIMIT_BYTES = 56 * 1024 * 1024

PROJ_TM = 512
PROJ_TN = 1536
ATTN_TQ = 256
ATTN_HALO = WINDOW_CHUNKS * CHUNK
MIX_TB = 256
POOL_HALO = 16
CONV_HALO = 8
MERGE_TM = 256
ROUTER_TM = 512
DEST_TM = 2048
DISPATCH_TM = 256
EXPERT_BM = 512
COMBINE_TM = 256
PACKED = D_MODEL // 2
ROW_TILE = PACKED // LANES
ISSUE_UNROLL = 2
PROJ_SRC = 512
NEG_BIG = -1e30

assert ROW_TILE == SUBLANES


def _cparams(*sem):
    return pltpu.CompilerParams(dimension_semantics=sem, vmem_limit_bytes=VMEM_LIMIT_BYTES)


def _resident(shape, index_map):
    return pl.BlockSpec(shape, index_map, pipeline_mode=pl.Buffered(1))


def _pack_rows(lo, hi):
    lo_u = lax.bitcast_convert_type(lo.astype(BF16).astype(F32), U32)
    hi_u = lax.bitcast_convert_type(hi.astype(BF16).astype(F32), U32)
    return (lo_u >> 16) | (hi_u & jnp.uint32(0xFFFF0000))


def _unpack_rows(w):
    lo = lax.bitcast_convert_type(w << 16, F32)
    hi = lax.bitcast_convert_type(w & jnp.uint32(0xFFFF0000), F32)
    return lo, hi


def _store_row_tiles(ref, y):
    m = y.shape[0]
    for j in range(ROW_TILE):
        lo = y[:, j * LANES:(j + 1) * LANES]
        hi = y[:, PACKED + j * LANES:PACKED + (j + 1) * LANES]
        ref[pl.ds(j, m, stride=ROW_TILE), :] = _pack_rows(lo, hi)


def _load_row_tile_chunk(ref, j, m):
    return _unpack_rows(ref[pl.ds(j, m, stride=ROW_TILE), :])


def _row_tile(ref, r):
    return ref.at[pl.ds(pl.multiple_of(r * ROW_TILE, ROW_TILE), ROW_TILE)]


def _layer_norm(y, g, b):
    mu = jnp.mean(y, axis=-1, keepdims=True)
    d = y - mu
    var = jnp.mean(d * d, axis=-1, keepdims=True)
    return d * lax.rsqrt(var + LN_EPS) * g + b


def _rope_kernel(pos_ref, inv_ref, cos_ref, sin_ref):
    ang = pos_ref[...].astype(F32) * inv_ref[...]
    lane = lax.broadcasted_iota(I32, ang.shape, 1)
    first_half = (lane % HEAD_DIM) < HALF_HEAD
    cos_ref[...] = jnp.cos(ang)
    s = jnp.sin(ang)
    sin_ref[...] = jnp.where(first_half, -s, s)


def _rope_tables(positions):
    t = positions.size
    tm = min(1024, t)
    inv_freq = 1.0 / (ROPE_THETA ** (jnp.arange(0, HEAD_DIM, 2, dtype=F32) / HEAD_DIM))
    inv = jnp.tile(inv_freq, LANES // HALF_HEAD).reshape(1, LANES)
    return pl.pallas_call(
        _rope_kernel,
        grid=(t // tm,),
        in_specs=[pl.BlockSpec((tm, 1), lambda i: (i, 0)),
                  pl.BlockSpec((1, LANES), lambda i: (0, 0))],
        out_specs=[pl.BlockSpec((tm, LANES), lambda i: (i, 0))] * 2,
        out_shape=[jax.ShapeDtypeStruct((t, LANES), F32)] * 2,
        compiler_params=_cparams("parallel"),
        name="rope_tables",
    )(positions.reshape(t, 1), inv)


def _proj_source_blocks():
    ref_order = (("q", ATTN_WIDTH), ("k", KV_WIDTH), ("v", KV_WIDTH), ("pool", POOL_WIDTH),
                 ("lru", LRU_WIDTH), ("gelu", LRU_WIDTH), ("gate", GATE_WIDTH))
    start, col = {}, 0
    for name, width in ref_order:
        start[name] = col
        col += width
    assert start["v"] == start["k"] + KV_WIDTH and 2 * KV_WIDTH == PROJ_SRC
    out_order = (("gate", GATE_WIDTH), ("q", ATTN_WIDTH), ("pool", POOL_WIDTH), ("lru", LRU_WIDTH),
                 ("gelu", LRU_WIDTH), ("k", 2 * KV_WIDTH))
    blocks = []
    for name, width in out_order:
        assert start[name] % PROJ_SRC == 0 and width % PROJ_SRC == 0
        blocks += [start[name] // PROJ_SRC + c for c in range(width // PROJ_SRC)]
    return blocks


def _in_proj_kernel(src_ref, x_ref, *refs):
    w_refs, o_ref, wb = refs[:-2], refs[-2], refs[-1]

    @pl.when(pl.program_id(1) == 0)
    def _():
        for c, w_ref in enumerate(w_refs):
            wb[:, c * PROJ_SRC:(c + 1) * PROJ_SRC] = w_ref[0].astype(BF16)

    o_ref[...] = jnp.dot(x_ref[...], wb[...], preferred_element_type=F32).astype(o_ref.dtype)


def _in_proj(xb, w_in, layer):
    t, d = xb.shape
    n = w_in.shape[2]
    tm = min(PROJ_TM, t)
    per = PROJ_TN // PROJ_SRC
    src = jnp.asarray(_proj_source_blocks(), I32)

    def w_spec(c):
        return pl.BlockSpec((1, d, PROJ_SRC), lambda j, i, s: (layer, 0, s[per * j + c]))

    grid_spec = pltpu.PrefetchScalarGridSpec(
        num_scalar_prefetch=1,
        grid=(n // PROJ_TN, t // tm),
        in_specs=[pl.BlockSpec((tm, d), lambda j, i, s: (i, 0))] + [w_spec(c) for c in range(per)],
        out_specs=pl.BlockSpec((tm, PROJ_TN), lambda j, i, s: (i, j)),
        scratch_shapes=[pltpu.VMEM((d, PROJ_TN), BF16)],
    )
    return pl.pallas_call(
        _in_proj_kernel,
        grid_spec=grid_spec,
        out_shape=jax.ShapeDtypeStruct((t, n), BF16),
        compiler_params=_cparams("arbitrary", "arbitrary"),
        name="in_proj",
    )(src, xb, *([w_in] * per))


def _rope(x, cos, sin):
    w = x.shape[1]
    reps = w // LANES
    c = jnp.concatenate([cos] * reps, axis=1)
    s = jnp.concatenate([sin] * reps, axis=1)
    lane = lax.broadcasted_iota(I32, x.shape, 1)
    first_half = (lane % HEAD_DIM) < HALF_HEAD
    rot = jnp.where(first_half, pltpu.roll(x, w - HALF_HEAD, 1), pltpu.roll(x, HALF_HEAD, 1))
    return x * c + rot * s


def _attn_kernel(sinks_ref, q_ref, k_ref, v_ref, kh_ref, vh_ref, cq_ref, sq_ref, ch_ref, sh_ref,
                 o_ref, *, steps_per_seq):
    first = (pl.program_id(0) % steps_per_seq) == 0
    q = _rope(q_ref[...].astype(F32), cq_ref[...], sq_ref[...]) * (HEAD_DIM ** -0.5)
    q = q.astype(BF16)
    k_own = _rope(k_ref[...].astype(F32), cq_ref[...], sq_ref[...])
    k_halo = _rope(kh_ref[...].astype(F32), ch_ref[...], sh_ref[...])
    k_all = jnp.concatenate([k_halo, k_own], axis=0).astype(BF16)
    v_all = jnp.concatenate([vh_ref[...], v_ref[...]], axis=0)
    tq = q.shape[0]
    tk = k_all.shape[0]
    q_chunk = lax.broadcasted_iota(I32, (tq, tk), 0) // CHUNK
    k_chunk = lax.broadcasted_iota(I32, (tq, tk), 1) // CHUNK - WINDOW_CHUNKS
    oldest = jnp.where(first, 0, -WINDOW_CHUNKS)
    valid = (k_chunk <= q_chunk) & (k_chunk >= jnp.maximum(q_chunk - WINDOW_CHUNKS, oldest))
    outs = []
    for h in range(N_HEADS):
        kv = h // KV_GROUP
        qh = q[:, h * HEAD_DIM:(h + 1) * HEAD_DIM]
        kh = k_all[:, kv * HEAD_DIM:(kv + 1) * HEAD_DIM]
        vh = v_all[:, kv * HEAD_DIM:(kv + 1) * HEAD_DIM]
        s = lax.dot_general(qh, kh, (((1,), (1,)), ((), ())), preferred_element_type=F32)
        s = jnp.where(valid, s, NEG_BIG)
        sink = sinks_ref[h]
        m = jnp.maximum(jnp.max(s, axis=1, keepdims=True), sink)
        p = jnp.exp(s - m)
        denom = jnp.sum(p, axis=1, keepdims=True) + jnp.exp(sink - m)
        o = jnp.dot(p.astype(BF16), vh, preferred_element_type=F32)
        outs.append(o / denom)
    o_ref[...] = jnp.concatenate(outs, axis=1).astype(o_ref.dtype)


def _attention(proj, cos, sin, sinks, seq_len):
    t = proj.shape[0]
    tq = ATTN_TQ
    steps_per_seq = seq_len // tq
    halo_per_tq = tq // ATTN_HALO
    q_blk = COL_Q // ATTN_WIDTH
    k_blk = COL_K // KV_WIDTH
    v_blk = COL_V // KV_WIDTH

    def halo_row(i):
        return jnp.maximum(i * halo_per_tq - 1, 0)

    grid_spec = pltpu.PrefetchScalarGridSpec(
        num_scalar_prefetch=1,
        grid=(t // tq,),
        in_specs=[
            pl.BlockSpec((tq, ATTN_WIDTH), lambda i, s: (i, q_blk)),
            pl.BlockSpec((tq, KV_WIDTH), lambda i, s: (i, k_blk)),
            pl.BlockSpec((tq, KV_WIDTH), lambda i, s: (i, v_blk)),
            pl.BlockSpec((ATTN_HALO, KV_WIDTH), lambda i, s: (halo_row(i), k_blk)),
            pl.BlockSpec((ATTN_HALO, KV_WIDTH), lambda i, s: (halo_row(i), v_blk)),
            pl.BlockSpec((tq, LANES), lambda i, s: (i, 0)),
            pl.BlockSpec((tq, LANES), lambda i, s: (i, 0)),
            pl.BlockSpec((ATTN_HALO, LANES), lambda i, s: (halo_row(i), 0)),
            pl.BlockSpec((ATTN_HALO, LANES), lambda i, s: (halo_row(i), 0)),
        ],
        out_specs=pl.BlockSpec((tq, ATTN_WIDTH), lambda i, s: (i, 0)),
    )
    return pl.pallas_call(
        functools.partial(_attn_kernel, steps_per_seq=steps_per_seq),
        grid_spec=grid_spec,
        out_shape=jax.ShapeDtypeStruct((t, ATTN_WIDTH), BF16),
        compiler_params=_cparams("parallel"),
        name="swa_attention",
    )(sinks, proj, proj, proj, proj, proj, cos, sin, cos, sin)


def _mix_kernel(up_ref, ul_ref, ug_ref, pw_ref, ps_ref, cw_ref, cb_ref, wa_ref, ba_ref, wx_ref,
                bx_ref, lam_ref, yp_ref, yl_ref, halo_p, halo_l, h_carry):
    s = pl.program_id(1)
    tb = up_ref.shape[0]

    @pl.when(s == 0)
    def _():
        halo_p[...] = jnp.zeros_like(halo_p)
        halo_l[...] = jnp.zeros_like(halo_l)
        h_carry[...] = jnp.zeros_like(h_carry)

    u = up_ref[...].astype(F32)
    ext = jnp.concatenate([halo_p[...], u], axis=0)
    s2 = ext + pltpu.roll(ext, 1, 0)
    s4 = s2 + pltpu.roll(s2, 2, 0)
    s8 = s4 + pltpu.roll(s4, 4, 0)
    s16 = s8 + pltpu.roll(s8, 8, 0)
    t_idx = (s * tb + lax.broadcasted_iota(I32, (tb, 1), 0)).astype(F32)
    for g, (win, ws) in enumerate(zip(POOL_WINDOWS, (s2, s4, s8, s16))):
        cols = slice(g * POOL_GROUP, (g + 1) * POOL_GROUP)
        count = jnp.minimum(t_idx + 1.0, float(win))
        pooled = ws[POOL_HALO:, cols] / count - u[:, cols]
        y = jnp.dot(pooled.astype(BF16), pw_ref[g], preferred_element_type=F32)
        yp_ref[:, cols] = (y * ps_ref[:, cols]).astype(yp_ref.dtype)
    halo_p[...] = u[tb - POOL_HALO:, :]

    ul = ul_ref[...].astype(F32)
    extl = jnp.concatenate([halo_l[...], ul], axis=0)
    xc = cb_ref[...] + cw_ref[CONV_WIDTH - 1:CONV_WIDTH, :] * ul
    for j in range(1, CONV_WIDTH):
        w_j = cw_ref[CONV_WIDTH - 1 - j:CONV_WIDTH - j, :]
        xc = xc + w_j * pltpu.roll(extl, j, 0)[CONV_HALO:, :]
    halo_l[...] = ul[tb - CONV_HALO:, :]
    xcb = xc.astype(BF16)
    r_parts, i_parts = [], []
    for j in range(LRU_WIDTH // MXU_DIM):
        cols = slice(j * MXU_DIM, (j + 1) * MXU_DIM)
        r_parts.append(jnp.dot(xcb[:, cols], wa_ref[j], preferred_element_type=F32))
        i_parts.append(jnp.dot(xcb[:, cols], wx_ref[j], preferred_element_type=F32))
    r = jax.nn.sigmoid(jnp.concatenate(r_parts, axis=1) + ba_ref[...])
    ig = jax.nn.sigmoid(jnp.concatenate(i_parts, axis=1) + bx_ref[...])
    z = -lam_ref[...]
    softplus = jnp.maximum(z, 0.0) + jnp.log1p(jnp.exp(-jnp.abs(z)))
    log_a = (-LRU_C) * r * softplus
    a = jnp.exp(log_a)
    b = jnp.sqrt(1.0 - jnp.exp(2.0 * log_a)) * (ig * xc)
    row = lax.broadcasted_iota(I32, a.shape, 0)
    d = 1
    while d < tb:
        keep = row >= d
        a_prev = jnp.where(keep, pltpu.roll(a, d, 0), 1.0)
        b_prev = jnp.where(keep, pltpu.roll(b, d, 0), 0.0)
        b = a * b_prev + b
        a = a * a_prev
        d *= 2
    h = a * h_carry[...] + b
    h_carry[...] = h[tb - 1:tb, :]
    yl_ref[...] = (h * jax.nn.gelu(ug_ref[...].astype(F32))).astype(yl_ref.dtype)


def _block_diag(w):
    per = MXU_DIM // LRU_BLOCK
    w = w.reshape(LRU_BLOCKS // per, per, LRU_BLOCK, LRU_BLOCK)
    eye = jnp.eye(per, dtype=w.dtype)
    out = jnp.einsum("gpij,pq->gpiqj", w, eye)
    return out.reshape(LRU_BLOCKS // per, MXU_DIM, MXU_DIM)


def _mixers(proj, p, batch, seq_len):
    t = proj.shape[0]
    tb = min(MIX_TB, seq_len)
    steps = seq_len // tb

    def rows(bi, si):
        return bi * steps + si

    def col_spec(col):
        blk = col // POOL_WIDTH
        return pl.BlockSpec((tb, POOL_WIDTH), lambda bi, si: (rows(bi, si), blk))

    def const(shape):
        nd = len(shape)
        return pl.BlockSpec(shape, lambda bi, si: (0,) * nd)

    out_spec = pl.BlockSpec((tb, POOL_WIDTH), lambda bi, si: (rows(bi, si), 0))
    return pl.pallas_call(
        _mix_kernel,
        grid=(batch, steps),
        in_specs=[col_spec(COL_POOL), col_spec(COL_LRU), col_spec(COL_GELU),
                  const((len(POOL_WINDOWS), POOL_GROUP, POOL_GROUP)), const((1, POOL_WIDTH)),
                  const((CONV_WIDTH, LRU_WIDTH)), const((1, LRU_WIDTH)),
                  const((LRU_WIDTH // MXU_DIM, MXU_DIM, MXU_DIM)), const((1, LRU_WIDTH)),
                  const((LRU_WIDTH // MXU_DIM, MXU_DIM, MXU_DIM)), const((1, LRU_WIDTH)),
                  const((1, LRU_WIDTH))],
        out_specs=[out_spec, out_spec],
        out_shape=[jax.ShapeDtypeStruct((t, POOL_WIDTH), BF16),
                   jax.ShapeDtypeStruct((t, LRU_WIDTH), BF16)],
        scratch_shapes=[pltpu.VMEM((POOL_HALO, POOL_WIDTH), F32),
                        pltpu.VMEM((CONV_HALO, LRU_WIDTH), F32),
                        pltpu.VMEM((1, LRU_WIDTH), F32)],
        compiler_params=_cparams("arbitrary", "arbitrary"),
        name="pool_rglru",
    )(proj, proj, proj, p["pool_w"], p["pool_scale"], p["conv_w"], p["conv_b"],
      p["lru_wa"], p["lru_ba"], p["lru_wx"], p["lru_bx"], p["lru_lambda"])


def _merge_kernel(gl_ref, ya_ref, yp_ref, yl_ref, x_ref, bg_ref, wb_ref, wo_ref, g_ref, b_ref,
                  xo_ref, xb_ref, xp_ref):
    merged = None
    for j, y_ref in enumerate((ya_ref, yp_ref, yl_ref)):
        cols = slice(j * D_MODEL, (j + 1) * D_MODEL)
        z = jnp.dot(y_ref[...], wb_ref[j], preferred_element_type=F32)
        gate = jax.nn.sigmoid(gl_ref[:, cols].astype(F32) + bg_ref[:, cols])
        merged = gate * z if merged is None else merged + gate * z
    h = jnp.dot(merged.astype(BF16), wo_ref[...], preferred_element_type=F32)
    y = _layer_norm(DEEPNORM_ALPHA * x_ref[...] + h, g_ref[...], b_ref[...])
    xo_ref[...] = y
    xb_ref[...] = y.astype(BF16)
    _store_row_tiles(xp_ref, y)


def _merge(proj, ya, yp, yl, x, p):
    t = x.shape[0]
    tm = min(MERGE_TM, t)
    row = lambda i: (i, 0)
    return pl.pallas_call(
        _merge_kernel,
        grid=(t // tm,),
        in_specs=[pl.BlockSpec((tm, GATE_WIDTH), row),
                  pl.BlockSpec((tm, ATTN_WIDTH), row),
                  pl.BlockSpec((tm, POOL_WIDTH), row),
                  pl.BlockSpec((tm, LRU_WIDTH), row),
                  pl.BlockSpec((tm, D_MODEL), row),
                  _resident((1, GATE_WIDTH), lambda i: (0, 0)),
                  _resident((N_BRANCH, ATTN_WIDTH, D_MODEL), lambda i: (0, 0, 0)),
                  _resident((D_MODEL, D_MODEL), lambda i: (0, 0)),
                  _resident((1, D_MODEL), lambda i: (0, 0)),
                  _resident((1, D_MODEL), lambda i: (0, 0))],
        out_specs=[pl.BlockSpec((tm, D_MODEL), row), pl.BlockSpec((tm, D_MODEL), row),
                   pl.BlockSpec((tm * ROW_TILE, LANES), row)],
        out_shape=[jax.ShapeDtypeStruct((t, D_MODEL), F32),
                   jax.ShapeDtypeStruct((t, D_MODEL), BF16),
                   jax.ShapeDtypeStruct((t * ROW_TILE, LANES), U32)],
        compiler_params=_cparams("parallel"),
        name="merge_ln",
    )(proj, ya, yp, yl, x, p["b_gate"], p["w_branch"], p["w_out"], p["ln1_g"], p["ln1_b"])


def _router_kernel(x_ref, rw_ref, rb_ref, idx_ref, gate_ref, rank_ref, cnt_ref, carry):
    @pl.when(pl.program_id(0) == 0)
    def _():
        carry[...] = jnp.zeros_like(carry)

    tm = x_ref.shape[0]
    logits = lax.dot_general(rw_ref[...], x_ref[...], (((1,), (1,)), ((), ())),
                             precision=lax.Precision.HIGHEST, preferred_element_type=F32)
    scores = jax.nn.sigmoid(logits)
    sel = scores + rb_ref[...]
    expert = lax.broadcasted_iota(I32, scores.shape, 0)
    idxs, tops = [], []
    chosen = jnp.zeros(scores.shape, F32)
    for _ in range(TOP_K):
        best = jnp.max(sel, axis=0, keepdims=True)
        ik = jnp.min(jnp.where(sel == best, expert, N_EXPERTS), axis=0, keepdims=True)
        hit = expert == ik
        tops.append(jnp.sum(jnp.where(hit, scores, 0.0), axis=0, keepdims=True))
        idxs.append(ik)
        chosen = jnp.where(hit, 1.0, chosen)
        sel = jnp.where(hit, -jnp.inf, sel)
    top = jnp.concatenate(tops, axis=0)
    idx = jnp.concatenate(idxs, axis=0)
    gate_ref[...] = top / jnp.sum(top, axis=0, keepdims=True) * ROUTED_SCALE
    idx_ref[...] = idx
    earlier = (lax.broadcasted_iota(I32, (tm, tm), 0) < lax.broadcasted_iota(I32, (tm, tm), 1))
    before = jnp.dot(chosen.astype(BF16), jnp.where(earlier, 1.0, 0.0).astype(BF16),
                     preferred_element_type=F32)
    base = before + carry[:, 0:1]
    ranks = [jnp.sum(jnp.where(expert == idxs[k], base, 0.0), axis=0, keepdims=True)
             for k in range(TOP_K)]
    rank_ref[...] = jnp.concatenate(ranks, axis=0).astype(I32)
    carry[...] = carry[...] + jnp.sum(chosen, axis=1, keepdims=True)
    cnt_ref[...] = carry[...].astype(I32)


def _router(x, rw_t, rbias):
    t = x.shape[0]
    tm = min(ROUTER_TM, t)
    tok = lambda i: (0, i)
    return pl.pallas_call(
        _router_kernel,
        grid=(t // tm,),
        in_specs=[pl.BlockSpec((tm, D_MODEL), lambda i: (i, 0)),
                  pl.BlockSpec((N_EXPERTS, D_MODEL), lambda i: (0, 0)),
                  pl.BlockSpec((N_EXPERTS, 1), lambda i: (0, 0))],
        out_specs=[pl.BlockSpec((TOP_K, tm), tok), pl.BlockSpec((TOP_K, tm), tok),
                   pl.BlockSpec((TOP_K, tm), tok),
                   pl.BlockSpec((N_EXPERTS, LANES), lambda i: (0, 0))],
        out_shape=[jax.ShapeDtypeStruct((TOP_K, t), I32), jax.ShapeDtypeStruct((TOP_K, t), F32),
                   jax.ShapeDtypeStruct((TOP_K, t), I32),
                   jax.ShapeDtypeStruct((N_EXPERTS, LANES), I32)],
        scratch_shapes=[pltpu.VMEM((N_EXPERTS, LANES), F32)],
        compiler_params=_cparams("arbitrary"),
        name="router_topk",
    )(x, rw_t, rbias)


def _dest_kernel(start_ref, idx_ref, rank_ref, dest_ref):
    idx = idx_ref[...]
    dest = rank_ref[...]
    for e in range(N_EXPERTS):
        dest = dest + jnp.where(idx == e, start_ref[e], 0)
    dest_ref[...] = dest


def _dest_rows(pad_start, idx, rank):
    t = idx.shape[1]
    tm = min(DEST_TM, t)
    spec = pl.BlockSpec((TOP_K, tm), lambda i, s: (0, i))
    return pl.pallas_call(
        _dest_kernel,
        grid_spec=pltpu.PrefetchScalarGridSpec(
            num_scalar_prefetch=1, grid=(t // tm,), in_specs=[spec, spec], out_specs=spec),
        out_shape=jax.ShapeDtypeStruct((TOP_K, t), I32),
        compiler_params=_cparams("parallel"),
        name="dest_rows",
    )(pad_start, idx, rank)


def _swiglu(xb, w1_ref, w3_ref, w2_ref):
    a = jnp.dot(xb, w1_ref[...], preferred_element_type=F32)
    g = jnp.dot(xb, w3_ref[...], preferred_element_type=F32)
    hid = (a * jax.nn.sigmoid(a) * g).astype(BF16)
    return jnp.dot(hid, w2_ref[...], preferred_element_type=F32)


def _dispatch_kernel(fill_start_ref, fill_len_ref, tail_ref, dest_ref, xp_ref, xb_ref, s1_ref, s3_ref,
                     s2_ref, xs_ref, sh_ref, zeros, sem, zsem, *, n_blocks):
    tm = xb_ref.shape[0]
    bm = zeros.shape[0] // ROW_TILE

    def issue(i, c):
        for u in range(ISSUE_UNROLL):
            t = i * ISSUE_UNROLL + u
            src = _row_tile(xp_ref, t)
            for k in range(TOP_K):
                pltpu.make_async_copy(src, _row_tile(xs_ref, dest_ref[t * TOP_K + k]),
                                      sem).start(priority=k % 2)
        return c

    lax.fori_loop(0, tm // ISSUE_UNROLL, issue, 0)

    @pl.when(pl.program_id(0) == 0)
    def _():
        zeros[...] = jnp.zeros_like(zeros)

        def fill(act):
            def body(e, c):
                pos = fill_start_ref[e]
                n = fill_len_ref[e]
                piece = 1
                while piece < bm:
                    take = (n & piece) != 0

                    @pl.when(take)
                    def _(pos=pos, piece=piece):
                        at = pl.multiple_of(pos * ROW_TILE, ROW_TILE)
                        act(pltpu.make_async_copy(zeros.at[pl.ds(0, piece * ROW_TILE)],
                                                  xs_ref.at[pl.ds(at, piece * ROW_TILE)], zsem))

                    pos = pos + jnp.where(take, piece, 0)
                    piece *= 2
                return c

            lax.fori_loop(0, N_EXPERTS, body, 0)

        def tail(act):
            def body(b, c):
                at = pl.multiple_of(b * (bm * ROW_TILE), bm * ROW_TILE)
                act(pltpu.make_async_copy(zeros, xs_ref.at[pl.ds(at, bm * ROW_TILE)], zsem))
                return c

            lax.fori_loop(tail_ref[0], n_blocks, body, 0)

        fill(lambda cp: cp.start())
        tail(lambda cp: cp.start())
        fill(lambda cp: cp.wait())
        tail(lambda cp: cp.wait())

    sh_ref[...] = _swiglu(xb_ref[...], s1_ref, s3_ref, s2_ref).astype(sh_ref.dtype)

    for _ in range(TOP_K):
        pltpu.make_async_copy(xp_ref, xs_ref.at[pl.ds(0, tm * ROW_TILE)], sem).wait()


def _dispatch(fill_start, fill_len, n_used, dest_flat, xp, xb, p, n_blocks):
    t = xb.shape[0]
    tm = min(DISPATCH_TM, t)
    grid_spec = pltpu.PrefetchScalarGridSpec(
        num_scalar_prefetch=3,
        grid=(t // tm,),
        in_specs=[pl.BlockSpec((tm * TOP_K,), lambda i, *_: (i,), memory_space=pltpu.SMEM),
                  pl.BlockSpec((tm * ROW_TILE, LANES), lambda i, *_: (i, 0)),
                  pl.BlockSpec((tm, D_MODEL), lambda i, *_: (i, 0)),
                  _resident((D_MODEL, D_EXPERT), lambda i, *_: (0, 0)),
                  _resident((D_MODEL, D_EXPERT), lambda i, *_: (0, 0)),
                  _resident((D_EXPERT, D_MODEL), lambda i, *_: (0, 0))],
        out_specs=[pl.BlockSpec(memory_space=pl.ANY),
                   pl.BlockSpec((tm, D_MODEL), lambda i, *_: (i, 0))],
        scratch_shapes=[pltpu.VMEM((EXPERT_BM * ROW_TILE, LANES), U32), pltpu.SemaphoreType.DMA,
                        pltpu.SemaphoreType.DMA],
    )
    return pl.pallas_call(
        functools.partial(_dispatch_kernel, n_blocks=n_blocks),
        grid_spec=grid_spec,
        out_shape=[jax.ShapeDtypeStruct((n_blocks * EXPERT_BM * ROW_TILE, LANES), U32),
                   jax.ShapeDtypeStruct((t, D_MODEL), BF16)],
        compiler_params=_cparams("arbitrary"),
        name="moe_dispatch",
    )(fill_start, fill_len, n_used, dest_flat, xp, xb, p["sh_w1"], p["sh_w3"], p["sh_w2"])


def _expert_kernel(bexp_ref, nused_ref, xs_ref, w1_ref, w3_ref, w2_ref, ys_ref, w1b, w3b, w2b, xb):
    b = pl.program_id(0)
    used = b < nused_ref[0]
    changed = jnp.logical_or(b == 0, bexp_ref[b] != bexp_ref[jnp.maximum(b - 1, 0)])

    @pl.when(jnp.logical_and(used, changed))
    def _():
        w1b[...] = w1_ref[0, 0].astype(BF16)
        w3b[...] = w3_ref[0, 0].astype(BF16)
        w2b[...] = w2_ref[0, 0].astype(BF16)

    @pl.when(used)
    def _():
        bm = xb.shape[0]
        for j in range(ROW_TILE):
            lo, hi = _load_row_tile_chunk(xs_ref, j, bm)
            xb[:, j * LANES:(j + 1) * LANES] = lo.astype(BF16)
            xb[:, PACKED + j * LANES:PACKED + (j + 1) * LANES] = hi.astype(BF16)
        _store_row_tiles(ys_ref, _swiglu(xb[...], w1b, w3b, w2b))

    @pl.when(jnp.logical_not(used))
    def _():
        ys_ref[...] = jnp.zeros_like(ys_ref)


def _experts(block_exp, n_used, xs, w1, w3, w2, layer):
    n_blocks = xs.shape[0] // (EXPERT_BM * ROW_TILE)

    def last_used(b, nu):
        return jnp.minimum(b, nu[0] - 1)

    def w_spec(shape):
        return pl.BlockSpec((1, 1) + shape, lambda b, be, nu: (layer, be[last_used(b, nu)], 0, 0))

    grid_spec = pltpu.PrefetchScalarGridSpec(
        num_scalar_prefetch=2,
        grid=(n_blocks,),
        in_specs=[
            pl.BlockSpec((EXPERT_BM * ROW_TILE, LANES), lambda b, be, nu: (last_used(b, nu), 0)),
            w_spec((D_MODEL, D_EXPERT)), w_spec((D_MODEL, D_EXPERT)), w_spec((D_EXPERT, D_MODEL)),
        ],
        out_specs=pl.BlockSpec((EXPERT_BM * ROW_TILE, LANES), lambda b, be, nu: (b, 0)),
        scratch_shapes=[pltpu.VMEM((D_MODEL, D_EXPERT), BF16), pltpu.VMEM((D_MODEL, D_EXPERT), BF16),
                        pltpu.VMEM((D_EXPERT, D_MODEL), BF16), pltpu.VMEM((EXPERT_BM, D_MODEL), BF16)],
    )
    return pl.pallas_call(
        _expert_kernel,
        grid_spec=grid_spec,
        out_shape=jax.ShapeDtypeStruct(xs.shape, U32),
        compiler_params=_cparams("arbitrary"),
        name="moe_experts",
    )(block_exp, n_used, xs, w1, w3, w2)


def _combine_kernel(dcur_ref, dnext_ref, gate_ref, x_ref, sh_ref, ys_ref, g_ref, b_ref,
                    xo_ref, xbo_ref, buf, hbuf, sem):
    i = pl.program_id(0)
    tm = x_ref.shape[0]
    slot = i % 2

    def issue(dref, s):
        def body(it, c):
            for u in range(ISSUE_UNROLL):
                t = it * ISSUE_UNROLL + u
                for k in range(TOP_K):
                    pltpu.make_async_copy(_row_tile(ys_ref, dref[t * TOP_K + k]),
                                          _row_tile(buf.at[s, k], t), sem.at[s]).start(priority=k % 2)
            return c

        lax.fori_loop(0, tm // ISSUE_UNROLL, body, 0)

    @pl.when(i == 0)
    def _():
        issue(dcur_ref, slot)

    @pl.when(i + 1 < pl.num_programs(0))
    def _():
        issue(dnext_ref, 1 - slot)

    for k in range(TOP_K):
        pltpu.make_async_copy(ys_ref.at[pl.ds(0, tm * ROW_TILE)], buf.at[slot, k], sem.at[slot]).wait()

    gates = [jnp.broadcast_to(gate_ref[:, k:k + 1], (tm, LANES)) for k in range(TOP_K)]
    for j in range(ROW_TILE):
        lo_cols = slice(j * LANES, (j + 1) * LANES)
        hi_cols = slice(PACKED + j * LANES, PACKED + (j + 1) * LANES)
        acc_lo = sh_ref[:, lo_cols].astype(F32)
        acc_hi = sh_ref[:, hi_cols].astype(F32)
        for k in range(TOP_K):
            lo, hi = _load_row_tile_chunk(buf.at[slot, k], j, tm)
            acc_lo = acc_lo + gates[k] * lo
            acc_hi = acc_hi + gates[k] * hi
        hbuf[:, lo_cols] = acc_lo
        hbuf[:, hi_cols] = acc_hi
    y = _layer_norm(DEEPNORM_ALPHA * x_ref[...] + hbuf[...], g_ref[...], b_ref[...])
    xo_ref[...] = y
    xbo_ref[...] = y.astype(BF16)


def _combine(dest_flat, gate_t, x, shared, ys, p):
    t = x.shape[0]
    tm = min(COMBINE_TM, t)
    steps = t // tm
    row = lambda i: (i, 0)
    return pl.pallas_call(
        _combine_kernel,
        grid=(steps,),
        in_specs=[pl.BlockSpec((tm * TOP_K,), lambda i: (i,), memory_space=pltpu.SMEM),
                  pl.BlockSpec((tm * TOP_K,), lambda i: (jnp.minimum(i + 1, steps - 1),),
                               memory_space=pltpu.SMEM),
                  pl.BlockSpec((tm, TOP_K), row),
                  pl.BlockSpec((tm, D_MODEL), row),
                  pl.BlockSpec((tm, D_MODEL), row),
                  pl.BlockSpec(memory_space=pl.ANY),
                  _resident((1, D_MODEL), lambda i: (0, 0)),
                  _resident((1, D_MODEL), lambda i: (0, 0))],
        out_specs=[pl.BlockSpec((tm, D_MODEL), row), pl.BlockSpec((tm, D_MODEL), row)],
        out_shape=[jax.ShapeDtypeStruct((t, D_MODEL), F32),
                   jax.ShapeDtypeStruct((t, D_MODEL), BF16)],
        scratch_shapes=[pltpu.VMEM((2, TOP_K, tm * ROW_TILE, LANES), U32),
                        pltpu.VMEM((tm, D_MODEL), F32), pltpu.SemaphoreType.DMA((2,))],
        compiler_params=_cparams("arbitrary"),
        name="moe_combine_ln",
    )(dest_flat, dest_flat, gate_t, x, shared, ys, p["ln2_g"], p["ln2_b"])


def _n_expert_blocks(t):
    return -(-(t * TOP_K + N_EXPERTS * (EXPERT_BM - 1)) // EXPERT_BM)


def _moe(x, xb, xp, p, exp_w1, exp_w3, exp_w2, layer):
    t = x.shape[0]
    n_blocks = _n_expert_blocks(t)
    idx, gate, rank, cnt = _router(x, p["router_w_t"], p["router_bias"])
    counts = cnt[:, 0]
    padded = (counts + EXPERT_BM - 1) // EXPERT_BM * EXPERT_BM
    pad_end = jnp.cumsum(padded)
    pad_start = pad_end - padded
    n_used = (pad_end[-1:] // EXPERT_BM).astype(I32)
    block_first = jnp.arange(n_blocks, dtype=I32) * EXPERT_BM
    block_exp = jnp.minimum(jnp.sum(pad_end[None, :] <= block_first[:, None], axis=1),
                            N_EXPERTS - 1).astype(I32)
    dest = _dest_rows(pad_start.astype(I32), idx, rank)
    dest_flat = dest.T.reshape(-1)
    xs, shared = _dispatch((pad_start + counts).astype(I32), (padded - counts).astype(I32), n_used,
                           dest_flat, xp, xb, p, n_blocks)
    ys = _experts(block_exp, n_used, xs, exp_w1, exp_w3, exp_w2, layer)
    return _combine(dest_flat, gate.T, x, shared, ys, p)


def _layer_params(l, b_gate, sinks, pool_w, pool_scale, conv_w, conv_b, lru_wa, lru_ba, lru_wx,
                  lru_bx, lru_lambda, w_branch, w_out, ln1_g, ln1_b, router_w, router_bias,
                  sh_w1, sh_w3, sh_w2, ln2_g, ln2_b):
    row = lambda a: a[l].reshape(1, -1).astype(F32)
    return dict(
        b_gate=row(b_gate), sinks=sinks[l].astype(F32),
        pool_w=pool_w[l].astype(BF16), pool_scale=row(pool_scale),
        conv_w=conv_w[l].astype(F32), conv_b=row(conv_b),
        lru_wa=_block_diag(lru_wa[l]).astype(BF16), lru_ba=row(lru_ba),
        lru_wx=_block_diag(lru_wx[l]).astype(BF16), lru_bx=row(lru_bx),
        lru_lambda=row(lru_lambda),
        w_branch=w_branch[l].astype(BF16), w_out=w_out[l].astype(BF16),
        ln1_g=row(ln1_g), ln1_b=row(ln1_b),
        router_w_t=router_w[l].T.astype(F32), router_bias=router_bias[l].reshape(-1, 1).astype(F32),
        sh_w1=sh_w1[l].astype(BF16), sh_w3=sh_w3[l].astype(BF16), sh_w2=sh_w2[l].astype(BF16),
        ln2_g=row(ln2_g), ln2_b=row(ln2_b),
    )


def kernel(x, positions, w_in, b_gate, sinks, pool_w, pool_scale, conv_w, conv_b, lru_wa, lru_ba, lru_wx, lru_bx, lru_lambda, w_branch, w_out, ln1_g, ln1_b, router_w, router_bias, exp_w1, exp_w3, exp_w2, sh_w1, sh_w3, sh_w2, ln2_g, ln2_b):
    batch, seq_len, d = x.shape
    assert d == D_MODEL and seq_len % max(ATTN_TQ, MIX_TB) == 0
    t = batch * seq_len
    cos, sin = _rope_tables(positions)
    xf = x.reshape(t, d).astype(F32)
    xb = xf.astype(BF16)
    for l in range(DEPTH):
        p = _layer_params(l, b_gate, sinks, pool_w, pool_scale, conv_w, conv_b, lru_wa, lru_ba,
                          lru_wx, lru_bx, lru_lambda, w_branch, w_out, ln1_g, ln1_b, router_w,
                          router_bias, sh_w1, sh_w3, sh_w2, ln2_g, ln2_b)
        proj = _in_proj(xb, w_in, l)
        ya = _attention(proj, cos, sin, p["sinks"], seq_len)
        yp, yl = _mixers(proj, p, batch, seq_len)
        xf, xb, xp = _merge(proj, ya, yp, yl, xf, p)
        xf, xb = _moe(xf, xb, xp, p, exp_w1, exp_w3, exp_w2, l)
    return xf.reshape(batch, seq_len, d).astype(x.dtype)
```

```python
import functools

import jax
import jax.numpy as jnp
from jax import lax
from jax.experimental import pallas as pl
from jax.experimental.pallas import tpu as pltpu

F32 = jnp.float32
BF16 = jnp.bfloat16
I32 = jnp.int32
U32 = jnp.uint32

D_MODEL = 2048
DEPTH = 2
CHUNK = 64
HEAD_DIM = 64
HALF_HEAD = HEAD_DIM // 2
N_HEADS = 16
N_KV_HEADS = 4
KV_GROUP = N_HEADS // N_KV_HEADS
ATTN_WIDTH = N_HEADS * HEAD_DIM
KV_WIDTH = N_KV_HEADS * HEAD_DIM
WINDOW_CHUNKS = 2
ROPE_THETA = 10000.0
POOL_WINDOWS = (2, 4, 8, 16)
POOL_WIDTH = 1024
POOL_GROUP = POOL_WIDTH // len(POOL_WINDOWS)
LRU_WIDTH = 1024
LRU_BLOCKS = 16
LRU_BLOCK = LRU_WIDTH // LRU_BLOCKS
CONV_WIDTH = 4
LRU_C = 8.0
N_BRANCH = 3
GATE_WIDTH = N_BRANCH * D_MODEL
IN_WIDTH = ATTN_WIDTH + 2 * KV_WIDTH + POOL_WIDTH + 2 * LRU_WIDTH + GATE_WIDTH
N_EXPERTS = 64
TOP_K = 8
D_EXPERT = 512
ROUTED_SCALE = 2.5
LN_EPS = 1e-5
DEEPNORM_ALPHA = (2 * DEPTH) ** 0.25

COL_GATE = 0
COL_Q = GATE_WIDTH
COL_POOL = COL_Q + ATTN_WIDTH
COL_LRU = COL_POOL + POOL_WIDTH
COL_GELU = COL_LRU + LRU_WIDTH
COL_K = COL_GELU + LRU_WIDTH
COL_V = COL_K + KV_WIDTH

LANES = 128
SUBLANES = 8
MXU_DIM = 256
VMEM_LIMIT_BYTES = 56 * 1024 * 1024

PROJ_TM = 512
PROJ_TN = 1536
ATTN_TQ = 256
ATTN_HALO = WINDOW_CHUNKS * CHUNK
MIX_TB = 256
POOL_HALO = 16
CONV_HALO = 8
MERGE_TM = 256
ROUTER_TM = 512
DEST_TM = 2048
DISPATCH_TM = 256
EXPERT_BM = 512
COMBINE_TM = 256
PACKED = D_MODEL // 2
ROW_TILE = PACKED // LANES
ISSUE_UNROLL = 2
PROJ_SRC = 512
NEG_BIG = -1e30

assert ROW_TILE == SUBLANES


def _cparams(*sem):
    return pltpu.CompilerParams(dimension_semantics=sem, vmem_limit_bytes=VMEM_LIMIT_BYTES)


def _resident(shape, index_map):
    return pl.BlockSpec(shape, index_map, pipeline_mode=pl.Buffered(1))


def _pack_rows(lo, hi):
    lo_u = lax.bitcast_convert_type(lo.astype(BF16).astype(F32), U32)
    hi_u = lax.bitcast_convert_type(hi.astype(BF16).astype(F32), U32)
    return (lo_u >> 16) | (hi_u & jnp.uint32(0xFFFF0000))


def _unpack_rows(w):
    lo = lax.bitcast_convert_type(w << 16, F32)
    hi = lax.bitcast_convert_type(w & jnp.uint32(0xFFFF0000), F32)
    return lo, hi


def _chunk_cols(j):
    return (slice(2 * j * LANES, (2 * j + 1) * LANES), slice((2 * j + 1) * LANES, (2 * j + 2) * LANES))


def _store_row_tiles(ref, y):
    m = y.shape[0]
    for j in range(ROW_TILE):
        lo_cols, hi_cols = _chunk_cols(j)
        ref[pl.ds(j, m, stride=ROW_TILE), :] = _pack_rows(y[:, lo_cols], y[:, hi_cols])


def _load_row_tile_chunk(ref, j, m):
    return _unpack_rows(ref[pl.ds(j, m, stride=ROW_TILE), :])


def _row_tile(ref, r):
    return ref.at[pl.ds(pl.multiple_of(r * ROW_TILE, ROW_TILE), ROW_TILE)]


def _layer_norm(y, g, b):
    mu = jnp.mean(y, axis=-1, keepdims=True)
    d = y - mu
    var = jnp.mean(d * d, axis=-1, keepdims=True)
    return d * lax.rsqrt(var + LN_EPS) * g + b


def _rope_kernel(pos_ref, inv_ref, cos_ref, sin_ref):
    ang = pos_ref[...].astype(F32) * inv_ref[...]
    lane = lax.broadcasted_iota(I32, ang.shape, 1)
    first_half = (lane % HEAD_DIM) < HALF_HEAD
    cos_ref[...] = jnp.cos(ang)
    s = jnp.sin(ang)
    sin_ref[...] = jnp.where(first_half, -s, s)


def _rope_tables(positions):
    t = positions.size
    tm = min(1024, t)
    inv_freq = 1.0 / (ROPE_THETA ** (jnp.arange(0, HEAD_DIM, 2, dtype=F32) / HEAD_DIM))
    inv = jnp.tile(inv_freq, LANES // HALF_HEAD).reshape(1, LANES)
    return pl.pallas_call(
        _rope_kernel,
        grid=(t // tm,),
        in_specs=[pl.BlockSpec((tm, 1), lambda i: (i, 0)),
                  pl.BlockSpec((1, LANES), lambda i: (0, 0))],
        out_specs=[pl.BlockSpec((tm, LANES), lambda i: (i, 0))] * 2,
        out_shape=[jax.ShapeDtypeStruct((t, LANES), F32)] * 2,
        compiler_params=_cparams("parallel"),
        name="rope_tables",
    )(positions.reshape(t, 1), inv)


def _proj_source_blocks():
    ref_order = (("q", ATTN_WIDTH), ("k", KV_WIDTH), ("v", KV_WIDTH), ("pool", POOL_WIDTH),
                 ("lru", LRU_WIDTH), ("gelu", LRU_WIDTH), ("gate", GATE_WIDTH))
    start, col = {}, 0
    for name, width in ref_order:
        start[name] = col
        col += width
    assert start["v"] == start["k"] + KV_WIDTH and 2 * KV_WIDTH == PROJ_SRC
    out_order = (("gate", GATE_WIDTH), ("q", ATTN_WIDTH), ("pool", POOL_WIDTH), ("lru", LRU_WIDTH),
                 ("gelu", LRU_WIDTH), ("k", 2 * KV_WIDTH))
    blocks = []
    for name, width in out_order:
        assert start[name] % PROJ_SRC == 0 and width % PROJ_SRC == 0
        blocks += [start[name] // PROJ_SRC + c for c in range(width // PROJ_SRC)]
    return blocks


def _in_proj_kernel(src_ref, x_ref, *refs):
    w_refs, o_ref, wb = refs[:-2], refs[-2], refs[-1]

    @pl.when(pl.program_id(1) == 0)
    def _():
        for c, w_ref in enumerate(w_refs):
            wb[:, c * PROJ_SRC:(c + 1) * PROJ_SRC] = w_ref[0].astype(BF16)

    o_ref[...] = jnp.dot(x_ref[...], wb[...], preferred_element_type=F32).astype(o_ref.dtype)


def _in_proj(xb, w_in, layer):
    t, d = xb.shape
    n = w_in.shape[2]
    tm = min(PROJ_TM, t)
    per = PROJ_TN // PROJ_SRC
    src = jnp.asarray(_proj_source_blocks(), I32)

    def w_spec(c):
        return pl.BlockSpec((1, d, PROJ_SRC), lambda j, i, s: (layer, 0, s[per * j + c]))

    grid_spec = pltpu.PrefetchScalarGridSpec(
        num_scalar_prefetch=1,
        grid=(n // PROJ_TN, t // tm),
        in_specs=[pl.BlockSpec((tm, d), lambda j, i, s: (i, 0))] + [w_spec(c) for c in range(per)],
        out_specs=pl.BlockSpec((tm, PROJ_TN), lambda j, i, s: (i, j)),
        scratch_shapes=[pltpu.VMEM((d, PROJ_TN), BF16)],
    )
    return pl.pallas_call(
        _in_proj_kernel,
        grid_spec=grid_spec,
        out_shape=jax.ShapeDtypeStruct((t, n), BF16),
        compiler_params=_cparams("arbitrary", "arbitrary"),
        name="in_proj",
    )(src, xb, *([w_in] * per))


def _rope(x, cos, sin):
    w = x.shape[1]
    reps = w // LANES
    c = jnp.concatenate([cos] * reps, axis=1)
    s = jnp.concatenate([sin] * reps, axis=1)
    lane = lax.broadcasted_iota(I32, x.shape, 1)
    first_half = (lane % HEAD_DIM) < HALF_HEAD
    rot = jnp.where(first_half, pltpu.roll(x, w - HALF_HEAD, 1), pltpu.roll(x, HALF_HEAD, 1))
    return x * c + rot * s


def _attn_kernel(q_ref, k_ref, v_ref, kh_ref, vh_ref, cq_ref, sq_ref, ch_ref, sh_ref, sink_ref,
                 o_ref, s_buf, p_buf, r_buf, *, steps_per_seq):
    first = (pl.program_id(0) % steps_per_seq) == 0
    q = _rope(q_ref[...].astype(F32), cq_ref[...], sq_ref[...]) * (HEAD_DIM ** -0.5)
    q = q.astype(BF16)
    k_own = _rope(k_ref[...].astype(F32), cq_ref[...], sq_ref[...])
    k_halo = _rope(kh_ref[...].astype(F32), ch_ref[...], sh_ref[...])
    k_all = jnp.concatenate([k_halo, k_own], axis=0).astype(BF16)
    v_all = jnp.concatenate([vh_ref[...], v_ref[...]], axis=0)
    span = (WINDOW_CHUNKS + 1) * CHUNK
    rows = KV_GROUP * CHUNK
    chunk_rows = N_KV_HEADS * rows
    n_chunks = q.shape[0] // CHUNK
    win_chunk = lax.broadcasted_iota(I32, (rows, span), 1) // CHUNK
    oldest = jnp.where(first, 0, -WINDOW_CHUNKS)

    def tile_rows(c, g):
        return slice(c * chunk_rows + g * rows, c * chunk_rows + (g + 1) * rows)

    def kv_cols(g):
        return slice(g * HEAD_DIM, (g + 1) * HEAD_DIM)

    for c in range(n_chunks):
        q_c = q[c * CHUNK:(c + 1) * CHUNK, :]
        k_win = k_all[c * CHUNK:c * CHUNK + span, :]
        for g in range(N_KV_HEADS):
            qs = jnp.concatenate(
                [q_c[:, (g * KV_GROUP + i) * HEAD_DIM:(g * KV_GROUP + i + 1) * HEAD_DIM]
                 for i in range(KV_GROUP)], axis=0)
            s = lax.dot_general(qs, k_win[:, kv_cols(g)], (((1,), (1,)), ((), ())),
                                preferred_element_type=F32)
            if c < WINDOW_CHUNKS:
                s = jnp.where(win_chunk + (c - WINDOW_CHUNKS) >= oldest, s, NEG_BIG)
            s_buf[tile_rows(c, g), :] = s
    sink = sink_ref[...]
    for c in range(n_chunks):
        blk = slice(c * chunk_rows, (c + 1) * chunk_rows)
        s = s_buf[blk, :]
        m = jnp.maximum(jnp.max(s, axis=1, keepdims=True), sink)
        p = jnp.exp(s - m)
        p_buf[blk, :] = p.astype(BF16)
        r_buf[blk, :] = 1.0 / (jnp.sum(p, axis=1, keepdims=True) + jnp.exp(sink - m))
    for c in range(n_chunks):
        v_win = v_all[c * CHUNK:c * CHUNK + span, :]
        heads = []
        for g in range(N_KV_HEADS):
            o = jnp.dot(p_buf[tile_rows(c, g), :], v_win[:, kv_cols(g)], preferred_element_type=F32)
            o = o * r_buf[tile_rows(c, g), :]
            heads += [o[i * CHUNK:(i + 1) * CHUNK, :] for i in range(KV_GROUP)]
        o_ref[c * CHUNK:(c + 1) * CHUNK, :] = jnp.concatenate(heads, axis=1).astype(o_ref.dtype)


def _attention(proj, cos, sin, sinks, seq_len):
    t = proj.shape[0]
    tq = ATTN_TQ
    steps_per_seq = seq_len // tq
    halo_per_tq = tq // ATTN_HALO
    q_blk = COL_Q // ATTN_WIDTH
    k_blk = COL_K // KV_WIDTH
    v_blk = COL_V // KV_WIDTH

    def halo_row(i):
        return jnp.maximum(i * halo_per_tq - 1, 0)

    span = (WINDOW_CHUNKS + 1) * CHUNK
    tile_rows = (tq // CHUNK) * N_HEADS * CHUNK
    sink_rows = jnp.repeat(sinks, CHUNK).reshape(N_HEADS * CHUNK, 1)
    return pl.pallas_call(
        functools.partial(_attn_kernel, steps_per_seq=steps_per_seq),
        grid=(t // tq,),
        in_specs=[
            pl.BlockSpec((tq, ATTN_WIDTH), lambda i: (i, q_blk)),
            pl.BlockSpec((tq, KV_WIDTH), lambda i: (i, k_blk)),
            pl.BlockSpec((tq, KV_WIDTH), lambda i: (i, v_blk)),
            pl.BlockSpec((ATTN_HALO, KV_WIDTH), lambda i: (halo_row(i), k_blk)),
            pl.BlockSpec((ATTN_HALO, KV_WIDTH), lambda i: (halo_row(i), v_blk)),
            pl.BlockSpec((tq, LANES), lambda i: (i, 0)),
            pl.BlockSpec((tq, LANES), lambda i: (i, 0)),
            pl.BlockSpec((ATTN_HALO, LANES), lambda i: (halo_row(i), 0)),
            pl.BlockSpec((ATTN_HALO, LANES), lambda i: (halo_row(i), 0)),
            pl.BlockSpec((N_HEADS * CHUNK, 1), lambda i: (0, 0)),
        ],
        out_specs=pl.BlockSpec((tq, ATTN_WIDTH), lambda i: (i, 0)),
        out_shape=jax.ShapeDtypeStruct((t, ATTN_WIDTH), BF16),
        scratch_shapes=[pltpu.VMEM((tile_rows, span), F32), pltpu.VMEM((tile_rows, span), BF16),
                        pltpu.VMEM((tile_rows, 1), F32)],
        compiler_params=_cparams("parallel"),
        name="swa_attention",
    )(proj, proj, proj, proj, proj, cos, sin, cos, sin, sink_rows)


def _mix_kernel(up_ref, ul_ref, ug_ref, pw_ref, ps_ref, cw_ref, cb_ref, wa_ref, ba_ref, wx_ref,
                bx_ref, lam_ref, yp_ref, yl_ref, halo_p, halo_l, h_carry):
    s = pl.program_id(1)
    tb = up_ref.shape[0]

    @pl.when(s == 0)
    def _():
        halo_p[...] = jnp.zeros_like(halo_p)
        halo_l[...] = jnp.zeros_like(halo_l)
        h_carry[...] = jnp.zeros_like(h_carry)

    u = up_ref[...].astype(F32)
    ext = jnp.concatenate([halo_p[...], u], axis=0)
    s2 = ext + pltpu.roll(ext, 1, 0)
    s4 = s2 + pltpu.roll(s2, 2, 0)
    s8 = s4 + pltpu.roll(s4, 4, 0)
    s16 = s8 + pltpu.roll(s8, 8, 0)
    t_idx = (s * tb + lax.broadcasted_iota(I32, (tb, 1), 0)).astype(F32)
    for g, (win, ws) in enumerate(zip(POOL_WINDOWS, (s2, s4, s8, s16))):
        cols = slice(g * POOL_GROUP, (g + 1) * POOL_GROUP)
        count = jnp.minimum(t_idx + 1.0, float(win))
        pooled = ws[POOL_HALO:, cols] / count - u[:, cols]
        y = jnp.dot(pooled.astype(BF16), pw_ref[g], preferred_element_type=F32)
        yp_ref[:, cols] = (y * ps_ref[:, cols]).astype(yp_ref.dtype)
    halo_p[...] = u[tb - POOL_HALO:, :]

    ul = ul_ref[...].astype(F32)
    extl = jnp.concatenate([halo_l[...], ul], axis=0)
    xc = cb_ref[...] + cw_ref[CONV_WIDTH - 1:CONV_WIDTH, :] * ul
    for j in range(1, CONV_WIDTH):
        w_j = cw_ref[CONV_WIDTH - 1 - j:CONV_WIDTH - j, :]
        xc = xc + w_j * pltpu.roll(extl, j, 0)[CONV_HALO:, :]
    halo_l[...] = ul[tb - CONV_HALO:, :]
    xcb = xc.astype(BF16)
    r_parts, i_parts = [], []
    for j in range(LRU_WIDTH // MXU_DIM):
        cols = slice(j * MXU_DIM, (j + 1) * MXU_DIM)
        r_parts.append(jnp.dot(xcb[:, cols], wa_ref[j], preferred_element_type=F32))
        i_parts.append(jnp.dot(xcb[:, cols], wx_ref[j], preferred_element_type=F32))
    r = jax.nn.sigmoid(jnp.concatenate(r_parts, axis=1) + ba_ref[...])
    ig = jax.nn.sigmoid(jnp.concatenate(i_parts, axis=1) + bx_ref[...])
    z = -lam_ref[...]
    softplus = jnp.maximum(z, 0.0) + jnp.log1p(jnp.exp(-jnp.abs(z)))
    log_a = (-LRU_C) * r * softplus
    a = jnp.exp(log_a)
    b = jnp.sqrt(1.0 - jnp.exp(2.0 * log_a)) * (ig * xc)
    row_in_group = lax.broadcasted_iota(I32, a.shape, 0) % SUBLANES
    d = 1
    while d < SUBLANES:
        keep = row_in_group >= d
        a_prev = jnp.where(keep, pltpu.roll(a, d, 0), 1.0)
        b_prev = jnp.where(keep, pltpu.roll(b, d, 0), 0.0)
        b = a * b_prev + b
        a = a * a_prev
        d *= 2
    gelu = jax.nn.gelu(ug_ref[...].astype(F32))
    h_prev = h_carry[...]
    pair = 2 * SUBLANES
    for g in range(tb // pair):
        hs = []
        for half in range(2):
            rows = slice(g * pair + half * SUBLANES, g * pair + (half + 1) * SUBLANES)
            h = a[rows, :] * h_prev + b[rows, :]
            h_prev = h[SUBLANES - 1:SUBLANES, :]
            hs.append(h)
        rows = slice(g * pair, (g + 1) * pair)
        yl_ref[rows, :] = (jnp.concatenate(hs, axis=0) * gelu[rows, :]).astype(yl_ref.dtype)
    h_carry[...] = h_prev


def _block_diag(w):
    per = MXU_DIM // LRU_BLOCK
    w = w.reshape(LRU_BLOCKS // per, per, LRU_BLOCK, LRU_BLOCK)
    eye = jnp.eye(per, dtype=w.dtype)
    out = jnp.einsum("gpij,pq->gpiqj", w, eye)
    return out.reshape(LRU_BLOCKS // per, MXU_DIM, MXU_DIM)


def _mixers(proj, p, batch, seq_len):
    t = proj.shape[0]
    tb = min(MIX_TB, seq_len)
    steps = seq_len // tb

    def rows(bi, si):
        return bi * steps + si

    def col_spec(col):
        blk = col // POOL_WIDTH
        return pl.BlockSpec((tb, POOL_WIDTH), lambda bi, si: (rows(bi, si), blk))

    def const(shape):
        nd = len(shape)
        return pl.BlockSpec(shape, lambda bi, si: (0,) * nd)

    out_spec = pl.BlockSpec((tb, POOL_WIDTH), lambda bi, si: (rows(bi, si), 0))
    return pl.pallas_call(
        _mix_kernel,
        grid=(batch, steps),
        in_specs=[col_spec(COL_POOL), col_spec(COL_LRU), col_spec(COL_GELU),
                  const((len(POOL_WINDOWS), POOL_GROUP, POOL_GROUP)), const((1, POOL_WIDTH)),
                  const((CONV_WIDTH, LRU_WIDTH)), const((1, LRU_WIDTH)),
                  const((LRU_WIDTH // MXU_DIM, MXU_DIM, MXU_DIM)), const((1, LRU_WIDTH)),
                  const((LRU_WIDTH // MXU_DIM, MXU_DIM, MXU_DIM)), const((1, LRU_WIDTH)),
                  const((1, LRU_WIDTH))],
        out_specs=[out_spec, out_spec],
        out_shape=[jax.ShapeDtypeStruct((t, POOL_WIDTH), BF16),
                   jax.ShapeDtypeStruct((t, LRU_WIDTH), BF16)],
        scratch_shapes=[pltpu.VMEM((POOL_HALO, POOL_WIDTH), F32),
                        pltpu.VMEM((CONV_HALO, LRU_WIDTH), F32),
                        pltpu.VMEM((1, LRU_WIDTH), F32)],
        compiler_params=_cparams("arbitrary", "arbitrary"),
        name="pool_rglru",
    )(proj, proj, proj, p["pool_w"], p["pool_scale"], p["conv_w"], p["conv_b"],
      p["lru_wa"], p["lru_ba"], p["lru_wx"], p["lru_bx"], p["lru_lambda"])


def _merge_kernel(gl_ref, ya_ref, yp_ref, yl_ref, x_ref, bg_ref, wb_ref, wo_ref, g_ref, b_ref,
                  xo_ref, xb_ref, xp_ref):
    merged = None
    for j, y_ref in enumerate((ya_ref, yp_ref, yl_ref)):
        cols = slice(j * D_MODEL, (j + 1) * D_MODEL)
        z = jnp.dot(y_ref[...], wb_ref[j], preferred_element_type=F32)
        gate = jax.nn.sigmoid(gl_ref[:, cols].astype(F32) + bg_ref[:, cols])
        merged = gate * z if merged is None else merged + gate * z
    h = jnp.dot(merged.astype(BF16), wo_ref[...], preferred_element_type=F32)
    y = _layer_norm(DEEPNORM_ALPHA * x_ref[...] + h, g_ref[...], b_ref[...])
    xo_ref[...] = y
    xb_ref[...] = y.astype(BF16)
    _store_row_tiles(xp_ref, y)


def _merge(proj, ya, yp, yl, x, p):
    t = x.shape[0]
    tm = min(MERGE_TM, t)
    row = lambda i: (i, 0)
    return pl.pallas_call(
        _merge_kernel,
        grid=(t // tm,),
        in_specs=[pl.BlockSpec((tm, GATE_WIDTH), row),
                  pl.BlockSpec((tm, ATTN_WIDTH), row),
                  pl.BlockSpec((tm, POOL_WIDTH), row),
                  pl.BlockSpec((tm, LRU_WIDTH), row),
                  pl.BlockSpec((tm, D_MODEL), row),
                  _resident((1, GATE_WIDTH), lambda i: (0, 0)),
                  _resident((N_BRANCH, ATTN_WIDTH, D_MODEL), lambda i: (0, 0, 0)),
                  _resident((D_MODEL, D_MODEL), lambda i: (0, 0)),
                  _resident((1, D_MODEL), lambda i: (0, 0)),
                  _resident((1, D_MODEL), lambda i: (0, 0))],
        out_specs=[pl.BlockSpec((tm, D_MODEL), row), pl.BlockSpec((tm, D_MODEL), row),
                   pl.BlockSpec((tm * ROW_TILE, LANES), row)],
        out_shape=[jax.ShapeDtypeStruct((t, D_MODEL), F32),
                   jax.ShapeDtypeStruct((t, D_MODEL), BF16),
                   jax.ShapeDtypeStruct((t * ROW_TILE, LANES), U32)],
        compiler_params=_cparams("parallel"),
        name="merge_ln",
    )(proj, ya, yp, yl, x, p["b_gate"], p["w_branch"], p["w_out"], p["ln1_g"], p["ln1_b"])


def _router_kernel(x_ref, rw_ref, rb_ref, idx_ref, gate_ref, rank_ref, cnt_ref, carry):
    @pl.when(pl.program_id(0) == 0)
    def _():
        carry[...] = jnp.zeros_like(carry)

    tm = x_ref.shape[0]
    logits = lax.dot_general(rw_ref[...], x_ref[...], (((1,), (1,)), ((), ())),
                             preferred_element_type=F32)
    scores = jax.nn.sigmoid(logits)
    sel = scores + rb_ref[...]
    expert = lax.broadcasted_iota(I32, scores.shape, 0)
    idxs, tops = [], []
    chosen = jnp.zeros(scores.shape, F32)
    for _ in range(TOP_K):
        best = jnp.max(sel, axis=0, keepdims=True)
        ik = jnp.min(jnp.where(sel == best, expert, N_EXPERTS), axis=0, keepdims=True)
        hit = expert == ik
        tops.append(jnp.sum(jnp.where(hit, scores, 0.0), axis=0, keepdims=True))
        idxs.append(ik)
        chosen = jnp.where(hit, 1.0, chosen)
        sel = jnp.where(hit, -jnp.inf, sel)
    top = jnp.concatenate(tops, axis=0)
    idx = jnp.concatenate(idxs, axis=0)
    gate_ref[...] = top / jnp.sum(top, axis=0, keepdims=True) * ROUTED_SCALE
    idx_ref[...] = idx
    earlier = (lax.broadcasted_iota(I32, (tm, tm), 0) < lax.broadcasted_iota(I32, (tm, tm), 1))
    before = jnp.dot(chosen.astype(BF16), jnp.where(earlier, 1.0, 0.0).astype(BF16),
                     preferred_element_type=F32)
    base = before + carry[:, 0:1]
    ranks = [jnp.sum(jnp.where(expert == idxs[k], base, 0.0), axis=0, keepdims=True)
             for k in range(TOP_K)]
    rank_ref[...] = jnp.concatenate(ranks, axis=0).astype(I32)
    carry[...] = carry[...] + jnp.sum(chosen, axis=1, keepdims=True)
    cnt_ref[...] = carry[...].astype(I32)


def _router(x, rw_t, rbias):
    t = x.shape[0]
    tm = min(ROUTER_TM, t)
    tok = lambda i: (0, i)
    return pl.pallas_call(
        _router_kernel,
        grid=(t // tm,),
        in_specs=[pl.BlockSpec((tm, D_MODEL), lambda i: (i, 0)),
                  pl.BlockSpec((N_EXPERTS, D_MODEL), lambda i: (0, 0)),
                  pl.BlockSpec((N_EXPERTS, 1), lambda i: (0, 0))],
        out_specs=[pl.BlockSpec((TOP_K, tm), tok), pl.BlockSpec((TOP_K, tm), tok),
                   pl.BlockSpec((TOP_K, tm), tok),
                   pl.BlockSpec((N_EXPERTS, LANES), lambda i: (0, 0))],
        out_shape=[jax.ShapeDtypeStruct((TOP_K, t), I32), jax.ShapeDtypeStruct((TOP_K, t), F32),
                   jax.ShapeDtypeStruct((TOP_K, t), I32),
                   jax.ShapeDtypeStruct((N_EXPERTS, LANES), I32)],
        scratch_shapes=[pltpu.VMEM((N_EXPERTS, LANES), F32)],
        compiler_params=_cparams("arbitrary"),
        name="router_topk",
    )(x, rw_t, rbias)


def _dest_kernel(start_ref, idx_ref, rank_ref, dest_ref):
    idx = idx_ref[...]
    dest = rank_ref[...]
    for e in range(N_EXPERTS):
        dest = dest + jnp.where(idx == e, start_ref[e], 0)
    dest_ref[...] = dest


def _dest_rows(pad_start, idx, rank):
    t = idx.shape[1]
    tm = min(DEST_TM, t)
    spec = pl.BlockSpec((TOP_K, tm), lambda i, s: (0, i))
    return pl.pallas_call(
        _dest_kernel,
        grid_spec=pltpu.PrefetchScalarGridSpec(
            num_scalar_prefetch=1, grid=(t // tm,), in_specs=[spec, spec], out_specs=spec),
        out_shape=jax.ShapeDtypeStruct((TOP_K, t), I32),
        compiler_params=_cparams("parallel"),
        name="dest_rows",
    )(pad_start, idx, rank)


def _swiglu(xb, w1_ref, w3_ref, w2_ref):
    a = jnp.dot(xb, w1_ref[...], preferred_element_type=F32)
    g = jnp.dot(xb, w3_ref[...], preferred_element_type=F32)
    hid = (a * jax.nn.sigmoid(a) * g).astype(BF16)
    return jnp.dot(hid, w2_ref[...], preferred_element_type=F32)


def _dispatch_kernel(fill_start_ref, fill_len_ref, tail_ref, dest_ref, xp_ref, xb_ref, s1_ref, s3_ref,
                     s2_ref, xs_ref, sh_ref, zeros, sem, zsem, *, n_blocks):
    tm = xb_ref.shape[0]
    bm = zeros.shape[0] // ROW_TILE

    def issue(i, c):
        for u in range(ISSUE_UNROLL):
            t = i * ISSUE_UNROLL + u
            src = _row_tile(xp_ref, t)
            for k in range(TOP_K):
                pltpu.make_async_copy(src, _row_tile(xs_ref, dest_ref[t * TOP_K + k]),
                                      sem).start(priority=k % 2)
        return c

    lax.fori_loop(0, tm // ISSUE_UNROLL, issue, 0)

    @pl.when(pl.program_id(0) == 0)
    def _():
        zeros[...] = jnp.zeros_like(zeros)

        def fill(act):
            def body(e, c):
                pos = fill_start_ref[e]
                n = fill_len_ref[e]
                piece = 1
                while piece < bm:
                    take = (n & piece) != 0

                    @pl.when(take)
                    def _(pos=pos, piece=piece):
                        at = pl.multiple_of(pos * ROW_TILE, ROW_TILE)
                        act(pltpu.make_async_copy(zeros.at[pl.ds(0, piece * ROW_TILE)],
                                                  xs_ref.at[pl.ds(at, piece * ROW_TILE)], zsem))

                    pos = pos + jnp.where(take, piece, 0)
                    piece *= 2
                return c

            lax.fori_loop(0, N_EXPERTS, body, 0)

        def tail(act):
            def body(b, c):
                at = pl.multiple_of(b * (bm * ROW_TILE), bm * ROW_TILE)
                act(pltpu.make_async_copy(zeros, xs_ref.at[pl.ds(at, bm * ROW_TILE)], zsem))
                return c

            lax.fori_loop(tail_ref[0], n_blocks, body, 0)

        fill(lambda cp: cp.start())
        tail(lambda cp: cp.start())
        fill(lambda cp: cp.wait())
        tail(lambda cp: cp.wait())

    sh_ref[...] = _swiglu(xb_ref[...], s1_ref, s3_ref, s2_ref).astype(sh_ref.dtype)

    for _ in range(TOP_K):
        pltpu.make_async_copy(xp_ref, xs_ref.at[pl.ds(0, tm * ROW_TILE)], sem).wait()


def _dispatch(fill_start, fill_len, n_used, dest_flat, xp, xb, p, n_blocks):
    t = xb.shape[0]
    tm = min(DISPATCH_TM, t)
    grid_spec = pltpu.PrefetchScalarGridSpec(
        num_scalar_prefetch=3,
        grid=(t // tm,),
        in_specs=[pl.BlockSpec((tm * TOP_K,), lambda i, *_: (i,), memory_space=pltpu.SMEM),
                  pl.BlockSpec((tm * ROW_TILE, LANES), lambda i, *_: (i, 0)),
                  pl.BlockSpec((tm, D_MODEL), lambda i, *_: (i, 0)),
                  _resident((D_MODEL, D_EXPERT), lambda i, *_: (0, 0)),
                  _resident((D_MODEL, D_EXPERT), lambda i, *_: (0, 0)),
                  _resident((D_EXPERT, D_MODEL), lambda i, *_: (0, 0))],
        out_specs=[pl.BlockSpec(memory_space=pl.ANY),
                   pl.BlockSpec((tm, D_MODEL), lambda i, *_: (i, 0))],
        scratch_shapes=[pltpu.VMEM((EXPERT_BM * ROW_TILE, LANES), U32), pltpu.SemaphoreType.DMA,
                        pltpu.SemaphoreType.DMA],
    )
    return pl.pallas_call(
        functools.partial(_dispatch_kernel, n_blocks=n_blocks),
        grid_spec=grid_spec,
        out_shape=[jax.ShapeDtypeStruct((n_blocks * EXPERT_BM * ROW_TILE, LANES), U32),
                   jax.ShapeDtypeStruct((t, D_MODEL), BF16)],
        compiler_params=_cparams("arbitrary"),
        name="moe_dispatch",
    )(fill_start, fill_len, n_used, dest_flat, xp, xb, p["sh_w1"], p["sh_w3"], p["sh_w2"])


def _expert_kernel(bexp_ref, nused_ref, xs_ref, w1_ref, w3_ref, w2_ref, ys_ref, w1b, w3b, w2b, xb):
    b = pl.program_id(0)
    used = b < nused_ref[0]
    changed = jnp.logical_or(b == 0, bexp_ref[b] != bexp_ref[jnp.maximum(b - 1, 0)])

    @pl.when(jnp.logical_and(used, changed))
    def _():
        w1b[...] = w1_ref[0, 0].astype(BF16)
        w3b[...] = w3_ref[0, 0].astype(BF16)
        w2b[...] = w2_ref[0, 0].astype(BF16)

    @pl.when(used)
    def _():
        bm = xb.shape[0]
        for j in range(ROW_TILE):
            lo, hi = _load_row_tile_chunk(xs_ref, j, bm)
            lo_cols, hi_cols = _chunk_cols(j)
            xb[:, lo_cols] = lo.astype(BF16)
            xb[:, hi_cols] = hi.astype(BF16)
        _store_row_tiles(ys_ref, _swiglu(xb[...], w1b, w3b, w2b))

    @pl.when(jnp.logical_not(used))
    def _():
        ys_ref[...] = jnp.zeros_like(ys_ref)


def _experts(block_exp, n_used, xs, w1, w3, w2, layer):
    n_blocks = xs.shape[0] // (EXPERT_BM * ROW_TILE)

    def last_used(b, nu):
        return jnp.minimum(b, nu[0] - 1)

    def w_spec(shape):
        return pl.BlockSpec((1, 1) + shape, lambda b, be, nu: (layer, be[last_used(b, nu)], 0, 0))

    grid_spec = pltpu.PrefetchScalarGridSpec(
        num_scalar_prefetch=2,
        grid=(n_blocks,),
        in_specs=[
            pl.BlockSpec((EXPERT_BM * ROW_TILE, LANES), lambda b, be, nu: (last_used(b, nu), 0)),
            w_spec((D_MODEL, D_EXPERT)), w_spec((D_MODEL, D_EXPERT)), w_spec((D_EXPERT, D_MODEL)),
        ],
        out_specs=pl.BlockSpec((EXPERT_BM * ROW_TILE, LANES), lambda b, be, nu: (b, 0)),
        scratch_shapes=[pltpu.VMEM((D_MODEL, D_EXPERT), BF16), pltpu.VMEM((D_MODEL, D_EXPERT), BF16),
                        pltpu.VMEM((D_EXPERT, D_MODEL), BF16), pltpu.VMEM((EXPERT_BM, D_MODEL), BF16)],
    )
    return pl.pallas_call(
        _expert_kernel,
        grid_spec=grid_spec,
        out_shape=jax.ShapeDtypeStruct(xs.shape, U32),
        compiler_params=_cparams("arbitrary"),
        name="moe_experts",
    )(block_exp, n_used, xs, w1, w3, w2)


def _combine_kernel(dcur_ref, dnext_ref, gate_ref, x_ref, sh_ref, ys_ref, g_ref, b_ref,
                    xo_ref, xbo_ref, buf, hbuf, sem):
    i = pl.program_id(0)
    tm = x_ref.shape[0]
    slot = i % 2

    def issue(dref, s):
        def body(it, c):
            for u in range(ISSUE_UNROLL):
                t = it * ISSUE_UNROLL + u
                for k in range(TOP_K):
                    pltpu.make_async_copy(_row_tile(ys_ref, dref[t * TOP_K + k]),
                                          _row_tile(buf.at[s, k], t), sem.at[s]).start(priority=k % 2)
            return c

        lax.fori_loop(0, tm // ISSUE_UNROLL, body, 0)

    @pl.when(i == 0)
    def _():
        issue(dcur_ref, slot)

    @pl.when(i + 1 < pl.num_programs(0))
    def _():
        issue(dnext_ref, 1 - slot)

    for k in range(TOP_K):
        pltpu.make_async_copy(ys_ref.at[pl.ds(0, tm * ROW_TILE)], buf.at[slot, k], sem.at[slot]).wait()

    gates = [jnp.broadcast_to(gate_ref[:, k:k + 1], (tm, LANES)) for k in range(TOP_K)]
    for j in range(ROW_TILE):
        lo_cols, hi_cols = _chunk_cols(j)
        acc_lo = sh_ref[:, lo_cols].astype(F32)
        acc_hi = sh_ref[:, hi_cols].astype(F32)
        for k in range(TOP_K):
            lo, hi = _load_row_tile_chunk(buf.at[slot, k], j, tm)
            acc_lo = acc_lo + gates[k] * lo
            acc_hi = acc_hi + gates[k] * hi
        hbuf[:, lo_cols] = acc_lo
        hbuf[:, hi_cols] = acc_hi
    y = _layer_norm(DEEPNORM_ALPHA * x_ref[...] + hbuf[...], g_ref[...], b_ref[...])
    xo_ref[...] = y
    xbo_ref[...] = y.astype(BF16)


def _combine(dest_flat, gate_t, x, shared, ys, p):
    t = x.shape[0]
    tm = min(COMBINE_TM, t)
    steps = t // tm
    row = lambda i: (i, 0)
    return pl.pallas_call(
        _combine_kernel,
        grid=(steps,),
        in_specs=[pl.BlockSpec((tm * TOP_K,), lambda i: (i,), memory_space=pltpu.SMEM),
                  pl.BlockSpec((tm * TOP_K,), lambda i: (jnp.minimum(i + 1, steps - 1),),
                               memory_space=pltpu.SMEM),
                  pl.BlockSpec((tm, TOP_K), row),
                  pl.BlockSpec((tm, D_MODEL), row),
                  pl.BlockSpec((tm, D_MODEL), row),
                  pl.BlockSpec(memory_space=pl.ANY),
                  _resident((1, D_MODEL), lambda i: (0, 0)),
                  _resident((1, D_MODEL), lambda i: (0, 0))],
        out_specs=[pl.BlockSpec((tm, D_MODEL), row), pl.BlockSpec((tm, D_MODEL), row)],
        out_shape=[jax.ShapeDtypeStruct((t, D_MODEL), F32),
                   jax.ShapeDtypeStruct((t, D_MODEL), BF16)],
        scratch_shapes=[pltpu.VMEM((2, TOP_K, tm * ROW_TILE, LANES), U32),
                        pltpu.VMEM((tm, D_MODEL), F32), pltpu.SemaphoreType.DMA((2,))],
        compiler_params=_cparams("arbitrary"),
        name="moe_combine_ln",
    )(dest_flat, dest_flat, gate_t, x, shared, ys, p["ln2_g"], p["ln2_b"])


def _n_expert_blocks(t):
    return -(-(t * TOP_K + N_EXPERTS * (EXPERT_BM - 1)) // EXPERT_BM)


def _moe(x, xb, xp, p, exp_w1, exp_w3, exp_w2, layer):
    t = x.shape[0]
    n_blocks = _n_expert_blocks(t)
    idx, gate, rank, cnt = _router(xb, p["router_w_t"], p["router_bias"])
    counts = cnt[:, 0]
    padded = (counts + EXPERT_BM - 1) // EXPERT_BM * EXPERT_BM
    pad_end = jnp.cumsum(padded)
    pad_start = pad_end - padded
    n_used = (pad_end[-1:] // EXPERT_BM).astype(I32)
    block_first = jnp.arange(n_blocks, dtype=I32) * EXPERT_BM
    block_exp = jnp.minimum(jnp.sum(pad_end[None, :] <= block_first[:, None], axis=1),
                            N_EXPERTS - 1).astype(I32)
    dest = _dest_rows(pad_start.astype(I32), idx, rank)
    dest_flat = dest.T.reshape(-1)
    xs, shared = _dispatch((pad_start + counts).astype(I32), (padded - counts).astype(I32), n_used,
                           dest_flat, xp, xb, p, n_blocks)
    ys = _experts(block_exp, n_used, xs, exp_w1, exp_w3, exp_w2, layer)
    return _combine(dest_flat, gate.T, x, shared, ys, p)


def _layer_params(l, b_gate, sinks, pool_w, pool_scale, conv_w, conv_b, lru_wa, lru_ba, lru_wx,
                  lru_bx, lru_lambda, w_branch, w_out, ln1_g, ln1_b, router_w, router_bias,
                  sh_w1, sh_w3, sh_w2, ln2_g, ln2_b):
    row = lambda a: a[l].reshape(1, -1).astype(F32)
    return dict(
        b_gate=row(b_gate), sinks=sinks[l].astype(F32),
        pool_w=pool_w[l].astype(BF16), pool_scale=row(pool_scale),
        conv_w=conv_w[l].astype(F32), conv_b=row(conv_b),
        lru_wa=_block_diag(lru_wa[l]).astype(BF16), lru_ba=row(lru_ba),
        lru_wx=_block_diag(lru_wx[l]).astype(BF16), lru_bx=row(lru_bx),
        lru_lambda=row(lru_lambda),
        w_branch=w_branch[l].astype(BF16), w_out=w_out[l].astype(BF16),
        ln1_g=row(ln1_g), ln1_b=row(ln1_b),
        router_w_t=router_w[l].T.astype(BF16), router_bias=router_bias[l].reshape(-1, 1).astype(F32),
        sh_w1=sh_w1[l].astype(BF16), sh_w3=sh_w3[l].astype(BF16), sh_w2=sh_w2[l].astype(BF16),
        ln2_g=row(ln2_g), ln2_b=row(ln2_b),
    )


def kernel(x, positions, w_in, b_gate, sinks, pool_w, pool_scale, conv_w, conv_b, lru_wa, lru_ba, lru_wx, lru_bx, lru_lambda, w_branch, w_out, ln1_g, ln1_b, router_w, router_bias, exp_w1, exp_w3, exp_w2, sh_w1, sh_w3, sh_w2, ln2_g, ln2_b):
    batch, seq_len, d = x.shape
    assert d == D_MODEL and seq_len % max(ATTN_TQ, MIX_TB) == 0
    t = batch * seq_len
    cos, sin = _rope_tables(positions)
    xf = x.reshape(t, d).astype(F32)
    xb = xf.astype(BF16)
    for l in range(DEPTH):
        p = _layer_params(l, b_gate, sinks, pool_w, pool_scale, conv_w, conv_b, lru_wa, lru_ba,
                          lru_wx, lru_bx, lru_lambda, w_branch, w_out, ln1_g, ln1_b, router_w,
                          router_bias, sh_w1, sh_w3, sh_w2, ln2_g, ln2_b)
        proj = _in_proj(xb, w_in, l)
        ya = _attention(proj, cos, sin, p["sinks"], seq_len)
        yp, yl = _mixers(proj, p, batch, seq_len)
        xf, xb, xp = _merge(proj, ya, yp, yl, xf, p)
        xf, xb = _moe(xf, xb, xp, p, exp_w1, exp_w3, exp_w2, l)
    return xf.reshape(batch, seq_len, d).astype(x.dtype)
```

```python
import functools

import jax
import jax.numpy as jnp
from jax import lax
from jax.experimental import pallas as pl
from jax.experimental.pallas import tpu as pltpu
from jax.experimental.pallas import tpu_sc as plsc

F32 = jnp.float32
BF16 = jnp.bfloat16
I32 = jnp.int32
U32 = jnp.uint32

D_MODEL = 2048
DEPTH = 2
CHUNK = 64
HEAD_DIM = 64
HALF_HEAD = HEAD_DIM // 2
N_HEADS = 16
N_KV_HEADS = 4
KV_GROUP = N_HEADS // N_KV_HEADS
ATTN_WIDTH = N_HEADS * HEAD_DIM
KV_WIDTH = N_KV_HEADS * HEAD_DIM
WINDOW_CHUNKS = 2
ROPE_THETA = 10000.0
POOL_WINDOWS = (2, 4, 8, 16)
POOL_WIDTH = 1024
POOL_GROUP = POOL_WIDTH // len(POOL_WINDOWS)
LRU_WIDTH = 1024
LRU_BLOCKS = 16
LRU_BLOCK = LRU_WIDTH // LRU_BLOCKS
CONV_WIDTH = 4
LRU_C = 8.0
N_BRANCH = 3
GATE_WIDTH = N_BRANCH * D_MODEL
IN_WIDTH = ATTN_WIDTH + 2 * KV_WIDTH + POOL_WIDTH + 2 * LRU_WIDTH + GATE_WIDTH
N_EXPERTS = 64
TOP_K = 8
D_EXPERT = 512
ROUTED_SCALE = 2.5
LN_EPS = 1e-5
DEEPNORM_ALPHA = (2 * DEPTH) ** 0.25

COL_GATE = 0
COL_Q = GATE_WIDTH
COL_POOL = COL_Q + ATTN_WIDTH
COL_LRU = COL_POOL + POOL_WIDTH
COL_GELU = COL_LRU + LRU_WIDTH
COL_K = COL_GELU + LRU_WIDTH
COL_V = COL_K + KV_WIDTH

LANES = 128
SUBLANES = 8
MXU_DIM = 256
VMEM_LIMIT_BYTES = 56 * 1024 * 1024

PROJ_TM = 512
PROJ_TN = 1536
ATTN_TQ = 256
ATTN_HALO = WINDOW_CHUNKS * CHUNK
MIX_TB = 256
POOL_HALO = 16
CONV_HALO = 8
MERGE_TM = 256
ROUTER_TM = 512
DEST_TM = 2048
EXPERT_BM = 512
COMBINE_TM = 256
PROJ_SRC = 512
NEG_BIG = -1e30

SUB_WORDS = 256
SUB_COLS = 2 * SUB_WORDS
SUB_ROWS = D_MODEL // SUB_COLS
SC_WINDOW = 128


def _cparams(*sem):
    return pltpu.CompilerParams(dimension_semantics=sem, vmem_limit_bytes=VMEM_LIMIT_BYTES)


def _resident(shape, index_map):
    return pl.BlockSpec(shape, index_map, pipeline_mode=pl.Buffered(1))


def _pack_rows(lo, hi):
    lo_u = lax.bitcast_convert_type(lo.astype(BF16).astype(F32), U32)
    hi_u = lax.bitcast_convert_type(hi.astype(BF16).astype(F32), U32)
    return (lo_u >> 16) | (hi_u & jnp.uint32(0xFFFF0000))


def _unpack_rows(w):
    lo = lax.bitcast_convert_type(w << 16, F32)
    hi = lax.bitcast_convert_type(w & jnp.uint32(0xFFFF0000), F32)
    return lo, hi


def _piece_cols(j):
    return (slice(j * SUB_COLS, j * SUB_COLS + SUB_WORDS),
            slice(j * SUB_COLS + SUB_WORDS, (j + 1) * SUB_COLS))


def _store_pieces(ref, y):
    for j in range(SUB_ROWS):
        lo_cols, hi_cols = _piece_cols(j)
        ref[j] = _pack_rows(y[:, lo_cols], y[:, hi_cols])


def _layer_norm(y, g, b):
    mu = jnp.mean(y, axis=-1, keepdims=True)
    d = y - mu
    var = jnp.mean(d * d, axis=-1, keepdims=True)
    return d * lax.rsqrt(var + LN_EPS) * g + b


def _rope_kernel(pos_ref, inv_ref, cos_ref, sin_ref):
    ang = pos_ref[...].astype(F32) * inv_ref[...]
    lane = lax.broadcasted_iota(I32, ang.shape, 1)
    first_half = (lane % HEAD_DIM) < HALF_HEAD
    cos_ref[...] = jnp.cos(ang)
    s = jnp.sin(ang)
    sin_ref[...] = jnp.where(first_half, -s, s)


def _rope_tables(positions):
    t = positions.size
    tm = min(1024, t)
    inv_freq = 1.0 / (ROPE_THETA ** (jnp.arange(0, HEAD_DIM, 2, dtype=F32) / HEAD_DIM))
    inv = jnp.tile(inv_freq, LANES // HALF_HEAD).reshape(1, LANES)
    return pl.pallas_call(
        _rope_kernel,
        grid=(t // tm,),
        in_specs=[pl.BlockSpec((tm, 1), lambda i: (i, 0)),
                  pl.BlockSpec((1, LANES), lambda i: (0, 0))],
        out_specs=[pl.BlockSpec((tm, LANES), lambda i: (i, 0))] * 2,
        out_shape=[jax.ShapeDtypeStruct((t, LANES), F32)] * 2,
        compiler_params=_cparams("parallel"),
        name="rope_tables",
    )(positions.reshape(t, 1), inv)


def _proj_source_blocks():
    ref_order = (("q", ATTN_WIDTH), ("k", KV_WIDTH), ("v", KV_WIDTH), ("pool", POOL_WIDTH),
                 ("lru", LRU_WIDTH), ("gelu", LRU_WIDTH), ("gate", GATE_WIDTH))
    start, col = {}, 0
    for name, width in ref_order:
        start[name] = col
        col += width
    assert start["v"] == start["k"] + KV_WIDTH and 2 * KV_WIDTH == PROJ_SRC
    out_order = (("gate", GATE_WIDTH), ("q", ATTN_WIDTH), ("pool", POOL_WIDTH), ("lru", LRU_WIDTH),
                 ("gelu", LRU_WIDTH), ("k", 2 * KV_WIDTH))
    blocks = []
    for name, width in out_order:
        assert start[name] % PROJ_SRC == 0 and width % PROJ_SRC == 0
        blocks += [start[name] // PROJ_SRC + c for c in range(width // PROJ_SRC)]
    return blocks


def _in_proj_kernel(src_ref, x_ref, *refs):
    w_refs, o_ref, wb = refs[:-2], refs[-2], refs[-1]

    @pl.when(pl.program_id(1) == 0)
    def _():
        for c, w_ref in enumerate(w_refs):
            wb[:, c * PROJ_SRC:(c + 1) * PROJ_SRC] = w_ref[0].astype(BF16)

    o_ref[...] = jnp.dot(x_ref[...], wb[...], preferred_element_type=F32).astype(o_ref.dtype)


def _in_proj(xb, w_in, layer):
    t, d = xb.shape
    n = w_in.shape[2]
    tm = min(PROJ_TM, t)
    per = PROJ_TN // PROJ_SRC
    src = jnp.asarray(_proj_source_blocks(), I32)

    def w_spec(c):
        return pl.BlockSpec((1, d, PROJ_SRC), lambda j, i, s: (layer, 0, s[per * j + c]))

    grid_spec = pltpu.PrefetchScalarGridSpec(
        num_scalar_prefetch=1,
        grid=(n // PROJ_TN, t // tm),
        in_specs=[pl.BlockSpec((tm, d), lambda j, i, s: (i, 0))] + [w_spec(c) for c in range(per)],
        out_specs=pl.BlockSpec((tm, PROJ_TN), lambda j, i, s: (i, j)),
        scratch_shapes=[pltpu.VMEM((d, PROJ_TN), BF16)],
    )
    return pl.pallas_call(
        _in_proj_kernel,
        grid_spec=grid_spec,
        out_shape=jax.ShapeDtypeStruct((t, n), BF16),
        compiler_params=_cparams("arbitrary", "arbitrary"),
        name="in_proj",
    )(src, xb, *([w_in] * per))


def _rope(x, cos, sin):
    w = x.shape[1]
    reps = w // LANES
    c = jnp.concatenate([cos] * reps, axis=1)
    s = jnp.concatenate([sin] * reps, axis=1)
    lane = lax.broadcasted_iota(I32, x.shape, 1)
    first_half = (lane % HEAD_DIM) < HALF_HEAD
    rot = jnp.where(first_half, pltpu.roll(x, w - HALF_HEAD, 1), pltpu.roll(x, HALF_HEAD, 1))
    return x * c + rot * s


def _attn_kernel(q_ref, k_ref, v_ref, kh_ref, vh_ref, cq_ref, sq_ref, ch_ref, sh_ref, sink_ref,
                 o_ref, s_buf, p_buf, r_buf, *, steps_per_seq):
    first = (pl.program_id(0) % steps_per_seq) == 0
    q = _rope(q_ref[...].astype(F32), cq_ref[...], sq_ref[...]) * (HEAD_DIM ** -0.5)
    q = q.astype(BF16)
    k_own = _rope(k_ref[...].astype(F32), cq_ref[...], sq_ref[...])
    k_halo = _rope(kh_ref[...].astype(F32), ch_ref[...], sh_ref[...])
    k_all = jnp.concatenate([k_halo, k_own], axis=0).astype(BF16)
    v_all = jnp.concatenate([vh_ref[...], v_ref[...]], axis=0)
    span = (WINDOW_CHUNKS + 1) * CHUNK
    rows = KV_GROUP * CHUNK
    chunk_rows = N_KV_HEADS * rows
    n_chunks = q.shape[0] // CHUNK
    win_chunk = lax.broadcasted_iota(I32, (rows, span), 1) // CHUNK
    oldest = jnp.where(first, 0, -WINDOW_CHUNKS)

    def tile_rows(c, g):
        return slice(c * chunk_rows + g * rows, c * chunk_rows + (g + 1) * rows)

    def kv_cols(g):
        return slice(g * HEAD_DIM, (g + 1) * HEAD_DIM)

    for c in range(n_chunks):
        q_c = q[c * CHUNK:(c + 1) * CHUNK, :]
        k_win = k_all[c * CHUNK:c * CHUNK + span, :]
        for g in range(N_KV_HEADS):
            qs = jnp.concatenate(
                [q_c[:, (g * KV_GROUP + i) * HEAD_DIM:(g * KV_GROUP + i + 1) * HEAD_DIM]
                 for i in range(KV_GROUP)], axis=0)
            s = lax.dot_general(qs, k_win[:, kv_cols(g)], (((1,), (1,)), ((), ())),
                                preferred_element_type=F32)
            if c < WINDOW_CHUNKS:
                s = jnp.where(win_chunk + (c - WINDOW_CHUNKS) >= oldest, s, NEG_BIG)
            s_buf[tile_rows(c, g), :] = s
    sink = sink_ref[...]
    for c in range(n_chunks):
        blk = slice(c * chunk_rows, (c + 1) * chunk_rows)
        s = s_buf[blk, :]
        m = jnp.maximum(jnp.max(s, axis=1, keepdims=True), sink)
        p = jnp.exp(s - m)
        p_buf[blk, :] = p.astype(BF16)
        r_buf[blk, :] = 1.0 / (jnp.sum(p, axis=1, keepdims=True) + jnp.exp(sink - m))
    for c in range(n_chunks):
        v_win = v_all[c * CHUNK:c * CHUNK + span, :]
        heads = []
        for g in range(N_KV_HEADS):
            o = jnp.dot(p_buf[tile_rows(c, g), :], v_win[:, kv_cols(g)], preferred_element_type=F32)
            o = o * r_buf[tile_rows(c, g), :]
            heads += [o[i * CHUNK:(i + 1) * CHUNK, :] for i in range(KV_GROUP)]
        o_ref[c * CHUNK:(c + 1) * CHUNK, :] = jnp.concatenate(heads, axis=1).astype(o_ref.dtype)


def _attention(proj, cos, sin, sinks, seq_len):
    t = proj.shape[0]
    tq = ATTN_TQ
    steps_per_seq = seq_len // tq
    halo_per_tq = tq // ATTN_HALO
    q_blk = COL_Q // ATTN_WIDTH
    k_blk = COL_K // KV_WIDTH
    v_blk = COL_V // KV_WIDTH

    def halo_row(i):
        return jnp.maximum(i * halo_per_tq - 1, 0)

    span = (WINDOW_CHUNKS + 1) * CHUNK
    tile_rows = (tq // CHUNK) * N_HEADS * CHUNK
    sink_rows = jnp.repeat(sinks, CHUNK).reshape(N_HEADS * CHUNK, 1)
    return pl.pallas_call(
        functools.partial(_attn_kernel, steps_per_seq=steps_per_seq),
        grid=(t // tq,),
        in_specs=[
            pl.BlockSpec((tq, ATTN_WIDTH), lambda i: (i, q_blk)),
            pl.BlockSpec((tq, KV_WIDTH), lambda i: (i, k_blk)),
            pl.BlockSpec((tq, KV_WIDTH), lambda i: (i, v_blk)),
            pl.BlockSpec((ATTN_HALO, KV_WIDTH), lambda i: (halo_row(i), k_blk)),
            pl.BlockSpec((ATTN_HALO, KV_WIDTH), lambda i: (halo_row(i), v_blk)),
            pl.BlockSpec((tq, LANES), lambda i: (i, 0)),
            pl.BlockSpec((tq, LANES), lambda i: (i, 0)),
            pl.BlockSpec((ATTN_HALO, LANES), lambda i: (halo_row(i), 0)),
            pl.BlockSpec((ATTN_HALO, LANES), lambda i: (halo_row(i), 0)),
            pl.BlockSpec((N_HEADS * CHUNK, 1), lambda i: (0, 0)),
        ],
        out_specs=pl.BlockSpec((tq, ATTN_WIDTH), lambda i: (i, 0)),
        out_shape=jax.ShapeDtypeStruct((t, ATTN_WIDTH), BF16),
        scratch_shapes=[pltpu.VMEM((tile_rows, span), F32), pltpu.VMEM((tile_rows, span), BF16),
                        pltpu.VMEM((tile_rows, 1), F32)],
        compiler_params=_cparams("parallel"),
        name="swa_attention",
    )(proj, proj, proj, proj, proj, cos, sin, cos, sin, sink_rows)


def _mix_kernel(up_ref, ul_ref, ug_ref, pw_ref, ps_ref, cw_ref, cb_ref, wa_ref, ba_ref, wx_ref,
                bx_ref, lam_ref, yp_ref, yl_ref, halo_p, halo_l, h_carry):
    s = pl.program_id(1)
    tb = up_ref.shape[0]

    @pl.when(s == 0)
    def _():
        halo_p[...] = jnp.zeros_like(halo_p)
        halo_l[...] = jnp.zeros_like(halo_l)
        h_carry[...] = jnp.zeros_like(h_carry)

    u = up_ref[...].astype(F32)
    ext = jnp.concatenate([halo_p[...], u], axis=0)
    s2 = ext + pltpu.roll(ext, 1, 0)
    s4 = s2 + pltpu.roll(s2, 2, 0)
    s8 = s4 + pltpu.roll(s4, 4, 0)
    s16 = s8 + pltpu.roll(s8, 8, 0)
    t_idx = (s * tb + lax.broadcasted_iota(I32, (tb, 1), 0)).astype(F32)
    for g, (win, ws) in enumerate(zip(POOL_WINDOWS, (s2, s4, s8, s16))):
        cols = slice(g * POOL_GROUP, (g + 1) * POOL_GROUP)
        count = jnp.minimum(t_idx + 1.0, float(win))
        pooled = ws[POOL_HALO:, cols] / count - u[:, cols]
        y = jnp.dot(pooled.astype(BF16), pw_ref[g], preferred_element_type=F32)
        yp_ref[:, cols] = (y * ps_ref[:, cols]).astype(yp_ref.dtype)
    halo_p[...] = u[tb - POOL_HALO:, :]

    ul = ul_ref[...].astype(F32)
    extl = jnp.concatenate([halo_l[...], ul], axis=0)
    xc = cb_ref[...] + cw_ref[CONV_WIDTH - 1:CONV_WIDTH, :] * ul
    for j in range(1, CONV_WIDTH):
        w_j = cw_ref[CONV_WIDTH - 1 - j:CONV_WIDTH - j, :]
        xc = xc + w_j * pltpu.roll(extl, j, 0)[CONV_HALO:, :]
    halo_l[...] = ul[tb - CONV_HALO:, :]
    xcb = xc.astype(BF16)
    r_parts, i_parts = [], []
    for j in range(LRU_WIDTH // MXU_DIM):
        cols = slice(j * MXU_DIM, (j + 1) * MXU_DIM)
        r_parts.append(jnp.dot(xcb[:, cols], wa_ref[j], preferred_element_type=F32))
        i_parts.append(jnp.dot(xcb[:, cols], wx_ref[j], preferred_element_type=F32))
    r = jax.nn.sigmoid(jnp.concatenate(r_parts, axis=1) + ba_ref[...])
    ig = jax.nn.sigmoid(jnp.concatenate(i_parts, axis=1) + bx_ref[...])
    z = -lam_ref[...]
    softplus = jnp.maximum(z, 0.0) + jnp.log1p(jnp.exp(-jnp.abs(z)))
    log_a = (-LRU_C) * r * softplus
    a = jnp.exp(log_a)
    b = jnp.sqrt(1.0 - jnp.exp(2.0 * log_a)) * (ig * xc)
    row_in_group = lax.broadcasted_iota(I32, a.shape, 0) % SUBLANES
    d = 1
    while d < SUBLANES:
        keep = row_in_group >= d
        a_prev = jnp.where(keep, pltpu.roll(a, d, 0), 1.0)
        b_prev = jnp.where(keep, pltpu.roll(b, d, 0), 0.0)
        b = a * b_prev + b
        a = a * a_prev
        d *= 2
    gelu = jax.nn.gelu(ug_ref[...].astype(F32))
    h_prev = h_carry[...]
    pair = 2 * SUBLANES
    for g in range(tb // pair):
        hs = []
        for half in range(2):
            rows = slice(g * pair + half * SUBLANES, g * pair + (half + 1) * SUBLANES)
            h = a[rows, :] * h_prev + b[rows, :]
            h_prev = h[SUBLANES - 1:SUBLANES, :]
            hs.append(h)
        rows = slice(g * pair, (g + 1) * pair)
        yl_ref[rows, :] = (jnp.concatenate(hs, axis=0) * gelu[rows, :]).astype(yl_ref.dtype)
    h_carry[...] = h_prev


def _block_diag(w):
    per = MXU_DIM // LRU_BLOCK
    w = w.reshape(LRU_BLOCKS // per, per, LRU_BLOCK, LRU_BLOCK)
    eye = jnp.eye(per, dtype=w.dtype)
    out = jnp.einsum("gpij,pq->gpiqj", w, eye)
    return out.reshape(LRU_BLOCKS // per, MXU_DIM, MXU_DIM)


def _mixers(proj, p, batch, seq_len):
    t = proj.shape[0]
    tb = min(MIX_TB, seq_len)
    steps = seq_len // tb

    def rows(bi, si):
        return bi * steps + si

    def col_spec(col):
        blk = col // POOL_WIDTH
        return pl.BlockSpec((tb, POOL_WIDTH), lambda bi, si: (rows(bi, si), blk))

    def const(shape):
        nd = len(shape)
        return pl.BlockSpec(shape, lambda bi, si: (0,) * nd)

    out_spec = pl.BlockSpec((tb, POOL_WIDTH), lambda bi, si: (rows(bi, si), 0))
    return pl.pallas_call(
        _mix_kernel,
        grid=(batch, steps),
        in_specs=[col_spec(COL_POOL), col_spec(COL_LRU), col_spec(COL_GELU),
                  const((len(POOL_WINDOWS), POOL_GROUP, POOL_GROUP)), const((1, POOL_WIDTH)),
                  const((CONV_WIDTH, LRU_WIDTH)), const((1, LRU_WIDTH)),
                  const((LRU_WIDTH // MXU_DIM, MXU_DIM, MXU_DIM)), const((1, LRU_WIDTH)),
                  const((LRU_WIDTH // MXU_DIM, MXU_DIM, MXU_DIM)), const((1, LRU_WIDTH)),
                  const((1, LRU_WIDTH))],
        out_specs=[out_spec, out_spec],
        out_shape=[jax.ShapeDtypeStruct((t, POOL_WIDTH), BF16),
                   jax.ShapeDtypeStruct((t, LRU_WIDTH), BF16)],
        scratch_shapes=[pltpu.VMEM((POOL_HALO, POOL_WIDTH), F32),
                        pltpu.VMEM((CONV_HALO, LRU_WIDTH), F32),
                        pltpu.VMEM((1, LRU_WIDTH), F32)],
        compiler_params=_cparams("arbitrary", "arbitrary"),
        name="pool_rglru",
    )(proj, proj, proj, p["pool_w"], p["pool_scale"], p["conv_w"], p["conv_b"],
      p["lru_wa"], p["lru_ba"], p["lru_wx"], p["lru_bx"], p["lru_lambda"])


def _merge_kernel(gl_ref, ya_ref, yp_ref, yl_ref, x_ref, bg_ref, wb_ref, wo_ref, g_ref, b_ref,
                  xo_ref, xb_ref, xp_ref):
    merged = None
    for j, y_ref in enumerate((ya_ref, yp_ref, yl_ref)):
        cols = slice(j * D_MODEL, (j + 1) * D_MODEL)
        z = jnp.dot(y_ref[...], wb_ref[j], preferred_element_type=F32)
        gate = jax.nn.sigmoid(gl_ref[:, cols].astype(F32) + bg_ref[:, cols])
        merged = gate * z if merged is None else merged + gate * z
    h = jnp.dot(merged.astype(BF16), wo_ref[...], preferred_element_type=F32)
    y = _layer_norm(DEEPNORM_ALPHA * x_ref[...] + h, g_ref[...], b_ref[...])
    xo_ref[...] = y
    xb_ref[...] = y.astype(BF16)
    _store_pieces(xp_ref, y)


def _merge(proj, ya, yp, yl, x, x_row0, p):
    t = proj.shape[0]
    tm = min(MERGE_TM, t)
    assert x_row0 % tm == 0
    row = lambda i: (i, 0)
    return pl.pallas_call(
        _merge_kernel,
        grid=(t // tm,),
        in_specs=[pl.BlockSpec((tm, GATE_WIDTH), row),
                  pl.BlockSpec((tm, ATTN_WIDTH), row),
                  pl.BlockSpec((tm, POOL_WIDTH), row),
                  pl.BlockSpec((tm, LRU_WIDTH), row),
                  pl.BlockSpec((tm, D_MODEL), lambda i: (i + x_row0 // tm, 0)),
                  _resident((1, GATE_WIDTH), lambda i: (0, 0)),
                  _resident((N_BRANCH, ATTN_WIDTH, D_MODEL), lambda i: (0, 0, 0)),
                  _resident((D_MODEL, D_MODEL), lambda i: (0, 0)),
                  _resident((1, D_MODEL), lambda i: (0, 0)),
                  _resident((1, D_MODEL), lambda i: (0, 0))],
        out_specs=[pl.BlockSpec((tm, D_MODEL), row), pl.BlockSpec((tm, D_MODEL), row),
                   pl.BlockSpec((SUB_ROWS, tm, SUB_WORDS), lambda i: (0, i, 0))],
        out_shape=[jax.ShapeDtypeStruct((t, D_MODEL), F32),
                   jax.ShapeDtypeStruct((t, D_MODEL), BF16),
                   jax.ShapeDtypeStruct((SUB_ROWS, t, SUB_WORDS), U32)],
        compiler_params=_cparams("parallel"),
        name="merge_ln",
    )(proj, ya, yp, yl, x, p["b_gate"], p["w_branch"], p["w_out"], p["ln1_g"], p["ln1_b"])


def _router_kernel(x_ref, rw_ref, rb_ref, idx_ref, gate_ref, rank_ref, cnt_ref, carry):
    @pl.when(pl.program_id(0) == 0)
    def _():
        carry[...] = jnp.zeros_like(carry)

    tm = x_ref.shape[0]
    logits = lax.dot_general(rw_ref[...], x_ref[...], (((1,), (1,)), ((), ())),
                             preferred_element_type=F32)
    scores = jax.nn.sigmoid(logits)
    sel = scores + rb_ref[...]
    expert = lax.broadcasted_iota(I32, scores.shape, 0)
    idxs, tops = [], []
    chosen = jnp.zeros(scores.shape, F32)
    for _ in range(TOP_K):
        best = jnp.max(sel, axis=0, keepdims=True)
        ik = jnp.min(jnp.where(sel == best, expert, N_EXPERTS), axis=0, keepdims=True)
        hit = expert == ik
        tops.append(jnp.sum(jnp.where(hit, scores, 0.0), axis=0, keepdims=True))
        idxs.append(ik)
        chosen = jnp.where(hit, 1.0, chosen)
        sel = jnp.where(hit, -jnp.inf, sel)
    top = jnp.concatenate(tops, axis=0)
    idx = jnp.concatenate(idxs, axis=0)
    gate_ref[...] = top / jnp.sum(top, axis=0, keepdims=True) * ROUTED_SCALE
    idx_ref[...] = idx
    earlier = (lax.broadcasted_iota(I32, (tm, tm), 0) < lax.broadcasted_iota(I32, (tm, tm), 1))
    before = jnp.dot(chosen.astype(BF16), jnp.where(earlier, 1.0, 0.0).astype(BF16),
                     preferred_element_type=F32)
    base = before + carry[:, 0:1]
    ranks = [jnp.sum(jnp.where(expert == idxs[k], base, 0.0), axis=0, keepdims=True)
             for k in range(TOP_K)]
    rank_ref[...] = jnp.concatenate(ranks, axis=0).astype(I32)
    carry[...] = carry[...] + jnp.sum(chosen, axis=1, keepdims=True)
    cnt_ref[...] = carry[...].astype(I32)


def _router(x, rw_t, rbias):
    t = x.shape[0]
    tm = min(ROUTER_TM, t)
    tok = lambda i: (0, i)
    return pl.pallas_call(
        _router_kernel,
        grid=(t // tm,),
        in_specs=[pl.BlockSpec((tm, D_MODEL), lambda i: (i, 0)),
                  pl.BlockSpec((N_EXPERTS, D_MODEL), lambda i: (0, 0)),
                  pl.BlockSpec((N_EXPERTS, 1), lambda i: (0, 0))],
        out_specs=[pl.BlockSpec((TOP_K, tm), tok), pl.BlockSpec((TOP_K, tm), tok),
                   pl.BlockSpec((TOP_K, tm), tok),
                   pl.BlockSpec((N_EXPERTS, LANES), lambda i: (0, 0))],
        out_shape=[jax.ShapeDtypeStruct((TOP_K, t), I32), jax.ShapeDtypeStruct((TOP_K, t), F32),
                   jax.ShapeDtypeStruct((TOP_K, t), I32),
                   jax.ShapeDtypeStruct((N_EXPERTS, LANES), I32)],
        scratch_shapes=[pltpu.VMEM((N_EXPERTS, LANES), F32)],
        compiler_params=_cparams("arbitrary"),
        name="router_topk",
    )(x, rw_t, rbias)


def _dest_kernel(start_ref, idx_ref, rank_ref, dest_ref):
    idx = idx_ref[...]
    dest = rank_ref[...]
    for e in range(N_EXPERTS):
        dest = dest + jnp.where(idx == e, start_ref[e], 0)
    dest_ref[...] = dest


def _dest_rows(pad_start, idx, rank):
    t = idx.shape[1]
    tm = min(DEST_TM, t)
    spec = pl.BlockSpec((TOP_K, tm), lambda i, s: (0, i))
    return pl.pallas_call(
        _dest_kernel,
        grid_spec=pltpu.PrefetchScalarGridSpec(
            num_scalar_prefetch=1, grid=(t // tm,), in_specs=[spec, spec], out_specs=spec),
        out_shape=jax.ShapeDtypeStruct((TOP_K, t), I32),
        compiler_params=_cparams("parallel"),
        name="dest_rows",
    )(pad_start, idx, rank)


def _swiglu(xb, w1_ref, w3_ref, w2_ref):
    a = jnp.dot(xb, w1_ref[...], preferred_element_type=F32)
    g = jnp.dot(xb, w3_ref[...], preferred_element_type=F32)
    hid = (a * jax.nn.sigmoid(a) * g).astype(BF16)
    return jnp.dot(hid, w2_ref[...], preferred_element_type=F32)


def _sc_mesh():
    return plsc.VectorSubcoreMesh(core_axis_name="core", subcore_axis_name="subcore")


def _piece_ids(dest, n_rows):
    plane = jnp.arange(SUB_ROWS, dtype=I32)[:, None, None] * n_rows
    return dest[None, :, :] + plane


def _sc_dispatch(xp, dest, n_rows):
    t = xp.shape[1]
    pieces = SUB_ROWS * t
    ids = jnp.swapaxes(_piece_ids(dest, n_rows), 0, 1).reshape(TOP_K, pieces)

    @pl.kernel(out_type=jax.ShapeDtypeStruct((SUB_ROWS * n_rows, SUB_WORDS), U32), mesh=_sc_mesh(),
               scratch_types=[])
    def scatter(x_hbm, i_hbm, o_hbm):
        def body(x_vmem, i_vmem):
            for k in range(TOP_K):
                pltpu.sync_copy(x_vmem, o_hbm.at[i_vmem.at[k]])

        pltpu.emit_pipeline(
            body,
            grid=(pieces // SC_WINDOW,),
            in_specs=[pl.BlockSpec((SC_WINDOW, SUB_WORDS), lambda i: (i, 0)),
                      pl.BlockSpec((TOP_K, SC_WINDOW), lambda i: (0, i))],
            out_specs=[],
            core_axis_name=("core", "subcore"),
            dimension_semantics=(pltpu.PARALLEL,),
        )(x_hbm, i_hbm)

    out = scatter(xp.reshape(pieces, SUB_WORDS), ids)
    return out.reshape(SUB_ROWS, n_rows, SUB_WORDS)


def _sc_gather(ys, dest):
    n_rows = ys.shape[1]
    t = dest.shape[1]
    pieces = SUB_ROWS * TOP_K * t
    ids = _piece_ids(dest, n_rows).reshape(1, pieces)

    @pl.kernel(out_type=jax.ShapeDtypeStruct((pieces, SUB_WORDS), U32), mesh=_sc_mesh(),
               scratch_types=[])
    def gather(y_hbm, i_hbm, o_hbm):
        def body(i_vmem, o_vmem):
            pltpu.sync_copy(y_hbm.at[i_vmem.at[0]], o_vmem)

        pltpu.emit_pipeline(
            body,
            grid=(pieces // SC_WINDOW,),
            in_specs=[pl.BlockSpec((1, SC_WINDOW), lambda i: (0, i))],
            out_specs=[pl.BlockSpec((SC_WINDOW, SUB_WORDS), lambda i: (i, 0))],
            core_axis_name=("core", "subcore"),
            dimension_semantics=(pltpu.PARALLEL,),
        )(i_hbm, o_hbm)

    out = gather(ys.reshape(SUB_ROWS * n_rows, SUB_WORDS), ids)
    return out.reshape(SUB_ROWS, TOP_K * t, SUB_WORDS)


def _expert_kernel(bexp_ref, blive_ref, nused_ref, xs_ref, w1_ref, w3_ref, w2_ref, ys_ref, w1b, w3b,
                   w2b, xb):
    b = pl.program_id(0)
    used = b < nused_ref[0]
    changed = jnp.logical_or(b == 0, bexp_ref[b] != bexp_ref[jnp.maximum(b - 1, 0)])

    @pl.when(jnp.logical_and(used, changed))
    def _():
        w1b[...] = w1_ref[0, 0].astype(BF16)
        w3b[...] = w3_ref[0, 0].astype(BF16)
        w2b[...] = w2_ref[0, 0].astype(BF16)

    @pl.when(used)
    def _():
        live = lax.broadcasted_iota(I32, (xb.shape[0], SUB_WORDS), 0) < blive_ref[b]
        for j in range(SUB_ROWS):
            lo, hi = _unpack_rows(jnp.where(live, xs_ref[j], jnp.uint32(0)))
            lo_cols, hi_cols = _piece_cols(j)
            xb[:, lo_cols] = lo.astype(BF16)
            xb[:, hi_cols] = hi.astype(BF16)
        _store_pieces(ys_ref, _swiglu(xb[...], w1b, w3b, w2b))

    @pl.when(jnp.logical_not(used))
    def _():
        ys_ref[...] = jnp.zeros_like(ys_ref)


def _experts(block_exp, block_live, n_used, xs, w1, w3, w2, layer):
    n_blocks = xs.shape[1] // EXPERT_BM

    def last_used(b, nu):
        return jnp.minimum(b, nu[0] - 1)

    def w_spec(shape):
        return pl.BlockSpec((1, 1) + shape, lambda b, be, bl, nu: (layer, be[last_used(b, nu)], 0, 0))

    grid_spec = pltpu.PrefetchScalarGridSpec(
        num_scalar_prefetch=3,
        grid=(n_blocks,),
        in_specs=[
            pl.BlockSpec((SUB_ROWS, EXPERT_BM, SUB_WORDS), lambda b, be, bl, nu: (0, last_used(b, nu), 0)),
            w_spec((D_MODEL, D_EXPERT)), w_spec((D_MODEL, D_EXPERT)), w_spec((D_EXPERT, D_MODEL)),
        ],
        out_specs=pl.BlockSpec((SUB_ROWS, EXPERT_BM, SUB_WORDS), lambda b, be, bl, nu: (0, b, 0)),
        scratch_shapes=[pltpu.VMEM((D_MODEL, D_EXPERT), BF16), pltpu.VMEM((D_MODEL, D_EXPERT), BF16),
                        pltpu.VMEM((D_EXPERT, D_MODEL), BF16), pltpu.VMEM((EXPERT_BM, D_MODEL), BF16)],
    )
    return pl.pallas_call(
        _expert_kernel,
        grid_spec=grid_spec,
        out_shape=jax.ShapeDtypeStruct(xs.shape, U32),
        compiler_params=_cparams("arbitrary"),
        name="moe_experts",
    )(block_exp, block_live, n_used, xs, w1, w3, w2)


def _combine_kernel(gate_ref, x_ref, xb_ref, s1_ref, s3_ref, s2_ref, g_ref, b_ref, *refs):
    yk_refs = refs[:TOP_K]
    xo_ref, xbo_ref, hbuf = refs[TOP_K:]
    tm = x_ref.shape[0]
    shared = _swiglu(xb_ref[...], s1_ref, s3_ref, s2_ref)
    gates = [jnp.broadcast_to(gate_ref[:, k:k + 1], (tm, SUB_WORDS)) for k in range(TOP_K)]
    for j in range(SUB_ROWS):
        lo_cols, hi_cols = _piece_cols(j)
        acc_lo = shared[:, lo_cols]
        acc_hi = shared[:, hi_cols]
        for k in range(TOP_K):
            lo, hi = _unpack_rows(yk_refs[k][j])
            acc_lo = acc_lo + gates[k] * lo
            acc_hi = acc_hi + gates[k] * hi
        hbuf[:, lo_cols] = acc_lo
        hbuf[:, hi_cols] = acc_hi
    y = _layer_norm(DEEPNORM_ALPHA * x_ref[...] + hbuf[...], g_ref[...], b_ref[...])
    xo_ref[...] = y
    xbo_ref[...] = y.astype(BF16)


def _combine(gate_t, x, xb, yg, p):
    t = x.shape[0]
    tm = min(COMBINE_TM, t)
    steps = t // tm
    row = lambda i: (i, 0)

    def slot_spec(k):
        return pl.BlockSpec((SUB_ROWS, tm, SUB_WORDS), lambda i: (0, k * steps + i, 0))

    return pl.pallas_call(
        _combine_kernel,
        grid=(steps,),
        in_specs=[pl.BlockSpec((tm, TOP_K), row),
                  pl.BlockSpec((tm, D_MODEL), row),
                  pl.BlockSpec((tm, D_MODEL), row),
                  _resident((D_MODEL, D_EXPERT), lambda i: (0, 0)),
                  _resident((D_MODEL, D_EXPERT), lambda i: (0, 0)),
                  _resident((D_EXPERT, D_MODEL), lambda i: (0, 0)),
                  _resident((1, D_MODEL), lambda i: (0, 0)),
                  _resident((1, D_MODEL), lambda i: (0, 0))] + [slot_spec(k) for k in range(TOP_K)],
        out_specs=[pl.BlockSpec((tm, D_MODEL), row), pl.BlockSpec((tm, D_MODEL), row)],
        out_shape=[jax.ShapeDtypeStruct((t, D_MODEL), F32),
                   jax.ShapeDtypeStruct((t, D_MODEL), BF16)],
        scratch_shapes=[pltpu.VMEM((tm, D_MODEL), F32)],
        compiler_params=_cparams("parallel"),
        name="moe_combine_ln",
    )(gate_t, x, xb, p["sh_w1"], p["sh_w3"], p["sh_w2"], p["ln2_g"], p["ln2_b"], *([yg] * TOP_K))


def _n_expert_blocks(t):
    return -(-(t * TOP_K + N_EXPERTS * (EXPERT_BM - 1)) // EXPERT_BM)


def _route(xb, p):
    t = xb.shape[0]
    n_blocks = _n_expert_blocks(t)
    idx, gate, rank, cnt = _router(xb, p["router_w_t"], p["router_bias"])
    counts = cnt[:, 0]
    padded = (counts + EXPERT_BM - 1) // EXPERT_BM * EXPERT_BM
    pad_end = jnp.cumsum(padded)
    pad_start = pad_end - padded
    n_used = (pad_end[-1:] // EXPERT_BM).astype(I32)
    block = jnp.arange(n_blocks, dtype=I32)
    block_exp = jnp.minimum(jnp.sum(pad_end[None, :] <= block[:, None] * EXPERT_BM, axis=1),
                            N_EXPERTS - 1).astype(I32)
    rows_before = (block - pad_start[block_exp] // EXPERT_BM) * EXPERT_BM
    block_live = jnp.clip(counts[block_exp] - rows_before, 0, EXPERT_BM).astype(I32)
    dest = _dest_rows(pad_start.astype(I32), idx, rank)
    return dict(dest=dest, gate_t=gate.T, block_exp=block_exp, block_live=block_live, n_used=n_used,
                n_rows=n_blocks * EXPERT_BM)


def _layer_params(l, b_gate, sinks, pool_w, pool_scale, conv_w, conv_b, lru_wa, lru_ba, lru_wx,
                  lru_bx, lru_lambda, w_branch, w_out, ln1_g, ln1_b, router_w, router_bias,
                  sh_w1, sh_w3, sh_w2, ln2_g, ln2_b):
    row = lambda a: a[l].reshape(1, -1).astype(F32)
    return dict(
        b_gate=row(b_gate), sinks=sinks[l].astype(F32),
        pool_w=pool_w[l].astype(BF16), pool_scale=row(pool_scale),
        conv_w=conv_w[l].astype(F32), conv_b=row(conv_b),
        lru_wa=_block_diag(lru_wa[l]).astype(BF16), lru_ba=row(lru_ba),
        lru_wx=_block_diag(lru_wx[l]).astype(BF16), lru_bx=row(lru_bx),
        lru_lambda=row(lru_lambda),
        w_branch=w_branch[l].astype(BF16), w_out=w_out[l].astype(BF16),
        ln1_g=row(ln1_g), ln1_b=row(ln1_b),
        router_w_t=router_w[l].T.astype(BF16), router_bias=router_bias[l].reshape(-1, 1).astype(F32),
        sh_w1=sh_w1[l].astype(BF16), sh_w3=sh_w3[l].astype(BF16), sh_w2=sh_w2[l].astype(BF16),
        ln2_g=row(ln2_g), ln2_b=row(ln2_b),
    )


def kernel(x, positions, w_in, b_gate, sinks, pool_w, pool_scale, conv_w, conv_b, lru_wa, lru_ba, lru_wx, lru_bx, lru_lambda, w_branch, w_out, ln1_g, ln1_b, router_w, router_bias, exp_w1, exp_w3, exp_w2, sh_w1, sh_w3, sh_w2, ln2_g, ln2_b):
    batch, seq_len, d = x.shape
    assert d == D_MODEL and seq_len % max(ATTN_TQ, MIX_TB, PROJ_TM) == 0
    t = batch * seq_len
    cos_all, sin_all = _rope_tables(positions)
    x_all = x.reshape(t, d).astype(F32)
    seqs = []
    for s in range(batch):
        rows = slice(s * seq_len, (s + 1) * seq_len)
        seqs.append(dict(xf=x_all, row0=s * seq_len, xb=x_all[rows].astype(BF16),
                         cos=cos_all[rows], sin=sin_all[rows]))
    for l in range(DEPTH):
        p = _layer_params(l, b_gate, sinks, pool_w, pool_scale, conv_w, conv_b, lru_wa, lru_ba,
                          lru_wx, lru_bx, lru_lambda, w_branch, w_out, ln1_g, ln1_b, router_w,
                          router_bias, sh_w1, sh_w3, sh_w2, ln2_g, ln2_b)
        for sq in seqs:
            proj = _in_proj(sq["xb"], w_in, l)
            ya = _attention(proj, sq["cos"], sq["sin"], p["sinks"], seq_len)
            yp, yl = _mixers(proj, p, 1, seq_len)
            sq["xf"], sq["xb"], xp = _merge(proj, ya, yp, yl, sq["xf"], sq["row0"], p)
            sq["row0"] = 0
            sq["route"] = _route(sq["xb"], p)
            sq["xs"] = _sc_dispatch(xp, sq["route"]["dest"], sq["route"]["n_rows"])
        for sq in seqs:
            r = sq["route"]
            ys = _experts(r["block_exp"], r["block_live"], r["n_used"], sq["xs"], exp_w1, exp_w3,
                          exp_w2, l)
            sq["yg"] = _sc_gather(ys, r["dest"])
        for sq in seqs:
            sq["xf"], sq["xb"] = _combine(sq["route"]["gate_t"], sq["xf"], sq["xb"], sq["yg"], p)
    out = jnp.stack([sq["xf"] for sq in seqs], axis=0)
    return out.reshape(batch, seq_len, d).astype(x.dtype)
```

```python
import functools

import jax
import jax.numpy as jnp
from jax import lax
from jax.experimental import pallas as pl
from jax.experimental.pallas import tpu as pltpu
from jax.experimental.pallas import tpu_sc as plsc

F32 = jnp.float32
BF16 = jnp.bfloat16
I32 = jnp.int32
U32 = jnp.uint32

D_MODEL = 2048
DEPTH = 2
CHUNK = 64
HEAD_DIM = 64
HALF_HEAD = HEAD_DIM // 2
N_HEADS = 16
N_KV_HEADS = 4
KV_GROUP = N_HEADS // N_KV_HEADS
ATTN_WIDTH = N_HEADS * HEAD_DIM
KV_WIDTH = N_KV_HEADS * HEAD_DIM
WINDOW_CHUNKS = 2
ROPE_THETA = 10000.0
POOL_WINDOWS = (2, 4, 8, 16)
POOL_WIDTH = 1024
POOL_GROUP = POOL_WIDTH // len(POOL_WINDOWS)
LRU_WIDTH = 1024
LRU_BLOCKS = 16
LRU_BLOCK = LRU_WIDTH // LRU_BLOCKS
CONV_WIDTH = 4
LRU_C = 8.0
N_BRANCH = 3
GATE_WIDTH = N_BRANCH * D_MODEL
IN_WIDTH = ATTN_WIDTH + 2 * KV_WIDTH + POOL_WIDTH + 2 * LRU_WIDTH + GATE_WIDTH
N_EXPERTS = 64
TOP_K = 8
D_EXPERT = 512
ROUTED_SCALE = 2.5
LN_EPS = 1e-5
DEEPNORM_ALPHA = (2 * DEPTH) ** 0.25

COL_GATE = 0
COL_Q = GATE_WIDTH
COL_POOL = COL_Q + ATTN_WIDTH
COL_LRU = COL_POOL + POOL_WIDTH
COL_GELU = COL_LRU + LRU_WIDTH
COL_K = COL_GELU + LRU_WIDTH
COL_V = COL_K + KV_WIDTH

LANES = 128
SUBLANES = 8
MXU_DIM = 256
VMEM_LIMIT_BYTES = 56 * 1024 * 1024

PROJ_TM = 512
PROJ_TN = 1536
ATTN_TQ = 256
ATTN_HALO = WINDOW_CHUNKS * CHUNK
MIX_TB = 256
POOL_HALO = 16
CONV_HALO = 8
MERGE_TM = 256
ROUTER_TM = 512
DEST_TM = 2048
EXPERT_BM = 512
COMBINE_TM = 256
PROJ_SRC = 512
NEG_BIG = -1e30

SUB_WORDS = 256
SUB_COLS = 2 * SUB_WORDS
SUB_ROWS = D_MODEL // SUB_COLS
SC_WINDOW = 128


def _cparams(*sem):
    return pltpu.CompilerParams(dimension_semantics=sem, vmem_limit_bytes=VMEM_LIMIT_BYTES)


def _resident(shape, index_map):
    return pl.BlockSpec(shape, index_map, pipeline_mode=pl.Buffered(1))


def _pack_rows(lo, hi):
    lo_u = lax.bitcast_convert_type(lo.astype(BF16).astype(F32), U32)
    hi_u = lax.bitcast_convert_type(hi.astype(BF16).astype(F32), U32)
    return (lo_u >> 16) | (hi_u & jnp.uint32(0xFFFF0000))


def _unpack_rows(w):
    lo = lax.bitcast_convert_type(w << 16, F32)
    hi = lax.bitcast_convert_type(w & jnp.uint32(0xFFFF0000), F32)
    return lo, hi


def _piece_cols(j):
    return (slice(j * SUB_COLS, j * SUB_COLS + SUB_WORDS),
            slice(j * SUB_COLS + SUB_WORDS, (j + 1) * SUB_COLS))


def _store_pieces(ref, y):
    for j in range(SUB_ROWS):
        lo_cols, hi_cols = _piece_cols(j)
        ref[j] = _pack_rows(y[:, lo_cols], y[:, hi_cols])


def _layer_norm(y, g, b):
    mu = jnp.mean(y, axis=-1, keepdims=True)
    d = y - mu
    var = jnp.mean(d * d, axis=-1, keepdims=True)
    return d * lax.rsqrt(var + LN_EPS) * g + b


def _rope_kernel(pos_ref, inv_ref, cos_ref, sin_ref):
    ang = pos_ref[...].astype(F32) * inv_ref[...]
    lane = lax.broadcasted_iota(I32, ang.shape, 1)
    first_half = (lane % HEAD_DIM) < HALF_HEAD
    cos_ref[...] = jnp.cos(ang)
    s = jnp.sin(ang)
    sin_ref[...] = jnp.where(first_half, -s, s)


def _rope_tables(positions):
    t = positions.size
    tm = min(1024, t)
    inv_freq = 1.0 / (ROPE_THETA ** (jnp.arange(0, HEAD_DIM, 2, dtype=F32) / HEAD_DIM))
    inv = jnp.tile(inv_freq, LANES // HALF_HEAD).reshape(1, LANES)
    return pl.pallas_call(
        _rope_kernel,
        grid=(t // tm,),
        in_specs=[pl.BlockSpec((tm, 1), lambda i: (i, 0)),
                  pl.BlockSpec((1, LANES), lambda i: (0, 0))],
        out_specs=[pl.BlockSpec((tm, LANES), lambda i: (i, 0))] * 2,
        out_shape=[jax.ShapeDtypeStruct((t, LANES), F32)] * 2,
        compiler_params=_cparams("parallel"),
        name="rope_tables",
    )(positions.reshape(t, 1), inv)


def _proj_source_blocks():
    ref_order = (("q", ATTN_WIDTH), ("k", KV_WIDTH), ("v", KV_WIDTH), ("pool", POOL_WIDTH),
                 ("lru", LRU_WIDTH), ("gelu", LRU_WIDTH), ("gate", GATE_WIDTH))
    start, col = {}, 0
    for name, width in ref_order:
        start[name] = col
        col += width
    assert start["v"] == start["k"] + KV_WIDTH and 2 * KV_WIDTH == PROJ_SRC
    out_order = (("gate", GATE_WIDTH), ("q", ATTN_WIDTH), ("pool", POOL_WIDTH), ("lru", LRU_WIDTH),
                 ("gelu", LRU_WIDTH), ("k", 2 * KV_WIDTH))
    blocks = []
    for name, width in out_order:
        assert start[name] % PROJ_SRC == 0 and width % PROJ_SRC == 0
        blocks += [start[name] // PROJ_SRC + c for c in range(width // PROJ_SRC)]
    return blocks


def _in_proj_kernel(src_ref, x_ref, *refs):
    w_refs, o_ref, wb = refs[:-2], refs[-2], refs[-1]

    @pl.when(pl.program_id(1) == 0)
    def _():
        for c, w_ref in enumerate(w_refs):
            wb[:, c * PROJ_SRC:(c + 1) * PROJ_SRC] = w_ref[0].astype(BF16)

    o_ref[...] = jnp.dot(x_ref[...], wb[...], preferred_element_type=F32).astype(o_ref.dtype)


def _in_proj(xb, w_in, layer):
    t, d = xb.shape
    n = w_in.shape[2]
    tm = min(PROJ_TM, t)
    per = PROJ_TN // PROJ_SRC
    src = jnp.asarray(_proj_source_blocks(), I32)

    def w_spec(c):
        return pl.BlockSpec((1, d, PROJ_SRC), lambda j, i, s: (layer, 0, s[per * j + c]))

    grid_spec = pltpu.PrefetchScalarGridSpec(
        num_scalar_prefetch=1,
        grid=(n // PROJ_TN, t // tm),
        in_specs=[pl.BlockSpec((tm, d), lambda j, i, s: (i, 0))] + [w_spec(c) for c in range(per)],
        out_specs=pl.BlockSpec((tm, PROJ_TN), lambda j, i, s: (i, j)),
        scratch_shapes=[pltpu.VMEM((d, PROJ_TN), BF16)],
    )
    return pl.pallas_call(
        _in_proj_kernel,
        grid_spec=grid_spec,
        out_shape=jax.ShapeDtypeStruct((t, n), BF16),
        compiler_params=_cparams("arbitrary", "arbitrary"),
        name="in_proj",
    )(src, xb, *([w_in] * per))


def _rope(x, cos, sin):
    w = x.shape[1]
    reps = w // LANES
    c = jnp.concatenate([cos] * reps, axis=1)
    s = jnp.concatenate([sin] * reps, axis=1)
    lane = lax.broadcasted_iota(I32, x.shape, 1)
    first_half = (lane % HEAD_DIM) < HALF_HEAD
    rot = jnp.where(first_half, pltpu.roll(x, w - HALF_HEAD, 1), pltpu.roll(x, HALF_HEAD, 1))
    return x * c + rot * s


def _attn_kernel(q_ref, k_ref, v_ref, kh_ref, vh_ref, cq_ref, sq_ref, ch_ref, sh_ref, sink_ref,
                 o_ref, s_buf, p_buf, r_buf, *, steps_per_seq):
    first = (pl.program_id(0) % steps_per_seq) == 0
    q = _rope(q_ref[...].astype(F32), cq_ref[...], sq_ref[...]) * (HEAD_DIM ** -0.5)
    q = q.astype(BF16)
    k_own = _rope(k_ref[...].astype(F32), cq_ref[...], sq_ref[...])
    k_halo = _rope(kh_ref[...].astype(F32), ch_ref[...], sh_ref[...])
    k_all = jnp.concatenate([k_halo, k_own], axis=0).astype(BF16)
    v_all = jnp.concatenate([vh_ref[...], v_ref[...]], axis=0)
    span = (WINDOW_CHUNKS + 1) * CHUNK
    rows = KV_GROUP * CHUNK
    chunk_rows = N_KV_HEADS * rows
    n_chunks = q.shape[0] // CHUNK
    win_chunk = lax.broadcasted_iota(I32, (rows, span), 1) // CHUNK
    oldest = jnp.where(first, 0, -WINDOW_CHUNKS)

    def tile_rows(c, g):
        return slice(c * chunk_rows + g * rows, c * chunk_rows + (g + 1) * rows)

    def kv_cols(g):
        return slice(g * HEAD_DIM, (g + 1) * HEAD_DIM)

    for c in range(n_chunks):
        q_c = q[c * CHUNK:(c + 1) * CHUNK, :]
        k_win = k_all[c * CHUNK:c * CHUNK + span, :]
        for g in range(N_KV_HEADS):
            qs = jnp.concatenate(
                [q_c[:, (g * KV_GROUP + i) * HEAD_DIM:(g * KV_GROUP + i + 1) * HEAD_DIM]
                 for i in range(KV_GROUP)], axis=0)
            s = lax.dot_general(qs, k_win[:, kv_cols(g)], (((1,), (1,)), ((), ())),
                                preferred_element_type=F32)
            if c < WINDOW_CHUNKS:
                s = jnp.where(win_chunk + (c - WINDOW_CHUNKS) >= oldest, s, NEG_BIG)
            s_buf[tile_rows(c, g), :] = s
    sink = sink_ref[...]
    for c in range(n_chunks):
        blk = slice(c * chunk_rows, (c + 1) * chunk_rows)
        s = s_buf[blk, :]
        m = jnp.maximum(jnp.max(s, axis=1, keepdims=True), sink)
        p = jnp.exp(s - m)
        p_buf[blk, :] = p.astype(BF16)
        r_buf[blk, :] = 1.0 / (jnp.sum(p, axis=1, keepdims=True) + jnp.exp(sink - m))
    for c in range(n_chunks):
        v_win = v_all[c * CHUNK:c * CHUNK + span, :]
        heads = []
        for g in range(N_KV_HEADS):
            o = jnp.dot(p_buf[tile_rows(c, g), :], v_win[:, kv_cols(g)], preferred_element_type=F32)
            o = o * r_buf[tile_rows(c, g), :]
            heads += [o[i * CHUNK:(i + 1) * CHUNK, :] for i in range(KV_GROUP)]
        o_ref[c * CHUNK:(c + 1) * CHUNK, :] = jnp.concatenate(heads, axis=1).astype(o_ref.dtype)


def _attention(proj, cos, sin, sinks, seq_len):
    t = proj.shape[0]
    tq = ATTN_TQ
    steps_per_seq = seq_len // tq
    halo_per_tq = tq // ATTN_HALO
    q_blk = COL_Q // ATTN_WIDTH
    k_blk = COL_K // KV_WIDTH
    v_blk = COL_V // KV_WIDTH

    def halo_row(i):
        return jnp.maximum(i * halo_per_tq - 1, 0)

    span = (WINDOW_CHUNKS + 1) * CHUNK
    tile_rows = (tq // CHUNK) * N_HEADS * CHUNK
    sink_rows = jnp.repeat(sinks, CHUNK).reshape(N_HEADS * CHUNK, 1)
    return pl.pallas_call(
        functools.partial(_attn_kernel, steps_per_seq=steps_per_seq),
        grid=(t // tq,),
        in_specs=[
            pl.BlockSpec((tq, ATTN_WIDTH), lambda i: (i, q_blk)),
            pl.BlockSpec((tq, KV_WIDTH), lambda i: (i, k_blk)),
            pl.BlockSpec((tq, KV_WIDTH), lambda i: (i, v_blk)),
            pl.BlockSpec((ATTN_HALO, KV_WIDTH), lambda i: (halo_row(i), k_blk)),
            pl.BlockSpec((ATTN_HALO, KV_WIDTH), lambda i: (halo_row(i), v_blk)),
            pl.BlockSpec((tq, LANES), lambda i: (i, 0)),
            pl.BlockSpec((tq, LANES), lambda i: (i, 0)),
            pl.BlockSpec((ATTN_HALO, LANES), lambda i: (halo_row(i), 0)),
            pl.BlockSpec((ATTN_HALO, LANES), lambda i: (halo_row(i), 0)),
            pl.BlockSpec((N_HEADS * CHUNK, 1), lambda i: (0, 0)),
        ],
        out_specs=pl.BlockSpec((tq, ATTN_WIDTH), lambda i: (i, 0)),
        out_shape=jax.ShapeDtypeStruct((t, ATTN_WIDTH), BF16),
        scratch_shapes=[pltpu.VMEM((tile_rows, span), F32), pltpu.VMEM((tile_rows, span), BF16),
                        pltpu.VMEM((tile_rows, 1), F32)],
        compiler_params=_cparams("parallel"),
        name="swa_attention",
    )(proj, proj, proj, proj, proj, cos, sin, cos, sin, sink_rows)


def _mix_kernel(up_ref, ul_ref, ug_ref, pw_ref, ps_ref, cw_ref, cb_ref, wa_ref, ba_ref, wx_ref,
                bx_ref, lam_ref, yp_ref, yl_ref, halo_p, halo_l, h_carry):
    s = pl.program_id(1)
    tb = up_ref.shape[0]

    @pl.when(s == 0)
    def _():
        halo_p[...] = jnp.zeros_like(halo_p)
        halo_l[...] = jnp.zeros_like(halo_l)
        h_carry[...] = jnp.zeros_like(h_carry)

    u = up_ref[...].astype(F32)
    ext = jnp.concatenate([halo_p[...], u], axis=0)
    s2 = ext + pltpu.roll(ext, 1, 0)
    s4 = s2 + pltpu.roll(s2, 2, 0)
    s8 = s4 + pltpu.roll(s4, 4, 0)
    s16 = s8 + pltpu.roll(s8, 8, 0)
    t_idx = (s * tb + lax.broadcasted_iota(I32, (tb, 1), 0)).astype(F32)
    for g, (win, ws) in enumerate(zip(POOL_WINDOWS, (s2, s4, s8, s16))):
        cols = slice(g * POOL_GROUP, (g + 1) * POOL_GROUP)
        count = jnp.minimum(t_idx + 1.0, float(win))
        pooled = ws[POOL_HALO:, cols] / count - u[:, cols]
        y = jnp.dot(pooled.astype(BF16), pw_ref[g], preferred_element_type=F32)
        yp_ref[:, cols] = (y * ps_ref[:, cols]).astype(yp_ref.dtype)
    halo_p[...] = u[tb - POOL_HALO:, :]

    ul = ul_ref[...].astype(F32)
    extl = jnp.concatenate([halo_l[...], ul], axis=0)
    xc = cb_ref[...] + cw_ref[CONV_WIDTH - 1:CONV_WIDTH, :] * ul
    for j in range(1, CONV_WIDTH):
        w_j = cw_ref[CONV_WIDTH - 1 - j:CONV_WIDTH - j, :]
        xc = xc + w_j * pltpu.roll(extl, j, 0)[CONV_HALO:, :]
    halo_l[...] = ul[tb - CONV_HALO:, :]
    xcb = xc.astype(BF16)
    r_parts, i_parts = [], []
    for j in range(LRU_WIDTH // MXU_DIM):
        cols = slice(j * MXU_DIM, (j + 1) * MXU_DIM)
        r_parts.append(jnp.dot(xcb[:, cols], wa_ref[j], preferred_element_type=F32))
        i_parts.append(jnp.dot(xcb[:, cols], wx_ref[j], preferred_element_type=F32))
    r = jax.nn.sigmoid(jnp.concatenate(r_parts, axis=1) + ba_ref[...])
    ig = jax.nn.sigmoid(jnp.concatenate(i_parts, axis=1) + bx_ref[...])
    z = -lam_ref[...]
    softplus = jnp.maximum(z, 0.0) + jnp.log1p(jnp.exp(-jnp.abs(z)))
    log_a = (-LRU_C) * r * softplus
    a = jnp.exp(log_a)
    b = jnp.sqrt(1.0 - jnp.exp(2.0 * log_a)) * (ig * xc)
    row_in_group = lax.broadcasted_iota(I32, a.shape, 0) % SUBLANES
    d = 1
    while d < SUBLANES:
        keep = row_in_group >= d
        a_prev = jnp.where(keep, pltpu.roll(a, d, 0), 1.0)
        b_prev = jnp.where(keep, pltpu.roll(b, d, 0), 0.0)
        b = a * b_prev + b
        a = a * a_prev
        d *= 2
    gelu = jax.nn.gelu(ug_ref[...].astype(F32))
    h_prev = h_carry[...]
    pair = 2 * SUBLANES
    for g in range(tb // pair):
        hs = []
        for half in range(2):
            rows = slice(g * pair + half * SUBLANES, g * pair + (half + 1) * SUBLANES)
            h = a[rows, :] * h_prev + b[rows, :]
            h_prev = h[SUBLANES - 1:SUBLANES, :]
            hs.append(h)
        rows = slice(g * pair, (g + 1) * pair)
        yl_ref[rows, :] = (jnp.concatenate(hs, axis=0) * gelu[rows, :]).astype(yl_ref.dtype)
    h_carry[...] = h_prev


def _block_diag(w):
    per = MXU_DIM // LRU_BLOCK
    w = w.reshape(LRU_BLOCKS // per, per, LRU_BLOCK, LRU_BLOCK)
    eye = jnp.eye(per, dtype=w.dtype)
    out = jnp.einsum("gpij,pq->gpiqj", w, eye)
    return out.reshape(LRU_BLOCKS // per, MXU_DIM, MXU_DIM)


def _mixers(proj, p, batch, seq_len):
    t = proj.shape[0]
    tb = min(MIX_TB, seq_len)
    steps = seq_len // tb

    def rows(bi, si):
        return bi * steps + si

    def col_spec(col):
        blk = col // POOL_WIDTH
        return pl.BlockSpec((tb, POOL_WIDTH), lambda bi, si: (rows(bi, si), blk))

    def const(shape):
        nd = len(shape)
        return pl.BlockSpec(shape, lambda bi, si: (0,) * nd)

    out_spec = pl.BlockSpec((tb, POOL_WIDTH), lambda bi, si: (rows(bi, si), 0))
    return pl.pallas_call(
        _mix_kernel,
        grid=(batch, steps),
        in_specs=[col_spec(COL_POOL), col_spec(COL_LRU), col_spec(COL_GELU),
                  const((len(POOL_WINDOWS), POOL_GROUP, POOL_GROUP)), const((1, POOL_WIDTH)),
                  const((CONV_WIDTH, LRU_WIDTH)), const((1, LRU_WIDTH)),
                  const((LRU_WIDTH // MXU_DIM, MXU_DIM, MXU_DIM)), const((1, LRU_WIDTH)),
                  const((LRU_WIDTH // MXU_DIM, MXU_DIM, MXU_DIM)), const((1, LRU_WIDTH)),
                  const((1, LRU_WIDTH))],
        out_specs=[out_spec, out_spec],
        out_shape=[jax.ShapeDtypeStruct((t, POOL_WIDTH), BF16),
                   jax.ShapeDtypeStruct((t, LRU_WIDTH), BF16)],
        scratch_shapes=[pltpu.VMEM((POOL_HALO, POOL_WIDTH), F32),
                        pltpu.VMEM((CONV_HALO, LRU_WIDTH), F32),
                        pltpu.VMEM((1, LRU_WIDTH), F32)],
        compiler_params=_cparams("arbitrary", "arbitrary"),
        name="pool_rglru",
    )(proj, proj, proj, p["pool_w"], p["pool_scale"], p["conv_w"], p["conv_b"],
      p["lru_wa"], p["lru_ba"], p["lru_wx"], p["lru_bx"], p["lru_lambda"])


def _merge_kernel(gl_ref, ya_ref, yp_ref, yl_ref, x_ref, bg_ref, wb_ref, wo_ref, g_ref, b_ref,
                  xo_ref, xb_ref, xp_ref):
    merged = None
    for j, y_ref in enumerate((ya_ref, yp_ref, yl_ref)):
        cols = slice(j * D_MODEL, (j + 1) * D_MODEL)
        z = jnp.dot(y_ref[...], wb_ref[j], preferred_element_type=F32)
        gate = jax.nn.sigmoid(gl_ref[:, cols].astype(F32) + bg_ref[:, cols])
        merged = gate * z if merged is None else merged + gate * z
    h = jnp.dot(merged.astype(BF16), wo_ref[...], preferred_element_type=F32)
    y = _layer_norm(DEEPNORM_ALPHA * x_ref[...] + h, g_ref[...], b_ref[...])
    xo_ref[...] = y
    xb_ref[...] = y.astype(BF16)
    _store_pieces(xp_ref, y)


def _merge(proj, ya, yp, yl, x, x_row0, p):
    t = proj.shape[0]
    tm = min(MERGE_TM, t)
    assert x_row0 % tm == 0
    row = lambda i: (i, 0)
    return pl.pallas_call(
        _merge_kernel,
        grid=(t // tm,),
        in_specs=[pl.BlockSpec((tm, GATE_WIDTH), row),
                  pl.BlockSpec((tm, ATTN_WIDTH), row),
                  pl.BlockSpec((tm, POOL_WIDTH), row),
                  pl.BlockSpec((tm, LRU_WIDTH), row),
                  pl.BlockSpec((tm, D_MODEL), lambda i: (i + x_row0 // tm, 0)),
                  _resident((1, GATE_WIDTH), lambda i: (0, 0)),
                  _resident((N_BRANCH, ATTN_WIDTH, D_MODEL), lambda i: (0, 0, 0)),
                  _resident((D_MODEL, D_MODEL), lambda i: (0, 0)),
                  _resident((1, D_MODEL), lambda i: (0, 0)),
                  _resident((1, D_MODEL), lambda i: (0, 0))],
        out_specs=[pl.BlockSpec((tm, D_MODEL), row), pl.BlockSpec((tm, D_MODEL), row),
                   pl.BlockSpec((SUB_ROWS, tm, SUB_WORDS), lambda i: (0, i, 0))],
        out_shape=[jax.ShapeDtypeStruct((t, D_MODEL), F32),
                   jax.ShapeDtypeStruct((t, D_MODEL), BF16),
                   jax.ShapeDtypeStruct((SUB_ROWS, t, SUB_WORDS), U32)],
        compiler_params=_cparams("parallel"),
        name="merge_ln",
    )(proj, ya, yp, yl, x, p["b_gate"], p["w_branch"], p["w_out"], p["ln1_g"], p["ln1_b"])


def _router_kernel(x_ref, rw_ref, rb_ref, idx_ref, gate_ref, rank_ref, cnt_ref, carry):
    @pl.when(pl.program_id(0) == 0)
    def _():
        carry[...] = jnp.zeros_like(carry)

    tm = x_ref.shape[0]
    logits = lax.dot_general(rw_ref[...], x_ref[...], (((1,), (1,)), ((), ())),
                             preferred_element_type=F32)
    scores = jax.nn.sigmoid(logits)
    sel = scores + rb_ref[...]
    expert = lax.broadcasted_iota(I32, scores.shape, 0)
    idxs, tops = [], []
    chosen = jnp.zeros(scores.shape, F32)
    for _ in range(TOP_K):
        best = jnp.max(sel, axis=0, keepdims=True)
        ik = jnp.min(jnp.where(sel == best, expert, N_EXPERTS), axis=0, keepdims=True)
        hit = expert == ik
        tops.append(jnp.sum(jnp.where(hit, scores, 0.0), axis=0, keepdims=True))
        idxs.append(ik)
        chosen = jnp.where(hit, 1.0, chosen)
        sel = jnp.where(hit, -jnp.inf, sel)
    top = jnp.concatenate(tops, axis=0)
    idx = jnp.concatenate(idxs, axis=0)
    gate_ref[...] = top / jnp.sum(top, axis=0, keepdims=True) * ROUTED_SCALE
    idx_ref[...] = idx
    earlier = (lax.broadcasted_iota(I32, (tm, tm), 0) < lax.broadcasted_iota(I32, (tm, tm), 1))
    before = jnp.dot(chosen.astype(BF16), jnp.where(earlier, 1.0, 0.0).astype(BF16),
                     preferred_element_type=F32)
    base = before + carry[:, 0:1]
    ranks = [jnp.sum(jnp.where(expert == idxs[k], base, 0.0), axis=0, keepdims=True)
             for k in range(TOP_K)]
    rank_ref[...] = jnp.concatenate(ranks, axis=0).astype(I32)
    carry[...] = carry[...] + jnp.sum(chosen, axis=1, keepdims=True)
    cnt_ref[...] = carry[...].astype(I32)


def _router(x, rw_t, rbias):
    t = x.shape[0]
    tm = min(ROUTER_TM, t)
    tok = lambda i: (0, i)
    return pl.pallas_call(
        _router_kernel,
        grid=(t // tm,),
        in_specs=[pl.BlockSpec((tm, D_MODEL), lambda i: (i, 0)),
                  pl.BlockSpec((N_EXPERTS, D_MODEL), lambda i: (0, 0)),
                  pl.BlockSpec((N_EXPERTS, 1), lambda i: (0, 0))],
        out_specs=[pl.BlockSpec((TOP_K, tm), tok), pl.BlockSpec((TOP_K, tm), tok),
                   pl.BlockSpec((TOP_K, tm), tok),
                   pl.BlockSpec((N_EXPERTS, LANES), lambda i: (0, 0))],
        out_shape=[jax.ShapeDtypeStruct((TOP_K, t), I32), jax.ShapeDtypeStruct((TOP_K, t), F32),
                   jax.ShapeDtypeStruct((TOP_K, t), I32),
                   jax.ShapeDtypeStruct((N_EXPERTS, LANES), I32)],
        scratch_shapes=[pltpu.VMEM((N_EXPERTS, LANES), F32)],
        compiler_params=_cparams("arbitrary"),
        name="router_topk",
    )(x, rw_t, rbias)


def _dest_kernel(start_ref, idx_ref, rank_ref, ids_ref, *, n_rows):
    idx = idx_ref[...]
    dest = rank_ref[...] + pl.program_id(0) * n_rows
    for e in range(N_EXPERTS):
        dest = dest + jnp.where(idx == e, start_ref[e], 0)
    ids_ref[...] = dest


def _piece_ids(pad_start, idx, rank, n_rows):
    t = idx.shape[1]
    tm = min(DEST_TM, t)
    spec = pl.BlockSpec((TOP_K, tm), lambda j, i, s: (0, i))
    return pl.pallas_call(
        functools.partial(_dest_kernel, n_rows=n_rows),
        grid_spec=pltpu.PrefetchScalarGridSpec(
            num_scalar_prefetch=1, grid=(SUB_ROWS, t // tm), in_specs=[spec, spec],
            out_specs=pl.BlockSpec((TOP_K, tm), lambda j, i, s: (j, i))),
        out_shape=jax.ShapeDtypeStruct((SUB_ROWS * TOP_K, t), I32),
        compiler_params=_cparams("parallel", "parallel"),
        name="piece_ids",
    )(pad_start, idx, rank)


def _swiglu(xb, w1_ref, w3_ref, w2_ref):
    a = jnp.dot(xb, w1_ref[...], preferred_element_type=F32)
    g = jnp.dot(xb, w3_ref[...], preferred_element_type=F32)
    hid = (a * jax.nn.sigmoid(a) * g).astype(BF16)
    return jnp.dot(hid, w2_ref[...], preferred_element_type=F32)


def _sc_mesh():
    return plsc.VectorSubcoreMesh(core_axis_name="core", subcore_axis_name="subcore")


def _sc_dispatch(xp, ids, n_rows):
    t = xp.shape[1]
    per_plane = t // SC_WINDOW

    @pl.kernel(out_type=jax.ShapeDtypeStruct((SUB_ROWS * n_rows, SUB_WORDS), U32), mesh=_sc_mesh(),
               scratch_types=[])
    def scatter(x_hbm, i_hbm, o_hbm):
        def body(x_vmem, i_vmem):
            for k in range(TOP_K):
                pltpu.sync_copy(x_vmem, o_hbm.at[i_vmem.at[k]])

        pltpu.emit_pipeline(
            body,
            grid=(SUB_ROWS * per_plane,),
            in_specs=[pl.BlockSpec((SC_WINDOW, SUB_WORDS), lambda w: (w, 0)),
                      pl.BlockSpec((TOP_K, SC_WINDOW), lambda w: (w // per_plane, w % per_plane))],
            out_specs=[],
            core_axis_name=("core", "subcore"),
            dimension_semantics=(pltpu.PARALLEL,),
        )(x_hbm, i_hbm)

    out = scatter(xp.reshape(SUB_ROWS * t, SUB_WORDS), ids)
    return out.reshape(SUB_ROWS, n_rows, SUB_WORDS)


def _sc_gather(ys, ids):
    n_rows = ys.shape[1]
    t = ids.shape[1]
    per_row = t // SC_WINDOW
    pieces = SUB_ROWS * TOP_K * t

    @pl.kernel(out_type=jax.ShapeDtypeStruct((pieces, SUB_WORDS), U32), mesh=_sc_mesh(),
               scratch_types=[])
    def gather(y_hbm, i_hbm, o_hbm):
        def body(i_vmem, o_vmem):
            pltpu.sync_copy(y_hbm.at[i_vmem.at[0]], o_vmem)

        pltpu.emit_pipeline(
            body,
            grid=(pieces // SC_WINDOW,),
            in_specs=[pl.BlockSpec((1, SC_WINDOW), lambda w: (w // per_row, w % per_row))],
            out_specs=[pl.BlockSpec((SC_WINDOW, SUB_WORDS), lambda w: (w, 0))],
            core_axis_name=("core", "subcore"),
            dimension_semantics=(pltpu.PARALLEL,),
        )(i_hbm, o_hbm)

    out = gather(ys.reshape(SUB_ROWS * n_rows, SUB_WORDS), ids)
    return out.reshape(SUB_ROWS, TOP_K * t, SUB_WORDS)


def _expert_kernel(bexp_ref, blive_ref, nused_ref, bfirst_ref, bslot_ref, bnext_ref, xs_ref, w1_hbm,
                   w3_hbm, w2_hbm, ys_ref, st1, st3, st2, w1b, w3b, w2b, xb, sem, *, layer):
    b = pl.program_id(0)
    used = b < nused_ref[0]

    def weight_copies(e, slot):
        return (pltpu.make_async_copy(w1_hbm.at[layer, e], st1.at[slot], sem.at[slot, 0]),
                pltpu.make_async_copy(w3_hbm.at[layer, e], st3.at[slot], sem.at[slot, 1]),
                pltpu.make_async_copy(w2_hbm.at[layer, e], st2.at[slot], sem.at[slot, 2]))

    @pl.when(b == 0)
    def _():
        for cp in weight_copies(bexp_ref[0], 0):
            cp.start()

    @pl.when(jnp.logical_and(used, bfirst_ref[b] == 1))
    def _():
        slot = bslot_ref[b]
        for cp in weight_copies(bexp_ref[b], slot):
            cp.wait()

        @pl.when(bnext_ref[b] >= 0)
        def _():
            for cp in weight_copies(bnext_ref[b], 1 - slot):
                cp.start()

        w1b[...] = st1[slot].astype(BF16)
        w3b[...] = st3[slot].astype(BF16)
        w2b[...] = st2[slot].astype(BF16)

    @pl.when(used)
    def _():
        live = lax.broadcasted_iota(I32, (xb.shape[0], SUB_WORDS), 0) < blive_ref[b]
        for j in range(SUB_ROWS):
            lo, hi = _unpack_rows(jnp.where(live, xs_ref[j], jnp.uint32(0)))
            lo_cols, hi_cols = _piece_cols(j)
            xb[:, lo_cols] = lo.astype(BF16)
            xb[:, hi_cols] = hi.astype(BF16)
        _store_pieces(ys_ref, _swiglu(xb[...], w1b, w3b, w2b))

    @pl.when(jnp.logical_not(used))
    def _():
        ys_ref[...] = jnp.zeros_like(ys_ref)


def _experts(r, xs, w1, w3, w2, layer):
    n_blocks = xs.shape[1] // EXPERT_BM
    n_tables = 6

    def rows_spec(index):
        return pl.BlockSpec((SUB_ROWS, EXPERT_BM, SUB_WORDS), lambda b, *tables: (0, index(b, tables), 0))

    grid_spec = pltpu.PrefetchScalarGridSpec(
        num_scalar_prefetch=n_tables,
        grid=(n_blocks,),
        in_specs=[rows_spec(lambda b, tables: jnp.minimum(b, tables[2][0] - 1)),
                  pl.BlockSpec(memory_space=pl.ANY), pl.BlockSpec(memory_space=pl.ANY),
                  pl.BlockSpec(memory_space=pl.ANY)],
        out_specs=rows_spec(lambda b, tables: b),
        scratch_shapes=[pltpu.VMEM((2, D_MODEL, D_EXPERT), F32), pltpu.VMEM((2, D_MODEL, D_EXPERT), F32),
                        pltpu.VMEM((2, D_EXPERT, D_MODEL), F32),
                        pltpu.VMEM((D_MODEL, D_EXPERT), BF16), pltpu.VMEM((D_MODEL, D_EXPERT), BF16),
                        pltpu.VMEM((D_EXPERT, D_MODEL), BF16), pltpu.VMEM((EXPERT_BM, D_MODEL), BF16),
                        pltpu.SemaphoreType.DMA((2, 3))],
    )
    return pl.pallas_call(
        functools.partial(_expert_kernel, layer=layer),
        grid_spec=grid_spec,
        out_shape=jax.ShapeDtypeStruct(xs.shape, U32),
        compiler_params=_cparams("arbitrary"),
        name="moe_experts",
    )(r["block_exp"], r["block_live"], r["n_used"], r["block_first"], r["block_slot"], r["block_next"],
      xs, w1, w3, w2)


def _combine_kernel(gate_ref, x_ref, xb_ref, s1_ref, s3_ref, s2_ref, g_ref, b_ref, *refs):
    yk_refs = refs[:TOP_K]
    xo_ref, xbo_ref, hbuf = refs[-3:]
    tm = x_ref.shape[0]
    shared = _swiglu(xb_ref[...], s1_ref, s3_ref, s2_ref)
    gates = [jnp.broadcast_to(gate_ref[:, k:k + 1], (tm, SUB_WORDS)) for k in range(TOP_K)]
    for j in range(SUB_ROWS):
        lo_cols, hi_cols = _piece_cols(j)
        acc_lo = shared[:, lo_cols]
        acc_hi = shared[:, hi_cols]
        for k in range(TOP_K):
            lo, hi = _unpack_rows(yk_refs[k][j])
            acc_lo = acc_lo + gates[k] * lo
            acc_hi = acc_hi + gates[k] * hi
        hbuf[:, lo_cols] = acc_lo
        hbuf[:, hi_cols] = acc_hi
    y = _layer_norm(DEEPNORM_ALPHA * x_ref[...] + hbuf[...], g_ref[...], b_ref[...])
    xo_ref[...] = y
    xbo_ref[...] = y.astype(BF16)


def _combine(gate_t, x, xb, yg, p, out_rows=None, out_row0=0, out_buf=None):
    t = x.shape[0]
    tm = min(COMBINE_TM, t)
    steps = t // tm
    row = lambda i: (i, 0)
    assert out_row0 % tm == 0

    def slot_spec(k):
        return pl.BlockSpec((SUB_ROWS, tm, SUB_WORDS), lambda i: (0, k * steps + i, 0))

    in_specs = [pl.BlockSpec((tm, TOP_K), row),
                pl.BlockSpec((tm, D_MODEL), row),
                pl.BlockSpec((tm, D_MODEL), row),
                _resident((D_MODEL, D_EXPERT), lambda i: (0, 0)),
                _resident((D_MODEL, D_EXPERT), lambda i: (0, 0)),
                _resident((D_EXPERT, D_MODEL), lambda i: (0, 0)),
                _resident((1, D_MODEL), lambda i: (0, 0)),
                _resident((1, D_MODEL), lambda i: (0, 0))] + [slot_spec(k) for k in range(TOP_K)]
    args = [gate_t, x, xb, p["sh_w1"], p["sh_w3"], p["sh_w2"], p["ln2_g"], p["ln2_b"]] + [yg] * TOP_K
    aliases = {}
    if out_buf is not None:
        aliases = {len(args): 0}
        in_specs.append(pl.BlockSpec(memory_space=pl.ANY))
        args.append(out_buf)
    return pl.pallas_call(
        _combine_kernel,
        grid=(steps,),
        in_specs=in_specs,
        out_specs=[pl.BlockSpec((tm, D_MODEL), lambda i: (i + out_row0 // tm, 0)),
                   pl.BlockSpec((tm, D_MODEL), row)],
        out_shape=[jax.ShapeDtypeStruct((out_rows or t, D_MODEL), F32),
                   jax.ShapeDtypeStruct((t, D_MODEL), BF16)],
        scratch_shapes=[pltpu.VMEM((tm, D_MODEL), F32)],
        input_output_aliases=aliases,
        compiler_params=_cparams("parallel"),
        name="moe_combine_ln",
    )(*args)


def _n_expert_blocks(t):
    return -(-(t * TOP_K + N_EXPERTS * (EXPERT_BM - 1)) // EXPERT_BM)


def _route(xb, p):
    t = xb.shape[0]
    n_blocks = _n_expert_blocks(t)
    idx, gate, rank, cnt = _router(xb, p["router_w_t"], p["router_bias"])
    counts = cnt[:, 0]
    padded = (counts + EXPERT_BM - 1) // EXPERT_BM * EXPERT_BM
    pad_end = jnp.cumsum(padded)
    pad_start = pad_end - padded
    n_used = (pad_end[-1:] // EXPERT_BM).astype(I32)
    block = jnp.arange(n_blocks, dtype=I32)
    block_exp = jnp.minimum(jnp.sum(pad_end[None, :] <= block[:, None] * EXPERT_BM, axis=1),
                            N_EXPERTS - 1).astype(I32)
    rows_before = (block - pad_start[block_exp] // EXPERT_BM) * EXPERT_BM
    block_live = jnp.clip(counts[block_exp] - rows_before, 0, EXPERT_BM).astype(I32)
    used = block < n_used[0]
    prev_exp = jnp.concatenate([jnp.full((1,), -1, I32), block_exp[:-1]])
    first = used & (block_exp != prev_exp)
    block_slot = ((jnp.cumsum(first.astype(I32)) - 1) % 2).astype(I32)
    later_first = jnp.where(first, block, n_blocks)
    next_first = lax.cummin(jnp.concatenate([later_first[1:], jnp.full((1,), n_blocks, I32)]), reverse=True)
    block_next = jnp.where(next_first < n_blocks, block_exp[jnp.minimum(next_first, n_blocks - 1)], -1)
    n_rows = n_blocks * EXPERT_BM
    ids = _piece_ids(pad_start.astype(I32), idx, rank, n_rows)
    return dict(ids=ids, gate_t=gate.T, block_exp=block_exp, block_live=block_live, n_used=n_used,
                block_first=first.astype(I32), block_slot=block_slot, block_next=block_next.astype(I32),
                n_rows=n_rows)


def _layer_params(l, b_gate, sinks, pool_w, pool_scale, conv_w, conv_b, lru_wa, lru_ba, lru_wx,
                  lru_bx, lru_lambda, w_branch, w_out, ln1_g, ln1_b, router_w, router_bias,
                  sh_w1, sh_w3, sh_w2, ln2_g, ln2_b):
    row = lambda a: a[l].reshape(1, -1).astype(F32)
    return dict(
        b_gate=row(b_gate), sinks=sinks[l].astype(F32),
        pool_w=pool_w[l].astype(BF16), pool_scale=row(pool_scale),
        conv_w=conv_w[l].astype(F32), conv_b=row(conv_b),
        lru_wa=_block_diag(lru_wa[l]).astype(BF16), lru_ba=row(lru_ba),
        lru_wx=_block_diag(lru_wx[l]).astype(BF16), lru_bx=row(lru_bx),
        lru_lambda=row(lru_lambda),
        w_branch=w_branch[l].astype(BF16), w_out=w_out[l].astype(BF16),
        ln1_g=row(ln1_g), ln1_b=row(ln1_b),
        router_w_t=router_w[l].T.astype(BF16), router_bias=router_bias[l].reshape(-1, 1).astype(F32),
        sh_w1=sh_w1[l].astype(BF16), sh_w3=sh_w3[l].astype(BF16), sh_w2=sh_w2[l].astype(BF16),
        ln2_g=row(ln2_g), ln2_b=row(ln2_b),
    )


def kernel(x, positions, w_in, b_gate, sinks, pool_w, pool_scale, conv_w, conv_b, lru_wa, lru_ba, lru_wx, lru_bx, lru_lambda, w_branch, w_out, ln1_g, ln1_b, router_w, router_bias, exp_w1, exp_w3, exp_w2, sh_w1, sh_w3, sh_w2, ln2_g, ln2_b):
    batch, seq_len, d = x.shape
    assert d == D_MODEL and seq_len % max(ATTN_TQ, MIX_TB, PROJ_TM) == 0
    t = batch * seq_len
    cos_all, sin_all = _rope_tables(positions)
    x_all = x.reshape(t, d).astype(F32)
    seqs = []
    for s in range(batch):
        rows = slice(s * seq_len, (s + 1) * seq_len)
        seqs.append(dict(xf=x_all, row0=s * seq_len, xb=x_all[rows].astype(BF16),
                         cos=cos_all[rows], sin=sin_all[rows]))
    for l in range(DEPTH):
        p = _layer_params(l, b_gate, sinks, pool_w, pool_scale, conv_w, conv_b, lru_wa, lru_ba,
                          lru_wx, lru_bx, lru_lambda, w_branch, w_out, ln1_g, ln1_b, router_w,
                          router_bias, sh_w1, sh_w3, sh_w2, ln2_g, ln2_b)
        for sq in seqs:
            proj = _in_proj(sq["xb"], w_in, l)
            ya = _attention(proj, sq["cos"], sq["sin"], p["sinks"], seq_len)
            yp, yl = _mixers(proj, p, 1, seq_len)
            sq["xf"], sq["xb"], xp = _merge(proj, ya, yp, yl, sq["xf"], sq["row0"], p)
            sq["row0"] = 0
            sq["route"] = _route(sq["xb"], p)
            sq["xs"] = _sc_dispatch(xp, sq["route"]["ids"], sq["route"]["n_rows"])
        for sq in seqs:
            ys = _experts(sq["route"], sq["xs"], exp_w1, exp_w3, exp_w2, l)
            sq["yg"] = _sc_gather(ys, sq["route"]["ids"])
        out = None
        for s, sq in enumerate(seqs):
            if l + 1 < DEPTH:
                sq["xf"], sq["xb"] = _combine(sq["route"]["gate_t"], sq["xf"], sq["xb"], sq["yg"], p)
            else:
                out, _ = _combine(sq["route"]["gate_t"], sq["xf"], sq["xb"], sq["yg"], p,
                                  out_rows=t, out_row0=s * seq_len, out_buf=out)
    return out.reshape(batch, seq_len, d).astype(x.dtype)
```

```python
import functools

import jax
import jax.numpy as jnp
from jax import lax
from jax.experimental import pallas as pl
from jax.experimental.pallas import tpu as pltpu
from jax.experimental.pallas import tpu_sc as plsc

F32 = jnp.float32
BF16 = jnp.bfloat16
I32 = jnp.int32
U32 = jnp.uint32

D_MODEL = 2048
DEPTH = 2
CHUNK = 64
HEAD_DIM = 64
HALF_HEAD = HEAD_DIM // 2
N_HEADS = 16
N_KV_HEADS = 4
KV_GROUP = N_HEADS // N_KV_HEADS
ATTN_WIDTH = N_HEADS * HEAD_DIM
KV_WIDTH = N_KV_HEADS * HEAD_DIM
WINDOW_CHUNKS = 2
ROPE_THETA = 10000.0
POOL_WINDOWS = (2, 4, 8, 16)
POOL_WIDTH = 1024
POOL_GROUP = POOL_WIDTH // len(POOL_WINDOWS)
LRU_WIDTH = 1024
LRU_BLOCKS = 16
LRU_BLOCK = LRU_WIDTH // LRU_BLOCKS
CONV_WIDTH = 4
LRU_C = 8.0
N_BRANCH = 3
GATE_WIDTH = N_BRANCH * D_MODEL
IN_WIDTH = ATTN_WIDTH + 2 * KV_WIDTH + POOL_WIDTH + 2 * LRU_WIDTH + GATE_WIDTH
N_EXPERTS = 64
TOP_K = 8
D_EXPERT = 512
ROUTED_SCALE = 2.5
LN_EPS = 1e-5
DEEPNORM_ALPHA = (2 * DEPTH) ** 0.25

COL_GATE = 0
COL_Q = GATE_WIDTH
COL_POOL = COL_Q + ATTN_WIDTH
COL_LRU = COL_POOL + POOL_WIDTH
COL_GELU = COL_LRU + LRU_WIDTH
COL_K = COL_GELU + LRU_WIDTH
COL_V = COL_K + KV_WIDTH

LANES = 128
SUBLANES = 8
MXU_DIM = 256
VMEM_LIMIT_BYTES = 56 * 1024 * 1024

PROJ_TM = 1024
PROJ_TN = 1536
ATTN_TQ = 256
ATTN_HALO = WINDOW_CHUNKS * CHUNK
MIX_TB = 256
POOL_HALO = 16
CONV_HALO = 8
MERGE_TM = 256
ROUTER_TM = 512
DEST_TM = 2048
EXPERT_BM = 512
COMBINE_TM = 256
PROJ_SRC = 512
NEG_BIG = -1e30

SUB_WORDS = 256
SUB_COLS = 2 * SUB_WORDS
SUB_ROWS = D_MODEL // SUB_COLS
SC_WINDOW = 128


def _cparams(*sem):
    return pltpu.CompilerParams(dimension_semantics=sem, vmem_limit_bytes=VMEM_LIMIT_BYTES)


def _resident(shape, index_map):
    return pl.BlockSpec(shape, index_map, pipeline_mode=pl.Buffered(1))


def _pack_rows(lo, hi):
    lo_u = lax.bitcast_convert_type(lo.astype(BF16).astype(F32), U32)
    hi_u = lax.bitcast_convert_type(hi.astype(BF16).astype(F32), U32)
    return (lo_u >> 16) | (hi_u & jnp.uint32(0xFFFF0000))


def _unpack_rows(w):
    lo = lax.bitcast_convert_type(w << 16, F32)
    hi = lax.bitcast_convert_type(w & jnp.uint32(0xFFFF0000), F32)
    return lo, hi


def _piece_cols(j):
    return (slice(j * SUB_COLS, j * SUB_COLS + SUB_WORDS),
            slice(j * SUB_COLS + SUB_WORDS, (j + 1) * SUB_COLS))


def _store_pieces(ref, y):
    for j in range(SUB_ROWS):
        lo_cols, hi_cols = _piece_cols(j)
        ref[j] = _pack_rows(y[:, lo_cols], y[:, hi_cols])


def _layer_norm(y, g, b):
    mu = jnp.mean(y, axis=-1, keepdims=True)
    d = y - mu
    var = jnp.mean(d * d, axis=-1, keepdims=True)
    return d * lax.rsqrt(var + LN_EPS) * g + b


def _rope_kernel(pos_ref, inv_ref, cos_ref, sin_ref):
    ang = pos_ref[...].astype(F32) * inv_ref[...]
    lane = lax.broadcasted_iota(I32, ang.shape, 1)
    first_half = (lane % HEAD_DIM) < HALF_HEAD
    cos_ref[...] = jnp.cos(ang)
    s = jnp.sin(ang)
    sin_ref[...] = jnp.where(first_half, -s, s)


def _rope_tables(positions):
    t = positions.size
    tm = min(1024, t)
    inv_freq = 1.0 / (ROPE_THETA ** (jnp.arange(0, HEAD_DIM, 2, dtype=F32) / HEAD_DIM))
    inv = jnp.tile(inv_freq, LANES // HALF_HEAD).reshape(1, LANES)
    return pl.pallas_call(
        _rope_kernel,
        grid=(t // tm,),
        in_specs=[pl.BlockSpec((tm, 1), lambda i: (i, 0)),
                  pl.BlockSpec((1, LANES), lambda i: (0, 0))],
        out_specs=[pl.BlockSpec((tm, LANES), lambda i: (i, 0))] * 2,
        out_shape=[jax.ShapeDtypeStruct((t, LANES), F32)] * 2,
        compiler_params=_cparams("parallel"),
        name="rope_tables",
    )(positions.reshape(t, 1), inv)


def _proj_source_blocks():
    ref_order = (("q", ATTN_WIDTH), ("k", KV_WIDTH), ("v", KV_WIDTH), ("pool", POOL_WIDTH),
                 ("lru", LRU_WIDTH), ("gelu", LRU_WIDTH), ("gate", GATE_WIDTH))
    start, col = {}, 0
    for name, width in ref_order:
        start[name] = col
        col += width
    assert start["v"] == start["k"] + KV_WIDTH and 2 * KV_WIDTH == PROJ_SRC
    out_order = (("gate", GATE_WIDTH), ("q", ATTN_WIDTH), ("pool", POOL_WIDTH), ("lru", LRU_WIDTH),
                 ("gelu", LRU_WIDTH), ("k", 2 * KV_WIDTH))
    blocks = []
    for name, width in out_order:
        assert start[name] % PROJ_SRC == 0 and width % PROJ_SRC == 0
        blocks += [start[name] // PROJ_SRC + c for c in range(width // PROJ_SRC)]
    return blocks


def _in_proj_kernel(src_ref, x_ref, *refs):
    w_refs, o_ref, wb = refs[:-2], refs[-2], refs[-1]

    @pl.when(pl.program_id(1) == 0)
    def _():
        for c, w_ref in enumerate(w_refs):
            wb[:, c * PROJ_SRC:(c + 1) * PROJ_SRC] = w_ref[0].astype(BF16)

    o_ref[...] = jnp.dot(x_ref[...], wb[...], preferred_element_type=F32).astype(o_ref.dtype)


def _in_proj(xb, w_in, layer):
    t, d = xb.shape
    n = w_in.shape[2]
    tm = min(PROJ_TM, t)
    per = PROJ_TN // PROJ_SRC
    src = jnp.asarray(_proj_source_blocks(), I32)

    def w_spec(c):
        return pl.BlockSpec((1, d, PROJ_SRC), lambda j, i, s: (layer, 0, s[per * j + c]))

    grid_spec = pltpu.PrefetchScalarGridSpec(
        num_scalar_prefetch=1,
        grid=(n // PROJ_TN, t // tm),
        in_specs=[pl.BlockSpec((tm, d), lambda j, i, s: (i, 0))] + [w_spec(c) for c in range(per)],
        out_specs=pl.BlockSpec((tm, PROJ_TN), lambda j, i, s: (i, j)),
        scratch_shapes=[pltpu.VMEM((d, PROJ_TN), BF16)],
    )
    return pl.pallas_call(
        _in_proj_kernel,
        grid_spec=grid_spec,
        out_shape=jax.ShapeDtypeStruct((t, n), BF16),
        compiler_params=_cparams("arbitrary", "arbitrary"),
        name="in_proj",
    )(src, xb, *([w_in] * per))


def _rope(x, cos, sin):
    w = x.shape[1]
    reps = w // LANES
    c = jnp.concatenate([cos] * reps, axis=1)
    s = jnp.concatenate([sin] * reps, axis=1)
    lane = lax.broadcasted_iota(I32, x.shape, 1)
    first_half = (lane % HEAD_DIM) < HALF_HEAD
    rot = jnp.where(first_half, pltpu.roll(x, w - HALF_HEAD, 1), pltpu.roll(x, HALF_HEAD, 1))
    return x * c + rot * s


def _attn_kernel(q_ref, k_ref, v_ref, kh_ref, vh_ref, cq_ref, sq_ref, ch_ref, sh_ref, sink_ref,
                 o_ref, s_buf, p_buf, r_buf, *, steps_per_seq):
    first = (pl.program_id(0) % steps_per_seq) == 0
    q = _rope(q_ref[...].astype(F32), cq_ref[...], sq_ref[...]) * (HEAD_DIM ** -0.5)
    q = q.astype(BF16)
    k_own = _rope(k_ref[...].astype(F32), cq_ref[...], sq_ref[...])
    k_halo = _rope(kh_ref[...].astype(F32), ch_ref[...], sh_ref[...])
    k_all = jnp.concatenate([k_halo, k_own], axis=0).astype(BF16)
    v_all = jnp.concatenate([vh_ref[...], v_ref[...]], axis=0)
    span = (WINDOW_CHUNKS + 1) * CHUNK
    rows = KV_GROUP * CHUNK
    chunk_rows = N_KV_HEADS * rows
    n_chunks = q.shape[0] // CHUNK
    win_chunk = lax.broadcasted_iota(I32, (rows, span), 1) // CHUNK
    oldest = jnp.where(first, 0, -WINDOW_CHUNKS)

    def tile_rows(c, g):
        return slice(c * chunk_rows + g * rows, c * chunk_rows + (g + 1) * rows)

    def kv_cols(g):
        return slice(g * HEAD_DIM, (g + 1) * HEAD_DIM)

    for c in range(n_chunks):
        q_c = q[c * CHUNK:(c + 1) * CHUNK, :]
        k_win = k_all[c * CHUNK:c * CHUNK + span, :]
        for g in range(N_KV_HEADS):
            qs = jnp.concatenate(
                [q_c[:, (g * KV_GROUP + i) * HEAD_DIM:(g * KV_GROUP + i + 1) * HEAD_DIM]
                 for i in range(KV_GROUP)], axis=0)
            s = lax.dot_general(qs, k_win[:, kv_cols(g)], (((1,), (1,)), ((), ())),
                                preferred_element_type=F32)
            if c < WINDOW_CHUNKS:
                s = jnp.where(win_chunk + (c - WINDOW_CHUNKS) >= oldest, s, NEG_BIG)
            s_buf[tile_rows(c, g), :] = s
    sink = sink_ref[...]
    for c in range(n_chunks):
        blk = slice(c * chunk_rows, (c + 1) * chunk_rows)
        s = s_buf[blk, :]
        m = jnp.maximum(jnp.max(s, axis=1, keepdims=True), sink)
        p = jnp.exp(s - m)
        p_buf[blk, :] = p.astype(BF16)
        r_buf[blk, :] = 1.0 / (jnp.sum(p, axis=1, keepdims=True) + jnp.exp(sink - m))
    for c in range(n_chunks):
        v_win = v_all[c * CHUNK:c * CHUNK + span, :]
        heads = []
        for g in range(N_KV_HEADS):
            o = jnp.dot(p_buf[tile_rows(c, g), :], v_win[:, kv_cols(g)], preferred_element_type=F32)
            o = o * r_buf[tile_rows(c, g), :]
            heads += [o[i * CHUNK:(i + 1) * CHUNK, :] for i in range(KV_GROUP)]
        o_ref[c * CHUNK:(c + 1) * CHUNK, :] = jnp.concatenate(heads, axis=1).astype(o_ref.dtype)


def _attention(proj, cos, sin, sinks, seq_len):
    t = proj.shape[0]
    tq = ATTN_TQ
    steps_per_seq = seq_len // tq
    halo_per_tq = tq // ATTN_HALO
    q_blk = COL_Q // ATTN_WIDTH
    k_blk = COL_K // KV_WIDTH
    v_blk = COL_V // KV_WIDTH

    def halo_row(i):
        return jnp.maximum(i * halo_per_tq - 1, 0)

    span = (WINDOW_CHUNKS + 1) * CHUNK
    tile_rows = (tq // CHUNK) * N_HEADS * CHUNK
    sink_rows = jnp.repeat(sinks, CHUNK).reshape(N_HEADS * CHUNK, 1)
    return pl.pallas_call(
        functools.partial(_attn_kernel, steps_per_seq=steps_per_seq),
        grid=(t // tq,),
        in_specs=[
            pl.BlockSpec((tq, ATTN_WIDTH), lambda i: (i, q_blk)),
            pl.BlockSpec((tq, KV_WIDTH), lambda i: (i, k_blk)),
            pl.BlockSpec((tq, KV_WIDTH), lambda i: (i, v_blk)),
            pl.BlockSpec((ATTN_HALO, KV_WIDTH), lambda i: (halo_row(i), k_blk)),
            pl.BlockSpec((ATTN_HALO, KV_WIDTH), lambda i: (halo_row(i), v_blk)),
            pl.BlockSpec((tq, LANES), lambda i: (i, 0)),
            pl.BlockSpec((tq, LANES), lambda i: (i, 0)),
            pl.BlockSpec((ATTN_HALO, LANES), lambda i: (halo_row(i), 0)),
            pl.BlockSpec((ATTN_HALO, LANES), lambda i: (halo_row(i), 0)),
            pl.BlockSpec((N_HEADS * CHUNK, 1), lambda i: (0, 0)),
        ],
        out_specs=pl.BlockSpec((tq, ATTN_WIDTH), lambda i: (i, 0)),
        out_shape=jax.ShapeDtypeStruct((t, ATTN_WIDTH), BF16),
        scratch_shapes=[pltpu.VMEM((tile_rows, span), F32), pltpu.VMEM((tile_rows, span), BF16),
                        pltpu.VMEM((tile_rows, 1), F32)],
        compiler_params=_cparams("parallel"),
        name="swa_attention",
    )(proj, proj, proj, proj, proj, cos, sin, cos, sin, sink_rows)


def _mix_kernel(up_ref, ul_ref, ug_ref, pw_ref, ps_ref, cw_ref, cb_ref, wa_ref, ba_ref, wx_ref,
                bx_ref, lam_ref, yp_ref, yl_ref, halo_p, halo_l, h_carry):
    s = pl.program_id(1)
    tb = up_ref.shape[0]

    @pl.when(s == 0)
    def _():
        halo_p[...] = jnp.zeros_like(halo_p)
        halo_l[...] = jnp.zeros_like(halo_l)
        h_carry[...] = jnp.zeros_like(h_carry)

    u = up_ref[...].astype(F32)
    ext = jnp.concatenate([halo_p[...], u], axis=0)
    s2 = ext + pltpu.roll(ext, 1, 0)
    s4 = s2 + pltpu.roll(s2, 2, 0)
    s8 = s4 + pltpu.roll(s4, 4, 0)
    s16 = s8 + pltpu.roll(s8, 8, 0)
    t_idx = (s * tb + lax.broadcasted_iota(I32, (tb, 1), 0)).astype(F32)
    for g, (win, ws) in enumerate(zip(POOL_WINDOWS, (s2, s4, s8, s16))):
        cols = slice(g * POOL_GROUP, (g + 1) * POOL_GROUP)
        count = jnp.minimum(t_idx + 1.0, float(win))
        pooled = ws[POOL_HALO:, cols] / count - u[:, cols]
        y = jnp.dot(pooled.astype(BF16), pw_ref[g], preferred_element_type=F32)
        yp_ref[:, cols] = (y * ps_ref[:, cols]).astype(yp_ref.dtype)
    halo_p[...] = u[tb - POOL_HALO:, :]

    ul = ul_ref[...].astype(F32)
    extl = jnp.concatenate([halo_l[...], ul], axis=0)
    xc = cb_ref[...] + cw_ref[CONV_WIDTH - 1:CONV_WIDTH, :] * ul
    for j in range(1, CONV_WIDTH):
        w_j = cw_ref[CONV_WIDTH - 1 - j:CONV_WIDTH - j, :]
        xc = xc + w_j * pltpu.roll(extl, j, 0)[CONV_HALO:, :]
    halo_l[...] = ul[tb - CONV_HALO:, :]
    xcb = xc.astype(BF16)
    r_parts, i_parts = [], []
    for j in range(LRU_WIDTH // MXU_DIM):
        cols = slice(j * MXU_DIM, (j + 1) * MXU_DIM)
        r_parts.append(jnp.dot(xcb[:, cols], wa_ref[j], preferred_element_type=F32))
        i_parts.append(jnp.dot(xcb[:, cols], wx_ref[j], preferred_element_type=F32))
    r = jax.nn.sigmoid(jnp.concatenate(r_parts, axis=1) + ba_ref[...])
    ig = jax.nn.sigmoid(jnp.concatenate(i_parts, axis=1) + bx_ref[...])
    z = -lam_ref[...]
    softplus = jnp.maximum(z, 0.0) + jnp.log1p(jnp.exp(-jnp.abs(z)))
    log_a = (-LRU_C) * r * softplus
    a = jnp.exp(log_a)
    b = jnp.sqrt(1.0 - jnp.exp(2.0 * log_a)) * (ig * xc)
    row_in_group = lax.broadcasted_iota(I32, a.shape, 0) % SUBLANES
    d = 1
    while d < SUBLANES:
        keep = row_in_group >= d
        a_prev = jnp.where(keep, pltpu.roll(a, d, 0), 1.0)
        b_prev = jnp.where(keep, pltpu.roll(b, d, 0), 0.0)
        b = a * b_prev + b
        a = a * a_prev
        d *= 2
    gelu = jax.nn.gelu(ug_ref[...].astype(F32))
    h_prev = h_carry[...]
    pair = 2 * SUBLANES
    for g in range(tb // pair):
        hs = []
        for half in range(2):
            rows = slice(g * pair + half * SUBLANES, g * pair + (half + 1) * SUBLANES)
            h = a[rows, :] * h_prev + b[rows, :]
            h_prev = h[SUBLANES - 1:SUBLANES, :]
            hs.append(h)
        rows = slice(g * pair, (g + 1) * pair)
        yl_ref[rows, :] = (jnp.concatenate(hs, axis=0) * gelu[rows, :]).astype(yl_ref.dtype)
    h_carry[...] = h_prev


def _block_diag(w):
    per = MXU_DIM // LRU_BLOCK
    w = w.reshape(LRU_BLOCKS // per, per, LRU_BLOCK, LRU_BLOCK)
    eye = jnp.eye(per, dtype=w.dtype)
    out = jnp.einsum("gpij,pq->gpiqj", w, eye)
    return out.reshape(LRU_BLOCKS // per, MXU_DIM, MXU_DIM)


def _mixers(proj, p, batch, seq_len):
    t = proj.shape[0]
    tb = min(MIX_TB, seq_len)
    steps = seq_len // tb

    def rows(bi, si):
        return bi * steps + si

    def col_spec(col):
        blk = col // POOL_WIDTH
        return pl.BlockSpec((tb, POOL_WIDTH), lambda bi, si: (rows(bi, si), blk))

    def const(shape):
        nd = len(shape)
        return pl.BlockSpec(shape, lambda bi, si: (0,) * nd)

    out_spec = pl.BlockSpec((tb, POOL_WIDTH), lambda bi, si: (rows(bi, si), 0))
    return pl.pallas_call(
        _mix_kernel,
        grid=(batch, steps),
        in_specs=[col_spec(COL_POOL), col_spec(COL_LRU), col_spec(COL_GELU),
                  const((len(POOL_WINDOWS), POOL_GROUP, POOL_GROUP)), const((1, POOL_WIDTH)),
                  const((CONV_WIDTH, LRU_WIDTH)), const((1, LRU_WIDTH)),
                  const((LRU_WIDTH // MXU_DIM, MXU_DIM, MXU_DIM)), const((1, LRU_WIDTH)),
                  const((LRU_WIDTH // MXU_DIM, MXU_DIM, MXU_DIM)), const((1, LRU_WIDTH)),
                  const((1, LRU_WIDTH))],
        out_specs=[out_spec, out_spec],
        out_shape=[jax.ShapeDtypeStruct((t, POOL_WIDTH), BF16),
                   jax.ShapeDtypeStruct((t, LRU_WIDTH), BF16)],
        scratch_shapes=[pltpu.VMEM((POOL_HALO, POOL_WIDTH), F32),
                        pltpu.VMEM((CONV_HALO, LRU_WIDTH), F32),
                        pltpu.VMEM((1, LRU_WIDTH), F32)],
        compiler_params=_cparams("arbitrary", "arbitrary"),
        name="pool_rglru",
    )(proj, proj, proj, p["pool_w"], p["pool_scale"], p["conv_w"], p["conv_b"],
      p["lru_wa"], p["lru_ba"], p["lru_wx"], p["lru_bx"], p["lru_lambda"])


def _merge_kernel(gl_ref, ya_ref, yp_ref, yl_ref, x_ref, bg_ref, wb_ref, wo_ref, g_ref, b_ref,
                  xo_ref, xb_ref, xp_ref):
    merged = None
    for j, y_ref in enumerate((ya_ref, yp_ref, yl_ref)):
        cols = slice(j * D_MODEL, (j + 1) * D_MODEL)
        z = jnp.dot(y_ref[...], wb_ref[j], preferred_element_type=F32)
        gate = jax.nn.sigmoid(gl_ref[:, cols].astype(F32) + bg_ref[:, cols])
        merged = gate * z if merged is None else merged + gate * z
    h = jnp.dot(merged.astype(BF16), wo_ref[...], preferred_element_type=F32)
    y = _layer_norm(DEEPNORM_ALPHA * x_ref[...] + h, g_ref[...], b_ref[...])
    xo_ref[...] = y
    xb_ref[...] = y.astype(BF16)
    _store_pieces(xp_ref, y)


def _merge(proj, ya, yp, yl, x, x_row0, p):
    t = proj.shape[0]
    tm = min(MERGE_TM, t)
    assert x_row0 % tm == 0
    row = lambda i: (i, 0)
    return pl.pallas_call(
        _merge_kernel,
        grid=(t // tm,),
        in_specs=[pl.BlockSpec((tm, GATE_WIDTH), row),
                  pl.BlockSpec((tm, ATTN_WIDTH), row),
                  pl.BlockSpec((tm, POOL_WIDTH), row),
                  pl.BlockSpec((tm, LRU_WIDTH), row),
                  pl.BlockSpec((tm, D_MODEL), lambda i: (i + x_row0 // tm, 0)),
                  _resident((1, GATE_WIDTH), lambda i: (0, 0)),
                  _resident((N_BRANCH, ATTN_WIDTH, D_MODEL), lambda i: (0, 0, 0)),
                  _resident((D_MODEL, D_MODEL), lambda i: (0, 0)),
                  _resident((1, D_MODEL), lambda i: (0, 0)),
                  _resident((1, D_MODEL), lambda i: (0, 0))],
        out_specs=[pl.BlockSpec((tm, D_MODEL), row), pl.BlockSpec((tm, D_MODEL), row),
                   pl.BlockSpec((SUB_ROWS, tm, SUB_WORDS), lambda i: (0, i, 0))],
        out_shape=[jax.ShapeDtypeStruct((t, D_MODEL), F32),
                   jax.ShapeDtypeStruct((t, D_MODEL), BF16),
                   jax.ShapeDtypeStruct((SUB_ROWS, t, SUB_WORDS), U32)],
        compiler_params=_cparams("parallel"),
        name="merge_ln",
    )(proj, ya, yp, yl, x, p["b_gate"], p["w_branch"], p["w_out"], p["ln1_g"], p["ln1_b"])


def _router_kernel(x_ref, rw_ref, rb_ref, idx_ref, gate_ref, rank_ref, cnt_ref, carry):
    @pl.when(pl.program_id(0) == 0)
    def _():
        carry[...] = jnp.zeros_like(carry)

    tm = x_ref.shape[0]
    logits = lax.dot_general(rw_ref[...], x_ref[...], (((1,), (1,)), ((), ())),
                             preferred_element_type=F32)
    scores = jax.nn.sigmoid(logits)
    sel = scores + rb_ref[...]
    expert = lax.broadcasted_iota(I32, scores.shape, 0)
    idxs, tops = [], []
    chosen = jnp.zeros(scores.shape, F32)
    for _ in range(TOP_K):
        best = jnp.max(sel, axis=0, keepdims=True)
        ik = jnp.min(jnp.where(sel == best, expert, N_EXPERTS), axis=0, keepdims=True)
        hit = expert == ik
        tops.append(jnp.sum(jnp.where(hit, scores, 0.0), axis=0, keepdims=True))
        idxs.append(ik)
        chosen = jnp.where(hit, 1.0, chosen)
        sel = jnp.where(hit, -jnp.inf, sel)
    top = jnp.concatenate(tops, axis=0)
    idx = jnp.concatenate(idxs, axis=0)
    gates = top / jnp.sum(top, axis=0, keepdims=True) * ROUTED_SCALE
    gate_ref[...] = jnp.transpose(gates)
    idx_ref[...] = idx
    earlier = (lax.broadcasted_iota(I32, (tm, tm), 0) < lax.broadcasted_iota(I32, (tm, tm), 1))
    before = jnp.dot(chosen.astype(BF16), jnp.where(earlier, 1.0, 0.0).astype(BF16),
                     preferred_element_type=F32)
    base = before + carry[:, 0:1]
    ranks = [jnp.sum(jnp.where(expert == idxs[k], base, 0.0), axis=0, keepdims=True)
             for k in range(TOP_K)]
    rank_ref[...] = jnp.concatenate(ranks, axis=0).astype(I32)
    carry[...] = carry[...] + jnp.sum(chosen, axis=1, keepdims=True)
    cnt_ref[...] = carry[...].astype(I32)


def _router(x, rw_t, rbias):
    t = x.shape[0]
    tm = min(ROUTER_TM, t)
    tok = lambda i: (0, i)
    return pl.pallas_call(
        _router_kernel,
        grid=(t // tm,),
        in_specs=[pl.BlockSpec((tm, D_MODEL), lambda i: (i, 0)),
                  pl.BlockSpec((N_EXPERTS, D_MODEL), lambda i: (0, 0)),
                  pl.BlockSpec((N_EXPERTS, 1), lambda i: (0, 0))],
        out_specs=[pl.BlockSpec((TOP_K, tm), tok), pl.BlockSpec((tm, TOP_K), lambda i: (i, 0)),
                   pl.BlockSpec((TOP_K, tm), tok),
                   pl.BlockSpec((N_EXPERTS, LANES), lambda i: (0, 0))],
        out_shape=[jax.ShapeDtypeStruct((TOP_K, t), I32), jax.ShapeDtypeStruct((t, TOP_K), F32),
                   jax.ShapeDtypeStruct((TOP_K, t), I32),
                   jax.ShapeDtypeStruct((N_EXPERTS, LANES), I32)],
        scratch_shapes=[pltpu.VMEM((N_EXPERTS, LANES), F32)],
        compiler_params=_cparams("arbitrary"),
        name="router_topk",
    )(x, rw_t, rbias)


def _dest_kernel(start_ref, idx_ref, rank_ref, ids_ref, *, n_rows):
    idx = idx_ref[...]
    dest = rank_ref[...] + pl.program_id(0) * n_rows
    for e in range(N_EXPERTS):
        dest = dest + jnp.where(idx == e, start_ref[e], 0)
    ids_ref[...] = dest


def _piece_ids(pad_start, idx, rank, n_rows):
    t = idx.shape[1]
    tm = min(DEST_TM, t)
    spec = pl.BlockSpec((TOP_K, tm), lambda j, i, s: (0, i))
    return pl.pallas_call(
        functools.partial(_dest_kernel, n_rows=n_rows),
        grid_spec=pltpu.PrefetchScalarGridSpec(
            num_scalar_prefetch=1, grid=(SUB_ROWS, t // tm), in_specs=[spec, spec],
            out_specs=pl.BlockSpec((TOP_K, tm), lambda j, i, s: (j, i))),
        out_shape=jax.ShapeDtypeStruct((SUB_ROWS * TOP_K, t), I32),
        compiler_params=_cparams("parallel", "parallel"),
        name="piece_ids",
    )(pad_start, idx, rank)


def _swiglu(xb, w1_ref, w3_ref, w2_ref):
    a = jnp.dot(xb, w1_ref[...], preferred_element_type=F32)
    g = jnp.dot(xb, w3_ref[...], preferred_element_type=F32)
    hid = (a * jax.nn.sigmoid(a) * g).astype(BF16)
    return jnp.dot(hid, w2_ref[...], preferred_element_type=F32)


def _sc_mesh():
    return plsc.VectorSubcoreMesh(core_axis_name="core", subcore_axis_name="subcore")


def _sc_dispatch(xp, ids, n_rows):
    t = xp.shape[1]
    per_plane = t // SC_WINDOW

    @pl.kernel(out_type=jax.ShapeDtypeStruct((SUB_ROWS * n_rows, SUB_WORDS), U32), mesh=_sc_mesh(),
               scratch_types=[])
    def scatter(x_hbm, i_hbm, o_hbm):
        def body(x_vmem, i_vmem):
            for k in range(TOP_K):
                pltpu.sync_copy(x_vmem, o_hbm.at[i_vmem.at[k]])

        pltpu.emit_pipeline(
            body,
            grid=(SUB_ROWS * per_plane,),
            in_specs=[pl.BlockSpec((SC_WINDOW, SUB_WORDS), lambda w: (w, 0)),
                      pl.BlockSpec((TOP_K, SC_WINDOW), lambda w: (w // per_plane, w % per_plane))],
            out_specs=[],
            core_axis_name=("core", "subcore"),
            dimension_semantics=(pltpu.PARALLEL,),
        )(x_hbm, i_hbm)

    out = scatter(xp.reshape(SUB_ROWS * t, SUB_WORDS), ids)
    return out.reshape(SUB_ROWS, n_rows, SUB_WORDS)


def _sc_gather(ys, ids):
    n_rows = ys.shape[1]
    t = ids.shape[1]
    per_row = t // SC_WINDOW
    pieces = SUB_ROWS * TOP_K * t

    @pl.kernel(out_type=jax.ShapeDtypeStruct((pieces, SUB_WORDS), U32), mesh=_sc_mesh(),
               scratch_types=[])
    def gather(y_hbm, i_hbm, o_hbm):
        def body(i_vmem, o_vmem):
            pltpu.sync_copy(y_hbm.at[i_vmem.at[0]], o_vmem)

        pltpu.emit_pipeline(
            body,
            grid=(pieces // SC_WINDOW,),
            in_specs=[pl.BlockSpec((1, SC_WINDOW), lambda w: (w // per_row, w % per_row))],
            out_specs=[pl.BlockSpec((SC_WINDOW, SUB_WORDS), lambda w: (w, 0))],
            core_axis_name=("core", "subcore"),
            dimension_semantics=(pltpu.PARALLEL,),
        )(i_hbm, o_hbm)

    out = gather(ys.reshape(SUB_ROWS * n_rows, SUB_WORDS), ids)
    return out.reshape(SUB_ROWS, TOP_K * t, SUB_WORDS)


def _expert_kernel(bexp_ref, blive_ref, nused_ref, bfirst_ref, bslot_ref, bnext_ref, xs_ref, w1_hbm,
                   w3_hbm, w2_hbm, ys_ref, st1, st3, st2, w1b, w3b, w2b, xb, sem, *, layer):
    b = pl.program_id(0)
    used = b < nused_ref[0]

    def weight_copies(e, slot):
        return (pltpu.make_async_copy(w1_hbm.at[layer, e], st1.at[slot], sem.at[slot, 0]),
                pltpu.make_async_copy(w3_hbm.at[layer, e], st3.at[slot], sem.at[slot, 1]),
                pltpu.make_async_copy(w2_hbm.at[layer, e], st2.at[slot], sem.at[slot, 2]))

    @pl.when(b == 0)
    def _():
        for cp in weight_copies(bexp_ref[0], 0):
            cp.start()

    @pl.when(jnp.logical_and(used, bfirst_ref[b] == 1))
    def _():
        slot = bslot_ref[b]
        for cp in weight_copies(bexp_ref[b], slot):
            cp.wait()

        @pl.when(bnext_ref[b] >= 0)
        def _():
            for cp in weight_copies(bnext_ref[b], 1 - slot):
                cp.start()

        w1b[...] = st1[slot].astype(BF16)
        w3b[...] = st3[slot].astype(BF16)
        w2b[...] = st2[slot].astype(BF16)

    @pl.when(used)
    def _():
        live = lax.broadcasted_iota(I32, (xb.shape[0], SUB_WORDS), 0) < blive_ref[b]
        for j in range(SUB_ROWS):
            lo, hi = _unpack_rows(jnp.where(live, xs_ref[j], jnp.uint32(0)))
            lo_cols, hi_cols = _piece_cols(j)
            xb[:, lo_cols] = lo.astype(BF16)
            xb[:, hi_cols] = hi.astype(BF16)
        _store_pieces(ys_ref, _swiglu(xb[...], w1b, w3b, w2b))

    @pl.when(jnp.logical_not(used))
    def _():
        ys_ref[...] = jnp.zeros_like(ys_ref)


def _experts(r, xs, w1, w3, w2, layer):
    n_blocks = xs.shape[1] // EXPERT_BM
    n_tables = 6

    def rows_spec(index):
        return pl.BlockSpec((SUB_ROWS, EXPERT_BM, SUB_WORDS), lambda b, *tables: (0, index(b, tables), 0))

    grid_spec = pltpu.PrefetchScalarGridSpec(
        num_scalar_prefetch=n_tables,
        grid=(n_blocks,),
        in_specs=[rows_spec(lambda b, tables: jnp.minimum(b, tables[2][0] - 1)),
                  pl.BlockSpec(memory_space=pl.ANY), pl.BlockSpec(memory_space=pl.ANY),
                  pl.BlockSpec(memory_space=pl.ANY)],
        out_specs=rows_spec(lambda b, tables: b),
        scratch_shapes=[pltpu.VMEM((2, D_MODEL, D_EXPERT), F32), pltpu.VMEM((2, D_MODEL, D_EXPERT), F32),
                        pltpu.VMEM((2, D_EXPERT, D_MODEL), F32),
                        pltpu.VMEM((D_MODEL, D_EXPERT), BF16), pltpu.VMEM((D_MODEL, D_EXPERT), BF16),
                        pltpu.VMEM((D_EXPERT, D_MODEL), BF16), pltpu.VMEM((EXPERT_BM, D_MODEL), BF16),
                        pltpu.SemaphoreType.DMA((2, 3))],
    )
    return pl.pallas_call(
        functools.partial(_expert_kernel, layer=layer),
        grid_spec=grid_spec,
        out_shape=jax.ShapeDtypeStruct(xs.shape, U32),
        compiler_params=_cparams("arbitrary"),
        name="moe_experts",
    )(r["block_exp"], r["block_live"], r["n_used"], r["block_first"], r["block_slot"], r["block_next"],
      xs, w1, w3, w2)


def _combine_kernel(gate_ref, x_ref, xb_ref, s1_ref, s3_ref, s2_ref, g_ref, b_ref, *refs):
    yk_refs = refs[:TOP_K]
    xo_ref, xbo_ref, hbuf = refs[-3:]
    tm = x_ref.shape[0]
    shared = _swiglu(xb_ref[...], s1_ref, s3_ref, s2_ref)
    gates = [jnp.broadcast_to(gate_ref[:, k:k + 1], (tm, SUB_WORDS)) for k in range(TOP_K)]
    for j in range(SUB_ROWS):
        lo_cols, hi_cols = _piece_cols(j)
        acc_lo = shared[:, lo_cols]
        acc_hi = shared[:, hi_cols]
        for k in range(TOP_K):
            lo, hi = _unpack_rows(yk_refs[k][j])
            acc_lo = acc_lo + gates[k] * lo
            acc_hi = acc_hi + gates[k] * hi
        hbuf[:, lo_cols] = acc_lo
        hbuf[:, hi_cols] = acc_hi
    y = _layer_norm(DEEPNORM_ALPHA * x_ref[...] + hbuf[...], g_ref[...], b_ref[...])
    xo_ref[...] = y
    xbo_ref[...] = y.astype(BF16)


def _combine(gate_t, x, xb, yg, p, out_rows=None, out_row0=0, out_buf=None):
    t = x.shape[0]
    tm = min(COMBINE_TM, t)
    steps = t // tm
    row = lambda i: (i, 0)
    assert out_row0 % tm == 0

    def slot_spec(k):
        return pl.BlockSpec((SUB_ROWS, tm, SUB_WORDS), lambda i: (0, k * steps + i, 0))

    in_specs = [pl.BlockSpec((tm, TOP_K), row),
                pl.BlockSpec((tm, D_MODEL), row),
                pl.BlockSpec((tm, D_MODEL), row),
                _resident((D_MODEL, D_EXPERT), lambda i: (0, 0)),
                _resident((D_MODEL, D_EXPERT), lambda i: (0, 0)),
                _resident((D_EXPERT, D_MODEL), lambda i: (0, 0)),
                _resident((1, D_MODEL), lambda i: (0, 0)),
                _resident((1, D_MODEL), lambda i: (0, 0))] + [slot_spec(k) for k in range(TOP_K)]
    args = [gate_t, x, xb, p["sh_w1"], p["sh_w3"], p["sh_w2"], p["ln2_g"], p["ln2_b"]] + [yg] * TOP_K
    aliases = {}
    if out_buf is not None:
        aliases = {len(args): 0}
        in_specs.append(pl.BlockSpec(memory_space=pl.ANY))
        args.append(out_buf)
    return pl.pallas_call(
        _combine_kernel,
        grid=(steps,),
        in_specs=in_specs,
        out_specs=[pl.BlockSpec((tm, D_MODEL), lambda i: (i + out_row0 // tm, 0)),
                   pl.BlockSpec((tm, D_MODEL), row)],
        out_shape=[jax.ShapeDtypeStruct((out_rows or t, D_MODEL), F32),
                   jax.ShapeDtypeStruct((t, D_MODEL), BF16)],
        scratch_shapes=[pltpu.VMEM((tm, D_MODEL), F32)],
        input_output_aliases=aliases,
        compiler_params=_cparams("parallel"),
        name="moe_combine_ln",
    )(*args)


def _n_expert_blocks(t):
    return -(-(t * TOP_K + N_EXPERTS * (EXPERT_BM - 1)) // EXPERT_BM)


def _route(xb, p):
    t = xb.shape[0]
    n_blocks = _n_expert_blocks(t)
    idx, gate_t, rank, cnt = _router(xb, p["router_w_t"], p["router_bias"])
    counts = cnt[:, 0]
    padded = (counts + EXPERT_BM - 1) // EXPERT_BM * EXPERT_BM
    pad_end = jnp.cumsum(padded)
    pad_start = pad_end - padded
    n_used = (pad_end[-1:] // EXPERT_BM).astype(I32)
    block = jnp.arange(n_blocks, dtype=I32)
    block_exp = jnp.minimum(jnp.sum(pad_end[None, :] <= block[:, None] * EXPERT_BM, axis=1),
                            N_EXPERTS - 1).astype(I32)
    rows_before = (block - pad_start[block_exp] // EXPERT_BM) * EXPERT_BM
    block_live = jnp.clip(counts[block_exp] - rows_before, 0, EXPERT_BM).astype(I32)
    used = block < n_used[0]
    prev_exp = jnp.concatenate([jnp.full((1,), -1, I32), block_exp[:-1]])
    first = used & (block_exp != prev_exp)
    block_slot = ((jnp.cumsum(first.astype(I32)) - 1) % 2).astype(I32)
    later_first = jnp.where(first, block, n_blocks)
    next_first = lax.cummin(jnp.concatenate([later_first[1:], jnp.full((1,), n_blocks, I32)]), reverse=True)
    block_next = jnp.where(next_first < n_blocks, block_exp[jnp.minimum(next_first, n_blocks - 1)], -1)
    n_rows = n_blocks * EXPERT_BM
    ids = _piece_ids(pad_start.astype(I32), idx, rank, n_rows)
    return dict(ids=ids, gate_t=gate_t, block_exp=block_exp, block_live=block_live, n_used=n_used,
                block_first=first.astype(I32), block_slot=block_slot, block_next=block_next.astype(I32),
                n_rows=n_rows)


def _layer_params(l, b_gate, sinks, pool_w, pool_scale, conv_w, conv_b, lru_wa, lru_ba, lru_wx,
                  lru_bx, lru_lambda, w_branch, w_out, ln1_g, ln1_b, router_w, router_bias,
                  sh_w1, sh_w3, sh_w2, ln2_g, ln2_b):
    row = lambda a: a[l].reshape(1, -1).astype(F32)
    return dict(
        b_gate=row(b_gate), sinks=sinks[l].astype(F32),
        pool_w=pool_w[l].astype(BF16), pool_scale=row(pool_scale),
        conv_w=conv_w[l].astype(F32), conv_b=row(conv_b),
        lru_wa=_block_diag(lru_wa[l]).astype(BF16), lru_ba=row(lru_ba),
        lru_wx=_block_diag(lru_wx[l]).astype(BF16), lru_bx=row(lru_bx),
        lru_lambda=row(lru_lambda),
        w_branch=w_branch[l].astype(BF16), w_out=w_out[l].astype(BF16),
        ln1_g=row(ln1_g), ln1_b=row(ln1_b),
        router_w_t=router_w[l].T.astype(BF16), router_bias=router_bias[l].reshape(-1, 1).astype(F32),
        sh_w1=sh_w1[l].astype(BF16), sh_w3=sh_w3[l].astype(BF16), sh_w2=sh_w2[l].astype(BF16),
        ln2_g=row(ln2_g), ln2_b=row(ln2_b),
    )


def kernel(x, positions, w_in, b_gate, sinks, pool_w, pool_scale, conv_w, conv_b, lru_wa, lru_ba, lru_wx, lru_bx, lru_lambda, w_branch, w_out, ln1_g, ln1_b, router_w, router_bias, exp_w1, exp_w3, exp_w2, sh_w1, sh_w3, sh_w2, ln2_g, ln2_b):
    batch, seq_len, d = x.shape
    assert d == D_MODEL and seq_len % max(ATTN_TQ, MIX_TB) == 0 and seq_len % min(PROJ_TM, seq_len) == 0
    t = batch * seq_len
    cos_all, sin_all = _rope_tables(positions)
    x_all = x.reshape(t, d).astype(F32)
    seqs = []
    for s in range(batch):
        rows = slice(s * seq_len, (s + 1) * seq_len)
        seqs.append(dict(xf=x_all, row0=s * seq_len, xb=x_all[rows].astype(BF16),
                         cos=cos_all[rows], sin=sin_all[rows]))
    for l in range(DEPTH):
        p = _layer_params(l, b_gate, sinks, pool_w, pool_scale, conv_w, conv_b, lru_wa, lru_ba,
                          lru_wx, lru_bx, lru_lambda, w_branch, w_out, ln1_g, ln1_b, router_w,
                          router_bias, sh_w1, sh_w3, sh_w2, ln2_g, ln2_b)
        for sq in seqs:
            proj = _in_proj(sq["xb"], w_in, l)
            ya = _attention(proj, sq["cos"], sq["sin"], p["sinks"], seq_len)
            yp, yl = _mixers(proj, p, 1, seq_len)
            sq["xf"], sq["xb"], xp = _merge(proj, ya, yp, yl, sq["xf"], sq["row0"], p)
            sq["row0"] = 0
            sq["route"] = _route(sq["xb"], p)
            sq["xs"] = _sc_dispatch(xp, sq["route"]["ids"], sq["route"]["n_rows"])
        for sq in seqs:
            ys = _experts(sq["route"], sq["xs"], exp_w1, exp_w3, exp_w2, l)
            sq["yg"] = _sc_gather(ys, sq["route"]["ids"])
        out = None
        for s, sq in enumerate(seqs):
            if l + 1 < DEPTH:
                sq["xf"], sq["xb"] = _combine(sq["route"]["gate_t"], sq["xf"], sq["xb"], sq["yg"], p)
            else:
                out, _ = _combine(sq["route"]["gate_t"], sq["xf"], sq["xb"], sq["yg"], p,
                                  out_rows=t, out_row0=s * seq_len, out_buf=out)
    return out.reshape(batch, seq_len, d).astype(x.dtype)
```

```python
import functools

import jax
import jax.numpy as jnp
from jax import lax
from jax.experimental import pallas as pl
from jax.experimental.pallas import tpu as pltpu
from jax.experimental.pallas import tpu_sc as plsc

F32 = jnp.float32
BF16 = jnp.bfloat16
I32 = jnp.int32
U32 = jnp.uint32

D_MODEL = 2048
DEPTH = 2
CHUNK = 64
HEAD_DIM = 64
HALF_HEAD = HEAD_DIM // 2
N_HEADS = 16
N_KV_HEADS = 4
KV_GROUP = N_HEADS // N_KV_HEADS
ATTN_WIDTH = N_HEADS * HEAD_DIM
KV_WIDTH = N_KV_HEADS * HEAD_DIM
WINDOW_CHUNKS = 2
ROPE_THETA = 10000.0
POOL_WINDOWS = (2, 4, 8, 16)
POOL_WIDTH = 1024
POOL_GROUP = POOL_WIDTH // len(POOL_WINDOWS)
LRU_WIDTH = 1024
LRU_BLOCKS = 16
LRU_BLOCK = LRU_WIDTH // LRU_BLOCKS
CONV_WIDTH = 4
LRU_C = 8.0
N_BRANCH = 3
GATE_WIDTH = N_BRANCH * D_MODEL
IN_WIDTH = ATTN_WIDTH + 2 * KV_WIDTH + POOL_WIDTH + 2 * LRU_WIDTH + GATE_WIDTH
N_EXPERTS = 64
TOP_K = 8
D_EXPERT = 512
ROUTED_SCALE = 2.5
LN_EPS = 1e-5
DEEPNORM_ALPHA = (2 * DEPTH) ** 0.25

COL_GATE = 0
COL_Q = GATE_WIDTH
COL_POOL = COL_Q + ATTN_WIDTH
COL_LRU = COL_POOL + POOL_WIDTH
COL_GELU = COL_LRU + LRU_WIDTH
COL_K = COL_GELU + LRU_WIDTH
COL_V = COL_K + KV_WIDTH

LANES = 128
SUBLANES = 8
MXU_DIM = 256
VMEM_LIMIT_BYTES = 56 * 1024 * 1024

PROJ_TM = 1024
PROJ_TN = 1536
ATTN_TQ = 256
ATTN_HALO = WINDOW_CHUNKS * CHUNK
MIX_TB = 256
POOL_HALO = 16
CONV_HALO = 8
MERGE_TM = 256
ROUTER_TM = 512
DEST_TM = 2048
EXPERT_BM = 512
COMBINE_TM = 256
PROJ_SRC = 512
NEG_BIG = -1e30

SUB_WORDS = 256
SUB_COLS = 2 * SUB_WORDS
SUB_ROWS = D_MODEL // SUB_COLS
SC_WINDOW = 128


def _cparams(*sem):
    return pltpu.CompilerParams(dimension_semantics=sem, vmem_limit_bytes=VMEM_LIMIT_BYTES)


def _resident(shape, index_map):
    return pl.BlockSpec(shape, index_map, pipeline_mode=pl.Buffered(1))


def _pack_rows(lo, hi):
    lo_u = lax.bitcast_convert_type(lo.astype(BF16).astype(F32), U32)
    hi_u = lax.bitcast_convert_type(hi.astype(BF16).astype(F32), U32)
    return (lo_u >> 16) | (hi_u & jnp.uint32(0xFFFF0000))


def _unpack_rows(w):
    lo = lax.bitcast_convert_type(w << 16, F32)
    hi = lax.bitcast_convert_type(w & jnp.uint32(0xFFFF0000), F32)
    return lo, hi


def _piece_cols(j):
    return (slice(j * SUB_COLS, j * SUB_COLS + SUB_WORDS),
            slice(j * SUB_COLS + SUB_WORDS, (j + 1) * SUB_COLS))


def _store_pieces(ref, y):
    for j in range(SUB_ROWS):
        lo_cols, hi_cols = _piece_cols(j)
        ref[j] = _pack_rows(y[:, lo_cols], y[:, hi_cols])


def _layer_norm(y, g, b):
    mu = jnp.mean(y, axis=-1, keepdims=True)
    d = y - mu
    var = jnp.mean(d * d, axis=-1, keepdims=True)
    return d * lax.rsqrt(var + LN_EPS) * g + b


def _rope_kernel(pos_ref, inv_ref, cos_ref, sin_ref):
    ang = pos_ref[...].astype(F32) * inv_ref[...]
    lane = lax.broadcasted_iota(I32, ang.shape, 1)
    first_half = (lane % HEAD_DIM) < HALF_HEAD
    cos_ref[...] = jnp.cos(ang)
    s = jnp.sin(ang)
    sin_ref[...] = jnp.where(first_half, -s, s)


def _rope_tables(positions):
    t = positions.size
    tm = min(1024, t)
    inv_freq = 1.0 / (ROPE_THETA ** (jnp.arange(0, HEAD_DIM, 2, dtype=F32) / HEAD_DIM))
    inv = jnp.tile(inv_freq, LANES // HALF_HEAD).reshape(1, LANES)
    return pl.pallas_call(
        _rope_kernel,
        grid=(t // tm,),
        in_specs=[pl.BlockSpec((tm, 1), lambda i: (i, 0)),
                  pl.BlockSpec((1, LANES), lambda i: (0, 0))],
        out_specs=[pl.BlockSpec((tm, LANES), lambda i: (i, 0))] * 2,
        out_shape=[jax.ShapeDtypeStruct((t, LANES), F32)] * 2,
        compiler_params=_cparams("parallel"),
        name="rope_tables",
    )(positions.reshape(t, 1), inv)


def _proj_source_blocks():
    ref_order = (("q", ATTN_WIDTH), ("k", KV_WIDTH), ("v", KV_WIDTH), ("pool", POOL_WIDTH),
                 ("lru", LRU_WIDTH), ("gelu", LRU_WIDTH), ("gate", GATE_WIDTH))
    start, col = {}, 0
    for name, width in ref_order:
        start[name] = col
        col += width
    assert start["v"] == start["k"] + KV_WIDTH and 2 * KV_WIDTH == PROJ_SRC
    out_order = (("gate", GATE_WIDTH), ("q", ATTN_WIDTH), ("pool", POOL_WIDTH), ("lru", LRU_WIDTH),
                 ("gelu", LRU_WIDTH), ("k", 2 * KV_WIDTH))
    blocks = []
    for name, width in out_order:
        assert start[name] % PROJ_SRC == 0 and width % PROJ_SRC == 0
        blocks += [start[name] // PROJ_SRC + c for c in range(width // PROJ_SRC)]
    return blocks


def _in_proj_kernel(src_ref, x_ref, *refs):
    w_refs, o_ref, wb = refs[:-2], refs[-2], refs[-1]

    @pl.when(pl.program_id(1) == 0)
    def _():
        for c, w_ref in enumerate(w_refs):
            wb[:, c * PROJ_SRC:(c + 1) * PROJ_SRC] = w_ref[0].astype(BF16)

    o_ref[...] = jnp.dot(x_ref[...], wb[...], preferred_element_type=F32).astype(o_ref.dtype)


def _in_proj(xb, xb_row0, rows, w_in, layer):
    d = xb.shape[1]
    n = w_in.shape[2]
    tm = min(PROJ_TM, rows)
    assert xb_row0 % tm == 0
    per = PROJ_TN // PROJ_SRC
    src = jnp.asarray(_proj_source_blocks(), I32)

    def w_spec(c):
        return pl.BlockSpec((1, d, PROJ_SRC), lambda j, i, s: (layer, 0, s[per * j + c]))

    grid_spec = pltpu.PrefetchScalarGridSpec(
        num_scalar_prefetch=1,
        grid=(n // PROJ_TN, rows // tm),
        in_specs=[pl.BlockSpec((tm, d), lambda j, i, s: (i + xb_row0 // tm, 0))]
        + [w_spec(c) for c in range(per)],
        out_specs=pl.BlockSpec((tm, PROJ_TN), lambda j, i, s: (i, j)),
        scratch_shapes=[pltpu.VMEM((d, PROJ_TN), BF16)],
    )
    return pl.pallas_call(
        _in_proj_kernel,
        grid_spec=grid_spec,
        out_shape=jax.ShapeDtypeStruct((rows, n), BF16),
        compiler_params=_cparams("arbitrary", "arbitrary"),
        name="in_proj",
    )(src, xb, *([w_in] * per))


def _rope(x, cos, sin):
    w = x.shape[1]
    reps = w // LANES
    c = jnp.concatenate([cos] * reps, axis=1)
    s = jnp.concatenate([sin] * reps, axis=1)
    lane = lax.broadcasted_iota(I32, x.shape, 1)
    first_half = (lane % HEAD_DIM) < HALF_HEAD
    rot = jnp.where(first_half, pltpu.roll(x, w - HALF_HEAD, 1), pltpu.roll(x, HALF_HEAD, 1))
    return x * c + rot * s


def _attn_kernel(q_ref, k_ref, v_ref, kh_ref, vh_ref, cq_ref, sq_ref, ch_ref, sh_ref, sink_ref,
                 o_ref, s_buf, p_buf, r_buf, *, steps_per_seq):
    first = (pl.program_id(0) % steps_per_seq) == 0
    q = _rope(q_ref[...].astype(F32), cq_ref[...], sq_ref[...]) * (HEAD_DIM ** -0.5)
    q = q.astype(BF16)
    k_own = _rope(k_ref[...].astype(F32), cq_ref[...], sq_ref[...])
    k_halo = _rope(kh_ref[...].astype(F32), ch_ref[...], sh_ref[...])
    k_all = jnp.concatenate([k_halo, k_own], axis=0).astype(BF16)
    v_all = jnp.concatenate([vh_ref[...], v_ref[...]], axis=0)
    span = (WINDOW_CHUNKS + 1) * CHUNK
    rows = KV_GROUP * CHUNK
    chunk_rows = N_KV_HEADS * rows
    n_chunks = q.shape[0] // CHUNK
    win_chunk = lax.broadcasted_iota(I32, (rows, span), 1) // CHUNK
    oldest = jnp.where(first, 0, -WINDOW_CHUNKS)

    def tile_rows(c, g):
        return slice(c * chunk_rows + g * rows, c * chunk_rows + (g + 1) * rows)

    def kv_cols(g):
        return slice(g * HEAD_DIM, (g + 1) * HEAD_DIM)

    for c in range(n_chunks):
        q_c = q[c * CHUNK:(c + 1) * CHUNK, :]
        k_win = k_all[c * CHUNK:c * CHUNK + span, :]
        for g in range(N_KV_HEADS):
            qs = jnp.concatenate(
                [q_c[:, (g * KV_GROUP + i) * HEAD_DIM:(g * KV_GROUP + i + 1) * HEAD_DIM]
                 for i in range(KV_GROUP)], axis=0)
            s = lax.dot_general(qs, k_win[:, kv_cols(g)], (((1,), (1,)), ((), ())),
                                preferred_element_type=F32)
            if c < WINDOW_CHUNKS:
                s = jnp.where(win_chunk + (c - WINDOW_CHUNKS) >= oldest, s, NEG_BIG)
            s_buf[tile_rows(c, g), :] = s
    sink = sink_ref[...]
    for c in range(n_chunks):
        blk = slice(c * chunk_rows, (c + 1) * chunk_rows)
        s = s_buf[blk, :]
        m = jnp.maximum(jnp.max(s, axis=1, keepdims=True), sink)
        p = jnp.exp(s - m)
        p_buf[blk, :] = p.astype(BF16)
        r_buf[blk, :] = 1.0 / (jnp.sum(p, axis=1, keepdims=True) + jnp.exp(sink - m))
    for c in range(n_chunks):
        v_win = v_all[c * CHUNK:c * CHUNK + span, :]
        heads = []
        for g in range(N_KV_HEADS):
            o = jnp.dot(p_buf[tile_rows(c, g), :], v_win[:, kv_cols(g)], preferred_element_type=F32)
            o = o * r_buf[tile_rows(c, g), :]
            heads += [o[i * CHUNK:(i + 1) * CHUNK, :] for i in range(KV_GROUP)]
        o_ref[c * CHUNK:(c + 1) * CHUNK, :] = jnp.concatenate(heads, axis=1).astype(o_ref.dtype)


def _attention(proj, cos, sin, sinks, seq_len):
    t = proj.shape[0]
    tq = ATTN_TQ
    steps_per_seq = seq_len // tq
    halo_per_tq = tq // ATTN_HALO
    q_blk = COL_Q // ATTN_WIDTH
    k_blk = COL_K // KV_WIDTH
    v_blk = COL_V // KV_WIDTH

    def halo_row(i):
        return jnp.maximum(i * halo_per_tq - 1, 0)

    span = (WINDOW_CHUNKS + 1) * CHUNK
    tile_rows = (tq // CHUNK) * N_HEADS * CHUNK
    sink_rows = jnp.repeat(sinks, CHUNK).reshape(N_HEADS * CHUNK, 1)
    return pl.pallas_call(
        functools.partial(_attn_kernel, steps_per_seq=steps_per_seq),
        grid=(t // tq,),
        in_specs=[
            pl.BlockSpec((tq, ATTN_WIDTH), lambda i: (i, q_blk)),
            pl.BlockSpec((tq, KV_WIDTH), lambda i: (i, k_blk)),
            pl.BlockSpec((tq, KV_WIDTH), lambda i: (i, v_blk)),
            pl.BlockSpec((ATTN_HALO, KV_WIDTH), lambda i: (halo_row(i), k_blk)),
            pl.BlockSpec((ATTN_HALO, KV_WIDTH), lambda i: (halo_row(i), v_blk)),
            pl.BlockSpec((tq, LANES), lambda i: (i, 0)),
            pl.BlockSpec((tq, LANES), lambda i: (i, 0)),
            pl.BlockSpec((ATTN_HALO, LANES), lambda i: (halo_row(i), 0)),
            pl.BlockSpec((ATTN_HALO, LANES), lambda i: (halo_row(i), 0)),
            pl.BlockSpec((N_HEADS * CHUNK, 1), lambda i: (0, 0)),
        ],
        out_specs=pl.BlockSpec((tq, ATTN_WIDTH), lambda i: (i, 0)),
        out_shape=jax.ShapeDtypeStruct((t, ATTN_WIDTH), BF16),
        scratch_shapes=[pltpu.VMEM((tile_rows, span), F32), pltpu.VMEM((tile_rows, span), BF16),
                        pltpu.VMEM((tile_rows, 1), F32)],
        compiler_params=_cparams("parallel"),
        name="swa_attention",
    )(proj, proj, proj, proj, proj, cos, sin, cos, sin, sink_rows)


def _mix_kernel(up_ref, ul_ref, ug_ref, pw_ref, ps_ref, cw_ref, cb_ref, wa_ref, ba_ref, wx_ref,
                bx_ref, lam_ref, yp_ref, yl_ref, halo_p, halo_l, h_carry):
    s = pl.program_id(1)
    tb = up_ref.shape[0]

    @pl.when(s == 0)
    def _():
        halo_p[...] = jnp.zeros_like(halo_p)
        halo_l[...] = jnp.zeros_like(halo_l)
        h_carry[...] = jnp.zeros_like(h_carry)

    u = up_ref[...].astype(F32)
    ext = jnp.concatenate([halo_p[...], u], axis=0)
    s2 = ext + pltpu.roll(ext, 1, 0)
    s4 = s2 + pltpu.roll(s2, 2, 0)
    s8 = s4 + pltpu.roll(s4, 4, 0)
    s16 = s8 + pltpu.roll(s8, 8, 0)
    t_idx = (s * tb + lax.broadcasted_iota(I32, (tb, 1), 0)).astype(F32)
    for g, (win, ws) in enumerate(zip(POOL_WINDOWS, (s2, s4, s8, s16))):
        cols = slice(g * POOL_GROUP, (g + 1) * POOL_GROUP)
        count = jnp.minimum(t_idx + 1.0, float(win))
        pooled = ws[POOL_HALO:, cols] / count - u[:, cols]
        y = jnp.dot(pooled.astype(BF16), pw_ref[g], preferred_element_type=F32)
        yp_ref[:, cols] = (y * ps_ref[:, cols]).astype(yp_ref.dtype)
    halo_p[...] = u[tb - POOL_HALO:, :]

    ul = ul_ref[...].astype(F32)
    extl = jnp.concatenate([halo_l[...], ul], axis=0)
    xc = cb_ref[...] + cw_ref[CONV_WIDTH - 1:CONV_WIDTH, :] * ul
    for j in range(1, CONV_WIDTH):
        w_j = cw_ref[CONV_WIDTH - 1 - j:CONV_WIDTH - j, :]
        xc = xc + w_j * pltpu.roll(extl, j, 0)[CONV_HALO:, :]
    halo_l[...] = ul[tb - CONV_HALO:, :]
    xcb = xc.astype(BF16)
    r_parts, i_parts = [], []
    for j in range(LRU_WIDTH // MXU_DIM):
        cols = slice(j * MXU_DIM, (j + 1) * MXU_DIM)
        r_parts.append(jnp.dot(xcb[:, cols], wa_ref[j], preferred_element_type=F32))
        i_parts.append(jnp.dot(xcb[:, cols], wx_ref[j], preferred_element_type=F32))
    r = jax.nn.sigmoid(jnp.concatenate(r_parts, axis=1) + ba_ref[...])
    ig = jax.nn.sigmoid(jnp.concatenate(i_parts, axis=1) + bx_ref[...])
    z = -lam_ref[...]
    softplus = jnp.maximum(z, 0.0) + jnp.log1p(jnp.exp(-jnp.abs(z)))
    log_a = (-LRU_C) * r * softplus
    a = jnp.exp(log_a)
    b = jnp.sqrt(1.0 - jnp.exp(2.0 * log_a)) * (ig * xc)
    row_in_group = lax.broadcasted_iota(I32, a.shape, 0) % SUBLANES
    d = 1
    while d < SUBLANES:
        keep = row_in_group >= d
        a_prev = jnp.where(keep, pltpu.roll(a, d, 0), 1.0)
        b_prev = jnp.where(keep, pltpu.roll(b, d, 0), 0.0)
        b = a * b_prev + b
        a = a * a_prev
        d *= 2
    gelu = jax.nn.gelu(ug_ref[...].astype(F32))
    h_prev = h_carry[...]
    pair = 2 * SUBLANES
    for g in range(tb // pair):
        hs = []
        for half in range(2):
            rows = slice(g * pair + half * SUBLANES, g * pair + (half + 1) * SUBLANES)
            h = a[rows, :] * h_prev + b[rows, :]
            h_prev = h[SUBLANES - 1:SUBLANES, :]
            hs.append(h)
        rows = slice(g * pair, (g + 1) * pair)
        yl_ref[rows, :] = (jnp.concatenate(hs, axis=0) * gelu[rows, :]).astype(yl_ref.dtype)
    h_carry[...] = h_prev


def _block_diag(w):
    per = MXU_DIM // LRU_BLOCK
    w = w.reshape(LRU_BLOCKS // per, per, LRU_BLOCK, LRU_BLOCK)
    eye = jnp.eye(per, dtype=w.dtype)
    out = jnp.einsum("gpij,pq->gpiqj", w, eye)
    return out.reshape(LRU_BLOCKS // per, MXU_DIM, MXU_DIM)


def _mixers(proj, p, batch, seq_len):
    t = proj.shape[0]
    tb = min(MIX_TB, seq_len)
    steps = seq_len // tb

    def rows(bi, si):
        return bi * steps + si

    def col_spec(col):
        blk = col // POOL_WIDTH
        return pl.BlockSpec((tb, POOL_WIDTH), lambda bi, si: (rows(bi, si), blk))

    def const(shape):
        nd = len(shape)
        return pl.BlockSpec(shape, lambda bi, si: (0,) * nd)

    out_spec = pl.BlockSpec((tb, POOL_WIDTH), lambda bi, si: (rows(bi, si), 0))
    return pl.pallas_call(
        _mix_kernel,
        grid=(batch, steps),
        in_specs=[col_spec(COL_POOL), col_spec(COL_LRU), col_spec(COL_GELU),
                  const((len(POOL_WINDOWS), POOL_GROUP, POOL_GROUP)), const((1, POOL_WIDTH)),
                  const((CONV_WIDTH, LRU_WIDTH)), const((1, LRU_WIDTH)),
                  const((LRU_WIDTH // MXU_DIM, MXU_DIM, MXU_DIM)), const((1, LRU_WIDTH)),
                  const((LRU_WIDTH // MXU_DIM, MXU_DIM, MXU_DIM)), const((1, LRU_WIDTH)),
                  const((1, LRU_WIDTH))],
        out_specs=[out_spec, out_spec],
        out_shape=[jax.ShapeDtypeStruct((t, POOL_WIDTH), BF16),
                   jax.ShapeDtypeStruct((t, LRU_WIDTH), BF16)],
        scratch_shapes=[pltpu.VMEM((POOL_HALO, POOL_WIDTH), F32),
                        pltpu.VMEM((CONV_HALO, LRU_WIDTH), F32),
                        pltpu.VMEM((1, LRU_WIDTH), F32)],
        compiler_params=_cparams("arbitrary", "arbitrary"),
        name="pool_rglru",
    )(proj, proj, proj, p["pool_w"], p["pool_scale"], p["conv_w"], p["conv_b"],
      p["lru_wa"], p["lru_ba"], p["lru_wx"], p["lru_bx"], p["lru_lambda"])


def _merge_kernel(gl_ref, ya_ref, yp_ref, yl_ref, x_ref, bg_ref, wb_ref, wo_ref, g_ref, b_ref,
                  xo_ref, xb_ref, xp_ref):
    merged = None
    for j, y_ref in enumerate((ya_ref, yp_ref, yl_ref)):
        cols = slice(j * D_MODEL, (j + 1) * D_MODEL)
        z = jnp.dot(y_ref[...], wb_ref[j], preferred_element_type=F32)
        gate = jax.nn.sigmoid(gl_ref[:, cols].astype(F32) + bg_ref[:, cols])
        merged = gate * z if merged is None else merged + gate * z
    h = jnp.dot(merged.astype(BF16), wo_ref[...], preferred_element_type=F32)
    y = _layer_norm(DEEPNORM_ALPHA * x_ref[...] + h, g_ref[...], b_ref[...])
    xo_ref[...] = y
    xb_ref[...] = y.astype(BF16)
    _store_pieces(xp_ref, y)


def _merge(proj, ya, yp, yl, x, x_row0, p):
    t = proj.shape[0]
    tm = min(MERGE_TM, t)
    assert x_row0 % tm == 0
    row = lambda i: (i, 0)
    return pl.pallas_call(
        _merge_kernel,
        grid=(t // tm,),
        in_specs=[pl.BlockSpec((tm, GATE_WIDTH), row),
                  pl.BlockSpec((tm, ATTN_WIDTH), row),
                  pl.BlockSpec((tm, POOL_WIDTH), row),
                  pl.BlockSpec((tm, LRU_WIDTH), row),
                  pl.BlockSpec((tm, D_MODEL), lambda i: (i + x_row0 // tm, 0)),
                  _resident((1, GATE_WIDTH), lambda i: (0, 0)),
                  _resident((N_BRANCH, ATTN_WIDTH, D_MODEL), lambda i: (0, 0, 0)),
                  _resident((D_MODEL, D_MODEL), lambda i: (0, 0)),
                  _resident((1, D_MODEL), lambda i: (0, 0)),
                  _resident((1, D_MODEL), lambda i: (0, 0))],
        out_specs=[pl.BlockSpec((tm, D_MODEL), row), pl.BlockSpec((tm, D_MODEL), row),
                   pl.BlockSpec((SUB_ROWS, tm, SUB_WORDS), lambda i: (0, i, 0))],
        out_shape=[jax.ShapeDtypeStruct((t, D_MODEL), F32),
                   jax.ShapeDtypeStruct((t, D_MODEL), BF16),
                   jax.ShapeDtypeStruct((SUB_ROWS, t, SUB_WORDS), U32)],
        compiler_params=_cparams("parallel"),
        name="merge_ln",
    )(proj, ya, yp, yl, x, p["b_gate"], p["w_branch"], p["w_out"], p["ln1_g"], p["ln1_b"])


def _router_kernel(x_ref, rw_ref, rb_ref, idx_ref, gate_ref, rank_ref, cnt_ref, carry):
    @pl.when(pl.program_id(0) == 0)
    def _():
        carry[...] = jnp.zeros_like(carry)

    tm = x_ref.shape[0]
    logits = lax.dot_general(rw_ref[...], x_ref[...], (((1,), (1,)), ((), ())),
                             preferred_element_type=F32)
    scores = jax.nn.sigmoid(logits)
    sel = scores + rb_ref[...]
    expert = lax.broadcasted_iota(I32, scores.shape, 0)
    idxs, tops = [], []
    chosen = jnp.zeros(scores.shape, F32)
    for _ in range(TOP_K):
        best = jnp.max(sel, axis=0, keepdims=True)
        ik = jnp.min(jnp.where(sel == best, expert, N_EXPERTS), axis=0, keepdims=True)
        hit = expert == ik
        tops.append(jnp.sum(jnp.where(hit, scores, 0.0), axis=0, keepdims=True))
        idxs.append(ik)
        chosen = jnp.where(hit, 1.0, chosen)
        sel = jnp.where(hit, -jnp.inf, sel)
    top = jnp.concatenate(tops, axis=0)
    idx = jnp.concatenate(idxs, axis=0)
    gates = top / jnp.sum(top, axis=0, keepdims=True) * ROUTED_SCALE
    gate_ref[...] = jnp.transpose(gates)
    idx_ref[...] = idx
    earlier = (lax.broadcasted_iota(I32, (tm, tm), 0) < lax.broadcasted_iota(I32, (tm, tm), 1))
    before = jnp.dot(chosen.astype(BF16), jnp.where(earlier, 1.0, 0.0).astype(BF16),
                     preferred_element_type=F32)
    base = before + carry[:, 0:1]
    ranks = [jnp.sum(jnp.where(expert == idxs[k], base, 0.0), axis=0, keepdims=True)
             for k in range(TOP_K)]
    rank_ref[...] = jnp.concatenate(ranks, axis=0).astype(I32)
    carry[...] = carry[...] + jnp.sum(chosen, axis=1, keepdims=True)
    cnt_ref[...] = carry[...].astype(I32)


def _router(x, rw_t, rbias):
    t = x.shape[0]
    tm = min(ROUTER_TM, t)
    tok = lambda i: (0, i)
    return pl.pallas_call(
        _router_kernel,
        grid=(t // tm,),
        in_specs=[pl.BlockSpec((tm, D_MODEL), lambda i: (i, 0)),
                  pl.BlockSpec((N_EXPERTS, D_MODEL), lambda i: (0, 0)),
                  pl.BlockSpec((N_EXPERTS, 1), lambda i: (0, 0))],
        out_specs=[pl.BlockSpec((TOP_K, tm), tok), pl.BlockSpec((tm, TOP_K), lambda i: (i, 0)),
                   pl.BlockSpec((TOP_K, tm), tok),
                   pl.BlockSpec((N_EXPERTS, LANES), lambda i: (0, 0))],
        out_shape=[jax.ShapeDtypeStruct((TOP_K, t), I32), jax.ShapeDtypeStruct((t, TOP_K), F32),
                   jax.ShapeDtypeStruct((TOP_K, t), I32),
                   jax.ShapeDtypeStruct((N_EXPERTS, LANES), I32)],
        scratch_shapes=[pltpu.VMEM((N_EXPERTS, LANES), F32)],
        compiler_params=_cparams("arbitrary"),
        name="router_topk",
    )(x, rw_t, rbias)


def _dest_kernel(start_ref, idx_ref, rank_ref, ids_ref, *, n_rows):
    idx = idx_ref[...]
    dest = rank_ref[...] + pl.program_id(0) * n_rows
    for e in range(N_EXPERTS):
        dest = dest + jnp.where(idx == e, start_ref[e], 0)
    ids_ref[...] = dest


def _piece_ids(pad_start, idx, rank, n_rows):
    t = idx.shape[1]
    tm = min(DEST_TM, t)
    spec = pl.BlockSpec((TOP_K, tm), lambda j, i, s: (0, i))
    return pl.pallas_call(
        functools.partial(_dest_kernel, n_rows=n_rows),
        grid_spec=pltpu.PrefetchScalarGridSpec(
            num_scalar_prefetch=1, grid=(SUB_ROWS, t // tm), in_specs=[spec, spec],
            out_specs=pl.BlockSpec((TOP_K, tm), lambda j, i, s: (j, i))),
        out_shape=jax.ShapeDtypeStruct((SUB_ROWS * TOP_K, t), I32),
        compiler_params=_cparams("parallel", "parallel"),
        name="piece_ids",
    )(pad_start, idx, rank)


def _swiglu(xb, w1_ref, w3_ref, w2_ref):
    a = jnp.dot(xb, w1_ref[...], preferred_element_type=F32)
    g = jnp.dot(xb, w3_ref[...], preferred_element_type=F32)
    hid = (a * jax.nn.sigmoid(a) * g).astype(BF16)
    return jnp.dot(hid, w2_ref[...], preferred_element_type=F32)


def _sc_mesh():
    return plsc.VectorSubcoreMesh(core_axis_name="core", subcore_axis_name="subcore")


def _sc_dispatch(xp, ids, n_rows):
    t = xp.shape[1]
    per_plane = t // SC_WINDOW

    @pl.kernel(out_type=jax.ShapeDtypeStruct((SUB_ROWS * n_rows, SUB_WORDS), U32), mesh=_sc_mesh(),
               scratch_types=[])
    def scatter(x_hbm, i_hbm, o_hbm):
        def body(x_vmem, i_vmem):
            for k in range(TOP_K):
                pltpu.sync_copy(x_vmem, o_hbm.at[i_vmem.at[k]])

        pltpu.emit_pipeline(
            body,
            grid=(SUB_ROWS * per_plane,),
            in_specs=[pl.BlockSpec((SC_WINDOW, SUB_WORDS), lambda w: (w, 0)),
                      pl.BlockSpec((TOP_K, SC_WINDOW), lambda w: (w // per_plane, w % per_plane))],
            out_specs=[],
            core_axis_name=("core", "subcore"),
            dimension_semantics=(pltpu.PARALLEL,),
        )(x_hbm, i_hbm)

    out = scatter(xp.reshape(SUB_ROWS * t, SUB_WORDS), ids)
    return out.reshape(SUB_ROWS, n_rows, SUB_WORDS)


def _sc_gather(ys, ids):
    n_rows = ys.shape[1]
    t = ids.shape[1]
    per_row = t // SC_WINDOW
    pieces = SUB_ROWS * TOP_K * t

    @pl.kernel(out_type=jax.ShapeDtypeStruct((pieces, SUB_WORDS), U32), mesh=_sc_mesh(),
               scratch_types=[])
    def gather(y_hbm, i_hbm, o_hbm):
        def body(i_vmem, o_vmem):
            pltpu.sync_copy(y_hbm.at[i_vmem.at[0]], o_vmem)

        pltpu.emit_pipeline(
            body,
            grid=(pieces // SC_WINDOW,),
            in_specs=[pl.BlockSpec((1, SC_WINDOW), lambda w: (w // per_row, w % per_row))],
            out_specs=[pl.BlockSpec((SC_WINDOW, SUB_WORDS), lambda w: (w, 0))],
            core_axis_name=("core", "subcore"),
            dimension_semantics=(pltpu.PARALLEL,),
        )(i_hbm, o_hbm)

    out = gather(ys.reshape(SUB_ROWS * n_rows, SUB_WORDS), ids)
    return out.reshape(SUB_ROWS, TOP_K * t, SUB_WORDS)


def _expert_kernel(bexp_ref, blive_ref, nused_ref, bfirst_ref, bslot_ref, bnext_ref, xs_ref, w1_hbm,
                   w3_hbm, w2_hbm, ys_ref, st1, st3, st2, w1b, w3b, w2b, xb, sem, *, layer):
    b = pl.program_id(0)
    used = b < nused_ref[0]

    def weight_copies(e, slot):
        return (pltpu.make_async_copy(w1_hbm.at[layer, e], st1.at[slot], sem.at[slot, 0]),
                pltpu.make_async_copy(w3_hbm.at[layer, e], st3.at[slot], sem.at[slot, 1]),
                pltpu.make_async_copy(w2_hbm.at[layer, e], st2.at[slot], sem.at[slot, 2]))

    @pl.when(b == 0)
    def _():
        for cp in weight_copies(bexp_ref[0], 0):
            cp.start()

    @pl.when(jnp.logical_and(used, bfirst_ref[b] == 1))
    def _():
        slot = bslot_ref[b]
        for cp in weight_copies(bexp_ref[b], slot):
            cp.wait()

        @pl.when(bnext_ref[b] >= 0)
        def _():
            for cp in weight_copies(bnext_ref[b], 1 - slot):
                cp.start()

        w1b[...] = st1[slot].astype(BF16)
        w3b[...] = st3[slot].astype(BF16)
        w2b[...] = st2[slot].astype(BF16)

    @pl.when(used)
    def _():
        live = lax.broadcasted_iota(I32, (xb.shape[0], SUB_WORDS), 0) < blive_ref[b]
        for j in range(SUB_ROWS):
            lo, hi = _unpack_rows(jnp.where(live, xs_ref[j], jnp.uint32(0)))
            lo_cols, hi_cols = _piece_cols(j)
            xb[:, lo_cols] = lo.astype(BF16)
            xb[:, hi_cols] = hi.astype(BF16)
        _store_pieces(ys_ref, _swiglu(xb[...], w1b, w3b, w2b))

    @pl.when(jnp.logical_not(used))
    def _():
        ys_ref[...] = jnp.zeros_like(ys_ref)


def _experts(r, xs, w1, w3, w2, layer):
    n_blocks = xs.shape[1] // EXPERT_BM
    n_tables = 6

    def rows_spec(index):
        return pl.BlockSpec((SUB_ROWS, EXPERT_BM, SUB_WORDS), lambda b, *tables: (0, index(b, tables), 0))

    grid_spec = pltpu.PrefetchScalarGridSpec(
        num_scalar_prefetch=n_tables,
        grid=(n_blocks,),
        in_specs=[rows_spec(lambda b, tables: jnp.minimum(b, tables[2][0] - 1)),
                  pl.BlockSpec(memory_space=pl.ANY), pl.BlockSpec(memory_space=pl.ANY),
                  pl.BlockSpec(memory_space=pl.ANY)],
        out_specs=rows_spec(lambda b, tables: b),
        scratch_shapes=[pltpu.VMEM((2, D_MODEL, D_EXPERT), F32), pltpu.VMEM((2, D_MODEL, D_EXPERT), F32),
                        pltpu.VMEM((2, D_EXPERT, D_MODEL), F32),
                        pltpu.VMEM((D_MODEL, D_EXPERT), BF16), pltpu.VMEM((D_MODEL, D_EXPERT), BF16),
                        pltpu.VMEM((D_EXPERT, D_MODEL), BF16), pltpu.VMEM((EXPERT_BM, D_MODEL), BF16),
                        pltpu.SemaphoreType.DMA((2, 3))],
    )
    return pl.pallas_call(
        functools.partial(_expert_kernel, layer=layer),
        grid_spec=grid_spec,
        out_shape=jax.ShapeDtypeStruct(xs.shape, U32),
        compiler_params=_cparams("arbitrary"),
        name="moe_experts",
    )(r["block_exp"], r["block_live"], r["n_used"], r["block_first"], r["block_slot"], r["block_next"],
      xs, w1, w3, w2)


def _combine_kernel(gate_ref, x_ref, xb_ref, s1_ref, s3_ref, s2_ref, g_ref, b_ref, *refs):
    yk_refs = refs[:TOP_K]
    xo_ref, xbo_ref, hbuf = refs[-3:]
    tm = x_ref.shape[0]
    shared = _swiglu(xb_ref[...], s1_ref, s3_ref, s2_ref)
    gates = [jnp.broadcast_to(gate_ref[:, k:k + 1], (tm, SUB_WORDS)) for k in range(TOP_K)]
    for j in range(SUB_ROWS):
        lo_cols, hi_cols = _piece_cols(j)
        acc_lo = shared[:, lo_cols]
        acc_hi = shared[:, hi_cols]
        for k in range(TOP_K):
            lo, hi = _unpack_rows(yk_refs[k][j])
            acc_lo = acc_lo + gates[k] * lo
            acc_hi = acc_hi + gates[k] * hi
        hbuf[:, lo_cols] = acc_lo
        hbuf[:, hi_cols] = acc_hi
    y = _layer_norm(DEEPNORM_ALPHA * x_ref[...] + hbuf[...], g_ref[...], b_ref[...])
    xo_ref[...] = y
    xbo_ref[...] = y.astype(BF16)


def _combine(gate_t, x, xb, yg, p, out_rows=None, out_row0=0, out_buf=None):
    t = x.shape[0]
    tm = min(COMBINE_TM, t)
    steps = t // tm
    row = lambda i: (i, 0)
    assert out_row0 % tm == 0

    def slot_spec(k):
        return pl.BlockSpec((SUB_ROWS, tm, SUB_WORDS), lambda i: (0, k * steps + i, 0))

    in_specs = [pl.BlockSpec((tm, TOP_K), row),
                pl.BlockSpec((tm, D_MODEL), row),
                pl.BlockSpec((tm, D_MODEL), row),
                _resident((D_MODEL, D_EXPERT), lambda i: (0, 0)),
                _resident((D_MODEL, D_EXPERT), lambda i: (0, 0)),
                _resident((D_EXPERT, D_MODEL), lambda i: (0, 0)),
                _resident((1, D_MODEL), lambda i: (0, 0)),
                _resident((1, D_MODEL), lambda i: (0, 0))] + [slot_spec(k) for k in range(TOP_K)]
    args = [gate_t, x, xb, p["sh_w1"], p["sh_w3"], p["sh_w2"], p["ln2_g"], p["ln2_b"]] + [yg] * TOP_K
    aliases = {}
    if out_buf is not None:
        aliases = {len(args): 0}
        in_specs.append(pl.BlockSpec(memory_space=pl.ANY))
        args.append(out_buf)
    return pl.pallas_call(
        _combine_kernel,
        grid=(steps,),
        in_specs=in_specs,
        out_specs=[pl.BlockSpec((tm, D_MODEL), lambda i: (i + out_row0 // tm, 0)),
                   pl.BlockSpec((tm, D_MODEL), row)],
        out_shape=[jax.ShapeDtypeStruct((out_rows or t, D_MODEL), F32),
                   jax.ShapeDtypeStruct((t, D_MODEL), BF16)],
        scratch_shapes=[pltpu.VMEM((tm, D_MODEL), F32)],
        input_output_aliases=aliases,
        compiler_params=_cparams("parallel"),
        name="moe_combine_ln",
    )(*args)


def _n_expert_blocks(t):
    return -(-(t * TOP_K + N_EXPERTS * (EXPERT_BM - 1)) // EXPERT_BM)


def _route(xb, p):
    t = xb.shape[0]
    n_blocks = _n_expert_blocks(t)
    idx, gate_t, rank, cnt = _router(xb, p["router_w_t"], p["router_bias"])
    counts = cnt[:, 0]
    padded = (counts + EXPERT_BM - 1) // EXPERT_BM * EXPERT_BM
    pad_end = jnp.cumsum(padded)
    pad_start = pad_end - padded
    n_used = (pad_end[-1:] // EXPERT_BM).astype(I32)
    block = jnp.arange(n_blocks, dtype=I32)
    block_exp = jnp.minimum(jnp.sum(pad_end[None, :] <= block[:, None] * EXPERT_BM, axis=1),
                            N_EXPERTS - 1).astype(I32)
    expert = jnp.arange(N_EXPERTS, dtype=I32)
    of_block = block_exp[:, None] == expert[None, :]

    def per_block(table):
        return jnp.sum(jnp.where(of_block, table[None, :], 0), axis=1).astype(I32)

    block_live = jnp.clip(per_block(counts) - (block * EXPERT_BM - per_block(pad_start)), 0, EXPERT_BM)
    active = counts > 0
    ordinal = jnp.cumsum(active.astype(I32)) - 1
    later = active[None, :] & (expert[None, :] > expert[:, None])
    next_active = jnp.min(jnp.where(later, expert[None, :], N_EXPERTS), axis=1)
    next_active = jnp.where(next_active < N_EXPERTS, next_active, -1)
    used = block < n_used[0]
    first = used & (block * EXPERT_BM == per_block(pad_start))
    n_rows = n_blocks * EXPERT_BM
    ids = _piece_ids(pad_start.astype(I32), idx, rank, n_rows)
    return dict(ids=ids, gate_t=gate_t, block_exp=block_exp, block_live=block_live.astype(I32),
                n_used=n_used, block_first=first.astype(I32), block_slot=per_block(ordinal) % 2,
                block_next=per_block(next_active), n_rows=n_rows)


def _layer_params(l, b_gate, sinks, pool_w, pool_scale, conv_w, conv_b, lru_wa, lru_ba, lru_wx,
                  lru_bx, lru_lambda, w_branch, w_out, ln1_g, ln1_b, router_w, router_bias,
                  sh_w1, sh_w3, sh_w2, ln2_g, ln2_b):
    row = lambda a: a[l].reshape(1, -1).astype(F32)
    return dict(
        b_gate=row(b_gate), sinks=sinks[l].astype(F32),
        pool_w=pool_w[l].astype(BF16), pool_scale=row(pool_scale),
        conv_w=conv_w[l].astype(F32), conv_b=row(conv_b),
        lru_wa=_block_diag(lru_wa[l]).astype(BF16), lru_ba=row(lru_ba),
        lru_wx=_block_diag(lru_wx[l]).astype(BF16), lru_bx=row(lru_bx),
        lru_lambda=row(lru_lambda),
        w_branch=w_branch[l].astype(BF16), w_out=w_out[l].astype(BF16),
        ln1_g=row(ln1_g), ln1_b=row(ln1_b),
        router_w_t=router_w[l].T.astype(BF16), router_bias=router_bias[l].reshape(-1, 1).astype(F32),
        sh_w1=sh_w1[l].astype(BF16), sh_w3=sh_w3[l].astype(BF16), sh_w2=sh_w2[l].astype(BF16),
        ln2_g=row(ln2_g), ln2_b=row(ln2_b),
    )


def kernel(x, positions, w_in, b_gate, sinks, pool_w, pool_scale, conv_w, conv_b, lru_wa, lru_ba, lru_wx, lru_bx, lru_lambda, w_branch, w_out, ln1_g, ln1_b, router_w, router_bias, exp_w1, exp_w3, exp_w2, sh_w1, sh_w3, sh_w2, ln2_g, ln2_b):
    batch, seq_len, d = x.shape
    assert d == D_MODEL and seq_len % max(ATTN_TQ, MIX_TB) == 0 and seq_len % min(PROJ_TM, seq_len) == 0
    t = batch * seq_len
    cos_all, sin_all = _rope_tables(positions)
    x_all = x.reshape(t, d).astype(F32)
    xb_all = x_all.astype(BF16)
    seqs = []
    for s in range(batch):
        rows = slice(s * seq_len, (s + 1) * seq_len)
        seqs.append(dict(xf=x_all, xb=xb_all, row0=s * seq_len, cos=cos_all[rows], sin=sin_all[rows]))
    for l in range(DEPTH):
        p = _layer_params(l, b_gate, sinks, pool_w, pool_scale, conv_w, conv_b, lru_wa, lru_ba,
                          lru_wx, lru_bx, lru_lambda, w_branch, w_out, ln1_g, ln1_b, router_w,
                          router_bias, sh_w1, sh_w3, sh_w2, ln2_g, ln2_b)
        for sq in seqs:
            proj = _in_proj(sq["xb"], sq["row0"], seq_len, w_in, l)
            ya = _attention(proj, sq["cos"], sq["sin"], p["sinks"], seq_len)
            yp, yl = _mixers(proj, p, 1, seq_len)
            sq["xf"], sq["xb"], xp = _merge(proj, ya, yp, yl, sq["xf"], sq["row0"], p)
            sq["row0"] = 0
            sq["route"] = _route(sq["xb"], p)
            sq["xs"] = _sc_dispatch(xp, sq["route"]["ids"], sq["route"]["n_rows"])
        for sq in seqs:
            ys = _experts(sq["route"], sq["xs"], exp_w1, exp_w3, exp_w2, l)
            sq["yg"] = _sc_gather(ys, sq["route"]["ids"])
        out = None
        for s, sq in enumerate(seqs):
            if l + 1 < DEPTH:
                sq["xf"], sq["xb"] = _combine(sq["route"]["gate_t"], sq["xf"], sq["xb"], sq["yg"], p)
            else:
                out, _ = _combine(sq["route"]["gate_t"], sq["xf"], sq["xb"], sq["yg"], p,
                                  out_rows=t, out_row0=s * seq_len, out_buf=out)
    return out.reshape(batch, seq_len, d).astype(x.dtype)
```

```python
import functools

import jax
import jax.numpy as jnp
from jax import lax
from jax.experimental import pallas as pl
from jax.experimental.pallas import tpu as pltpu
from jax.experimental.pallas import tpu_sc as plsc

F32 = jnp.float32
BF16 = jnp.bfloat16
I32 = jnp.int32
U32 = jnp.uint32

D_MODEL = 2048
DEPTH = 2
CHUNK = 64
HEAD_DIM = 64
HALF_HEAD = HEAD_DIM // 2
N_HEADS = 16
N_KV_HEADS = 4
KV_GROUP = N_HEADS // N_KV_HEADS
ATTN_WIDTH = N_HEADS * HEAD_DIM
KV_WIDTH = N_KV_HEADS * HEAD_DIM
WINDOW_CHUNKS = 2
ROPE_THETA = 10000.0
POOL_WINDOWS = (2, 4, 8, 16)
POOL_WIDTH = 1024
POOL_GROUP = POOL_WIDTH // len(POOL_WINDOWS)
LRU_WIDTH = 1024
LRU_BLOCKS = 16
LRU_BLOCK = LRU_WIDTH // LRU_BLOCKS
CONV_WIDTH = 4
LRU_C = 8.0
N_BRANCH = 3
GATE_WIDTH = N_BRANCH * D_MODEL
IN_WIDTH = ATTN_WIDTH + 2 * KV_WIDTH + POOL_WIDTH + 2 * LRU_WIDTH + GATE_WIDTH
N_EXPERTS = 64
TOP_K = 8
D_EXPERT = 512
ROUTED_SCALE = 2.5
LN_EPS = 1e-5
DEEPNORM_ALPHA = (2 * DEPTH) ** 0.25

COL_GATE = 0
COL_Q = GATE_WIDTH
COL_POOL = COL_Q + ATTN_WIDTH
COL_LRU = COL_POOL + POOL_WIDTH
COL_GELU = COL_LRU + LRU_WIDTH
COL_K = COL_GELU + LRU_WIDTH
COL_V = COL_K + KV_WIDTH

LANES = 128
SUBLANES = 8
MXU_DIM = 256
VMEM_LIMIT_BYTES = 56 * 1024 * 1024

PROJ_TM = 1024
PROJ_TN = 1536
ATTN_TQ = 256
ATTN_HALO = WINDOW_CHUNKS * CHUNK
MIX_TB = 256
POOL_HALO = 16
CONV_HALO = 8
MERGE_TM = 256
ROUTER_TM = 512
DEST_TM = 2048
EXPERT_BM = 512
COMBINE_TM = 256
PROJ_SRC = 512
NEG_BIG = -1e30

SUB_WORDS = 256
SUB_COLS = 2 * SUB_WORDS
SUB_ROWS = D_MODEL // SUB_COLS
SC_WINDOW = 128


def _cparams(*sem):
    return pltpu.CompilerParams(dimension_semantics=sem, vmem_limit_bytes=VMEM_LIMIT_BYTES)


def _resident(shape, index_map):
    return pl.BlockSpec(shape, index_map, pipeline_mode=pl.Buffered(1))


def _pack_rows(lo, hi):
    lo_u = lax.bitcast_convert_type(lo.astype(BF16).astype(F32), U32)
    hi_u = lax.bitcast_convert_type(hi.astype(BF16).astype(F32), U32)
    return (lo_u >> 16) | (hi_u & jnp.uint32(0xFFFF0000))


def _unpack_rows(w):
    lo = lax.bitcast_convert_type(w << 16, F32)
    hi = lax.bitcast_convert_type(w & jnp.uint32(0xFFFF0000), F32)
    return lo, hi


def _piece_cols(j):
    return (slice(j * SUB_COLS, j * SUB_COLS + SUB_WORDS),
            slice(j * SUB_COLS + SUB_WORDS, (j + 1) * SUB_COLS))


def _store_pieces(ref, y):
    for j in range(SUB_ROWS):
        lo_cols, hi_cols = _piece_cols(j)
        ref[j] = _pack_rows(y[:, lo_cols], y[:, hi_cols])


def _layer_norm(y, g, b):
    mu = jnp.mean(y, axis=-1, keepdims=True)
    d = y - mu
    var = jnp.mean(d * d, axis=-1, keepdims=True)
    return d * lax.rsqrt(var + LN_EPS) * g + b


def _rope_kernel(pos_ref, inv_ref, cos_ref, sin_ref):
    ang = pos_ref[...].astype(F32) * inv_ref[...]
    lane = lax.broadcasted_iota(I32, ang.shape, 1)
    first_half = (lane % HEAD_DIM) < HALF_HEAD
    cos_ref[...] = jnp.cos(ang)
    s = jnp.sin(ang)
    sin_ref[...] = jnp.where(first_half, -s, s)


def _rope_tables(positions):
    t = positions.size
    tm = min(1024, t)
    inv_freq = 1.0 / (ROPE_THETA ** (jnp.arange(0, HEAD_DIM, 2, dtype=F32) / HEAD_DIM))
    inv = jnp.tile(inv_freq, LANES // HALF_HEAD).reshape(1, LANES)
    return pl.pallas_call(
        _rope_kernel,
        grid=(t // tm,),
        in_specs=[pl.BlockSpec((tm, 1), lambda i: (i, 0)),
                  pl.BlockSpec((1, LANES), lambda i: (0, 0))],
        out_specs=[pl.BlockSpec((tm, LANES), lambda i: (i, 0))] * 2,
        out_shape=[jax.ShapeDtypeStruct((t, LANES), F32)] * 2,
        compiler_params=_cparams("parallel"),
        name="rope_tables",
    )(positions.reshape(t, 1), inv)


def _proj_source_blocks():
    ref_order = (("q", ATTN_WIDTH), ("k", KV_WIDTH), ("v", KV_WIDTH), ("pool", POOL_WIDTH),
                 ("lru", LRU_WIDTH), ("gelu", LRU_WIDTH), ("gate", GATE_WIDTH))
    start, col = {}, 0
    for name, width in ref_order:
        start[name] = col
        col += width
    assert start["v"] == start["k"] + KV_WIDTH and 2 * KV_WIDTH == PROJ_SRC
    out_order = (("gate", GATE_WIDTH), ("q", ATTN_WIDTH), ("pool", POOL_WIDTH), ("lru", LRU_WIDTH),
                 ("gelu", LRU_WIDTH), ("k", 2 * KV_WIDTH))
    blocks = []
    for name, width in out_order:
        assert start[name] % PROJ_SRC == 0 and width % PROJ_SRC == 0
        blocks += [start[name] // PROJ_SRC + c for c in range(width // PROJ_SRC)]
    return blocks


def _in_proj_kernel(src_ref, x_ref, *refs):
    w_refs, o_ref, wb = refs[:-2], refs[-2], refs[-1]

    @pl.when(pl.program_id(1) == 0)
    def _():
        for c, w_ref in enumerate(w_refs):
            wb[:, c * PROJ_SRC:(c + 1) * PROJ_SRC] = w_ref[0].astype(BF16)

    o_ref[...] = jnp.dot(x_ref[...], wb[...], preferred_element_type=F32).astype(o_ref.dtype)


def _in_proj(xb, xb_row0, rows, w_in, layer):
    d = xb.shape[1]
    n = w_in.shape[2]
    tm = min(PROJ_TM, rows)
    assert xb_row0 % tm == 0
    per = PROJ_TN // PROJ_SRC
    src = jnp.asarray(_proj_source_blocks(), I32)

    def w_spec(c):
        return pl.BlockSpec((1, d, PROJ_SRC), lambda j, i, s: (layer, 0, s[per * j + c]))

    grid_spec = pltpu.PrefetchScalarGridSpec(
        num_scalar_prefetch=1,
        grid=(n // PROJ_TN, rows // tm),
        in_specs=[pl.BlockSpec((tm, d), lambda j, i, s: (i + xb_row0 // tm, 0))]
        + [w_spec(c) for c in range(per)],
        out_specs=pl.BlockSpec((tm, PROJ_TN), lambda j, i, s: (i, j)),
        scratch_shapes=[pltpu.VMEM((d, PROJ_TN), BF16)],
    )
    return pl.pallas_call(
        _in_proj_kernel,
        grid_spec=grid_spec,
        out_shape=jax.ShapeDtypeStruct((rows, n), BF16),
        compiler_params=_cparams("arbitrary", "arbitrary"),
        name="in_proj",
    )(src, xb, *([w_in] * per))


def _rope(x, cos, sin):
    w = x.shape[1]
    reps = w // LANES
    c = jnp.concatenate([cos] * reps, axis=1)
    s = jnp.concatenate([sin] * reps, axis=1)
    lane = lax.broadcasted_iota(I32, x.shape, 1)
    first_half = (lane % HEAD_DIM) < HALF_HEAD
    rot = jnp.where(first_half, pltpu.roll(x, w - HALF_HEAD, 1), pltpu.roll(x, HALF_HEAD, 1))
    return x * c + rot * s


def _attn_kernel(q_ref, k_ref, v_ref, kh_ref, vh_ref, cq_ref, sq_ref, ch_ref, sh_ref, sink_ref,
                 o_ref, s_buf, p_buf, r_buf, *, steps_per_seq):
    first = (pl.program_id(0) % steps_per_seq) == 0
    q = _rope(q_ref[...].astype(F32), cq_ref[...], sq_ref[...]) * (HEAD_DIM ** -0.5)
    q = q.astype(BF16)
    k_own = _rope(k_ref[...].astype(F32), cq_ref[...], sq_ref[...])
    k_halo = _rope(kh_ref[...].astype(F32), ch_ref[...], sh_ref[...])
    k_all = jnp.concatenate([k_halo, k_own], axis=0).astype(BF16)
    v_all = jnp.concatenate([vh_ref[...], v_ref[...]], axis=0)
    span = (WINDOW_CHUNKS + 1) * CHUNK
    rows = KV_GROUP * CHUNK
    chunk_rows = N_KV_HEADS * rows
    n_chunks = q.shape[0] // CHUNK
    win_chunk = lax.broadcasted_iota(I32, (rows, span), 1) // CHUNK
    oldest = jnp.where(first, 0, -WINDOW_CHUNKS)

    def tile_rows(c, g):
        return slice(c * chunk_rows + g * rows, c * chunk_rows + (g + 1) * rows)

    def kv_cols(g):
        return slice(g * HEAD_DIM, (g + 1) * HEAD_DIM)

    for c in range(n_chunks):
        q_c = q[c * CHUNK:(c + 1) * CHUNK, :]
        k_win = k_all[c * CHUNK:c * CHUNK + span, :]
        for g in range(N_KV_HEADS):
            qs = jnp.concatenate(
                [q_c[:, (g * KV_GROUP + i) * HEAD_DIM:(g * KV_GROUP + i + 1) * HEAD_DIM]
                 for i in range(KV_GROUP)], axis=0)
            s = lax.dot_general(qs, k_win[:, kv_cols(g)], (((1,), (1,)), ((), ())),
                                preferred_element_type=F32)
            if c < WINDOW_CHUNKS:
                s = jnp.where(win_chunk + (c - WINDOW_CHUNKS) >= oldest, s, NEG_BIG)
            s_buf[tile_rows(c, g), :] = s
    sink = sink_ref[...]
    for c in range(n_chunks):
        blk = slice(c * chunk_rows, (c + 1) * chunk_rows)
        s = s_buf[blk, :]
        m = jnp.maximum(jnp.max(s, axis=1, keepdims=True), sink)
        p = jnp.exp(s - m)
        p_buf[blk, :] = p.astype(BF16)
        r_buf[blk, :] = 1.0 / (jnp.sum(p, axis=1, keepdims=True) + jnp.exp(sink - m))
    for c in range(n_chunks):
        v_win = v_all[c * CHUNK:c * CHUNK + span, :]
        heads = []
        for g in range(N_KV_HEADS):
            o = jnp.dot(p_buf[tile_rows(c, g), :], v_win[:, kv_cols(g)], preferred_element_type=F32)
            o = o * r_buf[tile_rows(c, g), :]
            heads += [o[i * CHUNK:(i + 1) * CHUNK, :] for i in range(KV_GROUP)]
        o_ref[c * CHUNK:(c + 1) * CHUNK, :] = jnp.concatenate(heads, axis=1).astype(o_ref.dtype)


def _attention(proj, cos, sin, sinks, seq_len):
    t = proj.shape[0]
    tq = ATTN_TQ
    steps_per_seq = seq_len // tq
    halo_per_tq = tq // ATTN_HALO
    q_blk = COL_Q // ATTN_WIDTH
    k_blk = COL_K // KV_WIDTH
    v_blk = COL_V // KV_WIDTH

    def halo_row(i):
        return jnp.maximum(i * halo_per_tq - 1, 0)

    span = (WINDOW_CHUNKS + 1) * CHUNK
    tile_rows = (tq // CHUNK) * N_HEADS * CHUNK
    sink_rows = jnp.repeat(sinks, CHUNK).reshape(N_HEADS * CHUNK, 1)
    return pl.pallas_call(
        functools.partial(_attn_kernel, steps_per_seq=steps_per_seq),
        grid=(t // tq,),
        in_specs=[
            pl.BlockSpec((tq, ATTN_WIDTH), lambda i: (i, q_blk)),
            pl.BlockSpec((tq, KV_WIDTH), lambda i: (i, k_blk)),
            pl.BlockSpec((tq, KV_WIDTH), lambda i: (i, v_blk)),
            pl.BlockSpec((ATTN_HALO, KV_WIDTH), lambda i: (halo_row(i), k_blk)),
            pl.BlockSpec((ATTN_HALO, KV_WIDTH), lambda i: (halo_row(i), v_blk)),
            pl.BlockSpec((tq, LANES), lambda i: (i, 0)),
            pl.BlockSpec((tq, LANES), lambda i: (i, 0)),
            pl.BlockSpec((ATTN_HALO, LANES), lambda i: (halo_row(i), 0)),
            pl.BlockSpec((ATTN_HALO, LANES), lambda i: (halo_row(i), 0)),
            pl.BlockSpec((N_HEADS * CHUNK, 1), lambda i: (0, 0)),
        ],
        out_specs=pl.BlockSpec((tq, ATTN_WIDTH), lambda i: (i, 0)),
        out_shape=jax.ShapeDtypeStruct((t, ATTN_WIDTH), BF16),
        scratch_shapes=[pltpu.VMEM((tile_rows, span), F32), pltpu.VMEM((tile_rows, span), BF16),
                        pltpu.VMEM((tile_rows, 1), F32)],
        compiler_params=_cparams("parallel"),
        name="swa_attention",
    )(proj, proj, proj, proj, proj, cos, sin, cos, sin, sink_rows)


def _mix_kernel(up_ref, ul_ref, ug_ref, pw_ref, ps_ref, cw_ref, cb_ref, wa_ref, ba_ref, wx_ref,
                bx_ref, lam_ref, yp_ref, yl_ref, halo_p, halo_l, h_carry):
    s = pl.program_id(1)
    tb = up_ref.shape[0]

    @pl.when(s == 0)
    def _():
        halo_p[...] = jnp.zeros_like(halo_p)
        halo_l[...] = jnp.zeros_like(halo_l)
        h_carry[...] = jnp.zeros_like(h_carry)

    u = up_ref[...].astype(F32)
    ext = jnp.concatenate([halo_p[...], u], axis=0)
    sums = []
    ws = ext
    for g, win in enumerate(POOL_WINDOWS):
        shift = win // 2
        ws = ws + pltpu.roll(ws, shift, 0)
        sums.append(ws[POOL_HALO:, :POOL_GROUP])
        if g + 1 < len(POOL_WINDOWS):
            ws = ws[:, POOL_GROUP:]
    t_idx = (s * tb + lax.broadcasted_iota(I32, (tb, 1), 0)).astype(F32)
    for g, win in enumerate(POOL_WINDOWS):
        cols = slice(g * POOL_GROUP, (g + 1) * POOL_GROUP)
        count = jnp.minimum(t_idx + 1.0, float(win))
        pooled = sums[g] / count - u[:, cols]
        y = jnp.dot(pooled.astype(BF16), pw_ref[g], preferred_element_type=F32)
        yp_ref[:, cols] = (y * ps_ref[:, cols]).astype(yp_ref.dtype)
    halo_p[...] = u[tb - POOL_HALO:, :]

    ul = ul_ref[...].astype(F32)
    extl = jnp.concatenate([halo_l[...], ul], axis=0)
    xc = cb_ref[...] + cw_ref[CONV_WIDTH - 1:CONV_WIDTH, :] * ul
    for j in range(1, CONV_WIDTH):
        w_j = cw_ref[CONV_WIDTH - 1 - j:CONV_WIDTH - j, :]
        xc = xc + w_j * pltpu.roll(extl, j, 0)[CONV_HALO:, :]
    halo_l[...] = ul[tb - CONV_HALO:, :]
    xcb = xc.astype(BF16)
    r_parts, i_parts = [], []
    for j in range(LRU_WIDTH // MXU_DIM):
        cols = slice(j * MXU_DIM, (j + 1) * MXU_DIM)
        r_parts.append(jnp.dot(xcb[:, cols], wa_ref[j], preferred_element_type=F32))
        i_parts.append(jnp.dot(xcb[:, cols], wx_ref[j], preferred_element_type=F32))
    r = jax.nn.sigmoid(jnp.concatenate(r_parts, axis=1) + ba_ref[...])
    ig = jax.nn.sigmoid(jnp.concatenate(i_parts, axis=1) + bx_ref[...])
    z = -lam_ref[...]
    softplus = jnp.maximum(z, 0.0) + jnp.log1p(jnp.exp(-jnp.abs(z)))
    log_a = (-LRU_C) * r * softplus
    a = jnp.exp(log_a)
    one_minus_a2 = 1.0 - a * a
    root = jnp.where(one_minus_a2 > 0.0, one_minus_a2 * lax.rsqrt(one_minus_a2), 0.0)
    b = root * (ig * xc)
    row_in_group = lax.broadcasted_iota(I32, a.shape, 0) % SUBLANES
    d = 1
    while d < SUBLANES:
        keep = row_in_group >= d
        a_prev = jnp.where(keep, pltpu.roll(a, d, 0), 1.0)
        b_prev = jnp.where(keep, pltpu.roll(b, d, 0), 0.0)
        b = a * b_prev + b
        a = a * a_prev
        d *= 2
    gelu = jax.nn.gelu(ug_ref[...].astype(F32))
    h_prev = h_carry[...]
    pair = 2 * SUBLANES
    for g in range(tb // pair):
        hs = []
        for half in range(2):
            rows = slice(g * pair + half * SUBLANES, g * pair + (half + 1) * SUBLANES)
            h = a[rows, :] * h_prev + b[rows, :]
            h_prev = h[SUBLANES - 1:SUBLANES, :]
            hs.append(h)
        rows = slice(g * pair, (g + 1) * pair)
        yl_ref[rows, :] = (jnp.concatenate(hs, axis=0) * gelu[rows, :]).astype(yl_ref.dtype)
    h_carry[...] = h_prev


def _block_diag(w):
    per = MXU_DIM // LRU_BLOCK
    w = w.reshape(LRU_BLOCKS // per, per, LRU_BLOCK, LRU_BLOCK)
    eye = jnp.eye(per, dtype=w.dtype)
    out = jnp.einsum("gpij,pq->gpiqj", w, eye)
    return out.reshape(LRU_BLOCKS // per, MXU_DIM, MXU_DIM)


def _mixers(proj, p, batch, seq_len):
    t = proj.shape[0]
    tb = min(MIX_TB, seq_len)
    steps = seq_len // tb

    def rows(bi, si):
        return bi * steps + si

    def col_spec(col):
        blk = col // POOL_WIDTH
        return pl.BlockSpec((tb, POOL_WIDTH), lambda bi, si: (rows(bi, si), blk))

    def const(shape):
        nd = len(shape)
        return pl.BlockSpec(shape, lambda bi, si: (0,) * nd)

    out_spec = pl.BlockSpec((tb, POOL_WIDTH), lambda bi, si: (rows(bi, si), 0))
    return pl.pallas_call(
        _mix_kernel,
        grid=(batch, steps),
        in_specs=[col_spec(COL_POOL), col_spec(COL_LRU), col_spec(COL_GELU),
                  const((len(POOL_WINDOWS), POOL_GROUP, POOL_GROUP)), const((1, POOL_WIDTH)),
                  const((CONV_WIDTH, LRU_WIDTH)), const((1, LRU_WIDTH)),
                  const((LRU_WIDTH // MXU_DIM, MXU_DIM, MXU_DIM)), const((1, LRU_WIDTH)),
                  const((LRU_WIDTH // MXU_DIM, MXU_DIM, MXU_DIM)), const((1, LRU_WIDTH)),
                  const((1, LRU_WIDTH))],
        out_specs=[out_spec, out_spec],
        out_shape=[jax.ShapeDtypeStruct((t, POOL_WIDTH), BF16),
                   jax.ShapeDtypeStruct((t, LRU_WIDTH), BF16)],
        scratch_shapes=[pltpu.VMEM((POOL_HALO, POOL_WIDTH), F32),
                        pltpu.VMEM((CONV_HALO, LRU_WIDTH), F32),
                        pltpu.VMEM((1, LRU_WIDTH), F32)],
        compiler_params=_cparams("arbitrary", "arbitrary"),
        name="pool_rglru",
    )(proj, proj, proj, p["pool_w"], p["pool_scale"], p["conv_w"], p["conv_b"],
      p["lru_wa"], p["lru_ba"], p["lru_wx"], p["lru_bx"], p["lru_lambda"])


def _merge_kernel(gl_ref, ya_ref, yp_ref, yl_ref, x_ref, bg_ref, wb_ref, wo_ref, g_ref, b_ref,
                  xo_ref, xb_ref, xp_ref):
    merged = None
    for j, y_ref in enumerate((ya_ref, yp_ref, yl_ref)):
        cols = slice(j * D_MODEL, (j + 1) * D_MODEL)
        z = jnp.dot(y_ref[...], wb_ref[j], preferred_element_type=F32)
        gate = jax.nn.sigmoid(gl_ref[:, cols].astype(F32) + bg_ref[:, cols])
        merged = gate * z if merged is None else merged + gate * z
    h = jnp.dot(merged.astype(BF16), wo_ref[...], preferred_element_type=F32)
    y = _layer_norm(DEEPNORM_ALPHA * x_ref[...] + h, g_ref[...], b_ref[...])
    xo_ref[...] = y
    xb_ref[...] = y.astype(BF16)
    _store_pieces(xp_ref, y)


def _merge(proj, ya, yp, yl, x, x_row0, p):
    t = proj.shape[0]
    tm = min(MERGE_TM, t)
    assert x_row0 % tm == 0
    row = lambda i: (i, 0)
    return pl.pallas_call(
        _merge_kernel,
        grid=(t // tm,),
        in_specs=[pl.BlockSpec((tm, GATE_WIDTH), row),
                  pl.BlockSpec((tm, ATTN_WIDTH), row),
                  pl.BlockSpec((tm, POOL_WIDTH), row),
                  pl.BlockSpec((tm, LRU_WIDTH), row),
                  pl.BlockSpec((tm, D_MODEL), lambda i: (i + x_row0 // tm, 0)),
                  _resident((1, GATE_WIDTH), lambda i: (0, 0)),
                  _resident((N_BRANCH, ATTN_WIDTH, D_MODEL), lambda i: (0, 0, 0)),
                  _resident((D_MODEL, D_MODEL), lambda i: (0, 0)),
                  _resident((1, D_MODEL), lambda i: (0, 0)),
                  _resident((1, D_MODEL), lambda i: (0, 0))],
        out_specs=[pl.BlockSpec((tm, D_MODEL), row), pl.BlockSpec((tm, D_MODEL), row),
                   pl.BlockSpec((SUB_ROWS, tm, SUB_WORDS), lambda i: (0, i, 0))],
        out_shape=[jax.ShapeDtypeStruct((t, D_MODEL), F32),
                   jax.ShapeDtypeStruct((t, D_MODEL), BF16),
                   jax.ShapeDtypeStruct((SUB_ROWS, t, SUB_WORDS), U32)],
        compiler_params=_cparams("parallel"),
        name="merge_ln",
    )(proj, ya, yp, yl, x, p["b_gate"], p["w_branch"], p["w_out"], p["ln1_g"], p["ln1_b"])


def _router_kernel(x_ref, rw_ref, rb_ref, idx_ref, gate_ref, rank_ref, cnt_ref, carry):
    @pl.when(pl.program_id(0) == 0)
    def _():
        carry[...] = jnp.zeros_like(carry)

    tm = x_ref.shape[0]
    logits = lax.dot_general(rw_ref[...], x_ref[...], (((1,), (1,)), ((), ())),
                             preferred_element_type=F32)
    scores = jax.nn.sigmoid(logits)
    sel = scores + rb_ref[...]
    expert = lax.broadcasted_iota(I32, scores.shape, 0)
    idxs, tops = [], []
    chosen = jnp.zeros(scores.shape, F32)
    for _ in range(TOP_K):
        best = jnp.max(sel, axis=0, keepdims=True)
        ik = jnp.min(jnp.where(sel == best, expert, N_EXPERTS), axis=0, keepdims=True)
        hit = expert == ik
        tops.append(jnp.sum(jnp.where(hit, scores, 0.0), axis=0, keepdims=True))
        idxs.append(ik)
        chosen = jnp.where(hit, 1.0, chosen)
        sel = jnp.where(hit, -jnp.inf, sel)
    top = jnp.concatenate(tops, axis=0)
    idx = jnp.concatenate(idxs, axis=0)
    gates = top / jnp.sum(top, axis=0, keepdims=True) * ROUTED_SCALE
    gate_ref[...] = jnp.transpose(gates)
    idx_ref[...] = idx
    earlier = (lax.broadcasted_iota(I32, (tm, tm), 0) < lax.broadcasted_iota(I32, (tm, tm), 1))
    before = jnp.dot(chosen.astype(BF16), jnp.where(earlier, 1.0, 0.0).astype(BF16),
                     preferred_element_type=F32)
    base = before + carry[:, 0:1]
    ranks = [jnp.sum(jnp.where(expert == idxs[k], base, 0.0), axis=0, keepdims=True)
             for k in range(TOP_K)]
    rank_ref[...] = jnp.concatenate(ranks, axis=0).astype(I32)
    carry[...] = carry[...] + jnp.sum(chosen, axis=1, keepdims=True)
    cnt_ref[...] = carry[...].astype(I32)


def _router(x, rw_t, rbias):
    t = x.shape[0]
    tm = min(ROUTER_TM, t)
    tok = lambda i: (0, i)
    return pl.pallas_call(
        _router_kernel,
        grid=(t // tm,),
        in_specs=[pl.BlockSpec((tm, D_MODEL), lambda i: (i, 0)),
                  pl.BlockSpec((N_EXPERTS, D_MODEL), lambda i: (0, 0)),
                  pl.BlockSpec((N_EXPERTS, 1), lambda i: (0, 0))],
        out_specs=[pl.BlockSpec((TOP_K, tm), tok), pl.BlockSpec((tm, TOP_K), lambda i: (i, 0)),
                   pl.BlockSpec((TOP_K, tm), tok),
                   pl.BlockSpec((N_EXPERTS, LANES), lambda i: (0, 0))],
        out_shape=[jax.ShapeDtypeStruct((TOP_K, t), I32), jax.ShapeDtypeStruct((t, TOP_K), F32),
                   jax.ShapeDtypeStruct((TOP_K, t), I32),
                   jax.ShapeDtypeStruct((N_EXPERTS, LANES), I32)],
        scratch_shapes=[pltpu.VMEM((N_EXPERTS, LANES), F32)],
        compiler_params=_cparams("arbitrary"),
        name="router_topk",
    )(x, rw_t, rbias)


def _dest_kernel(start_ref, idx_ref, rank_ref, ids_ref, *, n_rows):
    idx = idx_ref[...]
    dest = rank_ref[...] + pl.program_id(0) * n_rows
    for e in range(N_EXPERTS):
        dest = dest + jnp.where(idx == e, start_ref[e], 0)
    ids_ref[...] = dest


def _piece_ids(pad_start, idx, rank, n_rows):
    t = idx.shape[1]
    tm = min(DEST_TM, t)
    spec = pl.BlockSpec((TOP_K, tm), lambda j, i, s: (0, i))
    return pl.pallas_call(
        functools.partial(_dest_kernel, n_rows=n_rows),
        grid_spec=pltpu.PrefetchScalarGridSpec(
            num_scalar_prefetch=1, grid=(SUB_ROWS, t // tm), in_specs=[spec, spec],
            out_specs=pl.BlockSpec((TOP_K, tm), lambda j, i, s: (j, i))),
        out_shape=jax.ShapeDtypeStruct((SUB_ROWS * TOP_K, t), I32),
        compiler_params=_cparams("parallel", "parallel"),
        name="piece_ids",
    )(pad_start, idx, rank)


def _swiglu(xb, w1_ref, w3_ref, w2_ref):
    a = jnp.dot(xb, w1_ref[...], preferred_element_type=F32)
    g = jnp.dot(xb, w3_ref[...], preferred_element_type=F32)
    hid = (a * jax.nn.sigmoid(a) * g).astype(BF16)
    return jnp.dot(hid, w2_ref[...], preferred_element_type=F32)


def _sc_mesh():
    return plsc.VectorSubcoreMesh(core_axis_name="core", subcore_axis_name="subcore")


def _sc_dispatch(xp, ids, n_rows):
    t = xp.shape[1]
    per_plane = t // SC_WINDOW

    @pl.kernel(out_type=jax.ShapeDtypeStruct((SUB_ROWS * n_rows, SUB_WORDS), U32), mesh=_sc_mesh(),
               scratch_types=[])
    def scatter(x_hbm, i_hbm, o_hbm):
        def body(x_vmem, i_vmem):
            for k in range(TOP_K):
                pltpu.sync_copy(x_vmem, o_hbm.at[i_vmem.at[k]])

        pltpu.emit_pipeline(
            body,
            grid=(SUB_ROWS * per_plane,),
            in_specs=[pl.BlockSpec((SC_WINDOW, SUB_WORDS), lambda w: (w, 0)),
                      pl.BlockSpec((TOP_K, SC_WINDOW), lambda w: (w // per_plane, w % per_plane))],
            out_specs=[],
            core_axis_name=("core", "subcore"),
            dimension_semantics=(pltpu.PARALLEL,),
        )(x_hbm, i_hbm)

    out = scatter(xp.reshape(SUB_ROWS * t, SUB_WORDS), ids)
    return out.reshape(SUB_ROWS, n_rows, SUB_WORDS)


def _sc_gather(ys, ids):
    n_rows = ys.shape[1]
    t = ids.shape[1]
    per_row = t // SC_WINDOW
    pieces = SUB_ROWS * TOP_K * t

    @pl.kernel(out_type=jax.ShapeDtypeStruct((pieces, SUB_WORDS), U32), mesh=_sc_mesh(),
               scratch_types=[])
    def gather(y_hbm, i_hbm, o_hbm):
        def body(i_vmem, o_vmem):
            pltpu.sync_copy(y_hbm.at[i_vmem.at[0]], o_vmem)

        pltpu.emit_pipeline(
            body,
            grid=(pieces // SC_WINDOW,),
            in_specs=[pl.BlockSpec((1, SC_WINDOW), lambda w: (w // per_row, w % per_row))],
            out_specs=[pl.BlockSpec((SC_WINDOW, SUB_WORDS), lambda w: (w, 0))],
            core_axis_name=("core", "subcore"),
            dimension_semantics=(pltpu.PARALLEL,),
        )(i_hbm, o_hbm)

    out = gather(ys.reshape(SUB_ROWS * n_rows, SUB_WORDS), ids)
    return out.reshape(SUB_ROWS, TOP_K * t, SUB_WORDS)


def _expert_kernel(bexp_ref, blive_ref, nused_ref, bfirst_ref, bslot_ref, bnext_ref, xs_ref, w1_hbm,
                   w3_hbm, w2_hbm, ys_ref, st1, st3, st2, w1b, w3b, w2b, xb, sem, *, layer):
    b = pl.program_id(0)
    used = b < nused_ref[0]

    def weight_copies(e, slot):
        return (pltpu.make_async_copy(w1_hbm.at[layer, e], st1.at[slot], sem.at[slot, 0]),
                pltpu.make_async_copy(w3_hbm.at[layer, e], st3.at[slot], sem.at[slot, 1]),
                pltpu.make_async_copy(w2_hbm.at[layer, e], st2.at[slot], sem.at[slot, 2]))

    @pl.when(b == 0)
    def _():
        for cp in weight_copies(bexp_ref[0], 0):
            cp.start()

    @pl.when(jnp.logical_and(used, bfirst_ref[b] == 1))
    def _():
        slot = bslot_ref[b]
        for cp in weight_copies(bexp_ref[b], slot):
            cp.wait()

        @pl.when(bnext_ref[b] >= 0)
        def _():
            for cp in weight_copies(bnext_ref[b], 1 - slot):
                cp.start()

        w1b[...] = st1[slot].astype(BF16)
        w3b[...] = st3[slot].astype(BF16)
        w2b[...] = st2[slot].astype(BF16)

    bm = xb.shape[0]
    live_rows = blive_ref[b]

    def run(m):
        live = lax.broadcasted_iota(I32, (m, SUB_WORDS), 0) < live_rows
        for j in range(SUB_ROWS):
            lo, hi = _unpack_rows(jnp.where(live, xs_ref[j, :m, :], jnp.uint32(0)))
            lo_cols, hi_cols = _piece_cols(j)
            xb[:m, lo_cols] = lo.astype(BF16)
            xb[:m, hi_cols] = hi.astype(BF16)
        y = _swiglu(xb[:m, :], w1b, w3b, w2b)
        for j in range(SUB_ROWS):
            lo_cols, hi_cols = _piece_cols(j)
            ys_ref[j, :m, :] = _pack_rows(y[:, lo_cols], y[:, hi_cols])

    @pl.when(jnp.logical_and(used, live_rows > bm // 2))
    def _():
        run(bm)

    @pl.when(jnp.logical_and(used, live_rows <= bm // 2))
    def _():
        run(bm // 2)
        ys_ref[:, bm // 2:, :] = jnp.zeros((SUB_ROWS, bm - bm // 2, SUB_WORDS), U32)

    @pl.when(jnp.logical_not(used))
    def _():
        ys_ref[...] = jnp.zeros_like(ys_ref)


def _experts(r, xs, w1, w3, w2, layer):
    n_blocks = xs.shape[1] // EXPERT_BM
    n_tables = 6

    def rows_spec(index):
        return pl.BlockSpec((SUB_ROWS, EXPERT_BM, SUB_WORDS), lambda b, *tables: (0, index(b, tables), 0))

    grid_spec = pltpu.PrefetchScalarGridSpec(
        num_scalar_prefetch=n_tables,
        grid=(n_blocks,),
        in_specs=[rows_spec(lambda b, tables: jnp.minimum(b, tables[2][0] - 1)),
                  pl.BlockSpec(memory_space=pl.ANY), pl.BlockSpec(memory_space=pl.ANY),
                  pl.BlockSpec(memory_space=pl.ANY)],
        out_specs=rows_spec(lambda b, tables: b),
        scratch_shapes=[pltpu.VMEM((2, D_MODEL, D_EXPERT), F32), pltpu.VMEM((2, D_MODEL, D_EXPERT), F32),
                        pltpu.VMEM((2, D_EXPERT, D_MODEL), F32),
                        pltpu.VMEM((D_MODEL, D_EXPERT), BF16), pltpu.VMEM((D_MODEL, D_EXPERT), BF16),
                        pltpu.VMEM((D_EXPERT, D_MODEL), BF16), pltpu.VMEM((EXPERT_BM, D_MODEL), BF16),
                        pltpu.SemaphoreType.DMA((2, 3))],
    )
    return pl.pallas_call(
        functools.partial(_expert_kernel, layer=layer),
        grid_spec=grid_spec,
        out_shape=jax.ShapeDtypeStruct(xs.shape, U32),
        compiler_params=_cparams("arbitrary"),
        name="moe_experts",
    )(r["block_exp"], r["block_live"], r["n_used"], r["block_first"], r["block_slot"], r["block_next"],
      xs, w1, w3, w2)


def _combine_kernel(gate_ref, x_ref, xb_ref, s1_ref, s3_ref, s2_ref, g_ref, b_ref, *refs):
    yk_refs = refs[:TOP_K]
    xo_ref, xbo_ref, hbuf = refs[-3:]
    tm = x_ref.shape[0]
    shared = _swiglu(xb_ref[...], s1_ref, s3_ref, s2_ref)
    gates = [jnp.broadcast_to(gate_ref[:, k:k + 1], (tm, SUB_WORDS)) for k in range(TOP_K)]
    for j in range(SUB_ROWS):
        lo_cols, hi_cols = _piece_cols(j)
        acc_lo = shared[:, lo_cols]
        acc_hi = shared[:, hi_cols]
        for k in range(TOP_K):
            lo, hi = _unpack_rows(yk_refs[k][j])
            acc_lo = acc_lo + gates[k] * lo
            acc_hi = acc_hi + gates[k] * hi
        hbuf[:, lo_cols] = acc_lo
        hbuf[:, hi_cols] = acc_hi
    y = _layer_norm(DEEPNORM_ALPHA * x_ref[...] + hbuf[...], g_ref[...], b_ref[...])
    xo_ref[...] = y
    xbo_ref[...] = y.astype(BF16)


def _combine(gate_t, x, xb, yg, p, out_rows=None, out_row0=0, out_buf=None):
    t = x.shape[0]
    tm = min(COMBINE_TM, t)
    steps = t // tm
    row = lambda i: (i, 0)
    assert out_row0 % tm == 0

    def slot_spec(k):
        return pl.BlockSpec((SUB_ROWS, tm, SUB_WORDS), lambda i: (0, k * steps + i, 0))

    in_specs = [pl.BlockSpec((tm, TOP_K), row),
                pl.BlockSpec((tm, D_MODEL), row),
                pl.BlockSpec((tm, D_MODEL), row),
                _resident((D_MODEL, D_EXPERT), lambda i: (0, 0)),
                _resident((D_MODEL, D_EXPERT), lambda i: (0, 0)),
                _resident((D_EXPERT, D_MODEL), lambda i: (0, 0)),
                _resident((1, D_MODEL), lambda i: (0, 0)),
                _resident((1, D_MODEL), lambda i: (0, 0))] + [slot_spec(k) for k in range(TOP_K)]
    args = [gate_t, x, xb, p["sh_w1"], p["sh_w3"], p["sh_w2"], p["ln2_g"], p["ln2_b"]] + [yg] * TOP_K
    aliases = {}
    if out_buf is not None:
        aliases = {len(args): 0}
        in_specs.append(pl.BlockSpec(memory_space=pl.ANY))
        args.append(out_buf)
    return pl.pallas_call(
        _combine_kernel,
        grid=(steps,),
        in_specs=in_specs,
        out_specs=[pl.BlockSpec((tm, D_MODEL), lambda i: (i + out_row0 // tm, 0)),
                   pl.BlockSpec((tm, D_MODEL), row)],
        out_shape=[jax.ShapeDtypeStruct((out_rows or t, D_MODEL), F32),
                   jax.ShapeDtypeStruct((t, D_MODEL), BF16)],
        scratch_shapes=[pltpu.VMEM((tm, D_MODEL), F32)],
        input_output_aliases=aliases,
        compiler_params=_cparams("parallel"),
        name="moe_combine_ln",
    )(*args)


def _n_expert_blocks(t):
    return -(-(t * TOP_K + N_EXPERTS * (EXPERT_BM - 1)) // EXPERT_BM)


def _route(xb, p):
    t = xb.shape[0]
    n_blocks = _n_expert_blocks(t)
    idx, gate_t, rank, cnt = _router(xb, p["router_w_t"], p["router_bias"])
    counts = cnt[:, 0]
    padded = (counts + EXPERT_BM - 1) // EXPERT_BM * EXPERT_BM
    pad_end = jnp.cumsum(padded)
    pad_start = pad_end - padded
    n_used = (pad_end[-1:] // EXPERT_BM).astype(I32)
    block = jnp.arange(n_blocks, dtype=I32)
    block_exp = jnp.minimum(jnp.sum(pad_end[None, :] <= block[:, None] * EXPERT_BM, axis=1),
                            N_EXPERTS - 1).astype(I32)
    expert = jnp.arange(N_EXPERTS, dtype=I32)
    of_block = block_exp[:, None] == expert[None, :]

    def per_block(table):
        return jnp.sum(jnp.where(of_block, table[None, :], 0), axis=1).astype(I32)

    block_live = jnp.clip(per_block(counts) - (block * EXPERT_BM - per_block(pad_start)), 0, EXPERT_BM)
    active = counts > 0
    ordinal = jnp.cumsum(active.astype(I32)) - 1
    later = active[None, :] & (expert[None, :] > expert[:, None])
    next_active = jnp.min(jnp.where(later, expert[None, :], N_EXPERTS), axis=1)
    next_active = jnp.where(next_active < N_EXPERTS, next_active, -1)
    used = block < n_used[0]
    first = used & (block * EXPERT_BM == per_block(pad_start))
    n_rows = n_blocks * EXPERT_BM
    ids = _piece_ids(pad_start.astype(I32), idx, rank, n_rows)
    return dict(ids=ids, gate_t=gate_t, block_exp=block_exp, block_live=block_live.astype(I32),
                n_used=n_used, block_first=first.astype(I32), block_slot=per_block(ordinal) % 2,
                block_next=per_block(next_active), n_rows=n_rows)


def _layer_params(l, b_gate, sinks, pool_w, pool_scale, conv_w, conv_b, lru_wa, lru_ba, lru_wx,
                  lru_bx, lru_lambda, w_branch, w_out, ln1_g, ln1_b, router_w, router_bias,
                  sh_w1, sh_w3, sh_w2, ln2_g, ln2_b):
    row = lambda a: a[l].reshape(1, -1).astype(F32)
    return dict(
        b_gate=row(b_gate), sinks=sinks[l].astype(F32),
        pool_w=pool_w[l].astype(BF16), pool_scale=row(pool_scale),
        conv_w=conv_w[l].astype(F32), conv_b=row(conv_b),
        lru_wa=_block_diag(lru_wa[l]).astype(BF16), lru_ba=row(lru_ba),
        lru_wx=_block_diag(lru_wx[l]).astype(BF16), lru_bx=row(lru_bx),
        lru_lambda=row(lru_lambda),
        w_branch=w_branch[l].astype(BF16), w_out=w_out[l].astype(BF16),
        ln1_g=row(ln1_g), ln1_b=row(ln1_b),
        router_w_t=router_w[l].T.astype(BF16), router_bias=router_bias[l].reshape(-1, 1).astype(F32),
        sh_w1=sh_w1[l].astype(BF16), sh_w3=sh_w3[l].astype(BF16), sh_w2=sh_w2[l].astype(BF16),
        ln2_g=row(ln2_g), ln2_b=row(ln2_b),
    )


def kernel(x, positions, w_in, b_gate, sinks, pool_w, pool_scale, conv_w, conv_b, lru_wa, lru_ba, lru_wx, lru_bx, lru_lambda, w_branch, w_out, ln1_g, ln1_b, router_w, router_bias, exp_w1, exp_w3, exp_w2, sh_w1, sh_w3, sh_w2, ln2_g, ln2_b):
    batch, seq_len, d = x.shape
    assert d == D_MODEL and seq_len % max(ATTN_TQ, MIX_TB) == 0 and seq_len % min(PROJ_TM, seq_len) == 0
    t = batch * seq_len
    cos_all, sin_all = _rope_tables(positions)
    x_all = x.reshape(t, d).astype(F32)
    xb_all = x_all.astype(BF16)
    seqs = []
    for s in range(batch):
        rows = slice(s * seq_len, (s + 1) * seq_len)
        seqs.append(dict(xf=x_all, xb=xb_all, row0=s * seq_len, cos=cos_all[rows], sin=sin_all[rows]))
    for l in range(DEPTH):
        p = _layer_params(l, b_gate, sinks, pool_w, pool_scale, conv_w, conv_b, lru_wa, lru_ba,
                          lru_wx, lru_bx, lru_lambda, w_branch, w_out, ln1_g, ln1_b, router_w,
                          router_bias, sh_w1, sh_w3, sh_w2, ln2_g, ln2_b)
        for sq in seqs:
            proj = _in_proj(sq["xb"], sq["row0"], seq_len, w_in, l)
            ya = _attention(proj, sq["cos"], sq["sin"], p["sinks"], seq_len)
            yp, yl = _mixers(proj, p, 1, seq_len)
            sq["xf"], sq["xb"], xp = _merge(proj, ya, yp, yl, sq["xf"], sq["row0"], p)
            sq["row0"] = 0
            sq["route"] = _route(sq["xb"], p)
            sq["xs"] = _sc_dispatch(xp, sq["route"]["ids"], sq["route"]["n_rows"])
        for sq in seqs:
            ys = _experts(sq["route"], sq["xs"], exp_w1, exp_w3, exp_w2, l)
            sq["yg"] = _sc_gather(ys, sq["route"]["ids"])
        out = None
        for s, sq in enumerate(seqs):
            if l + 1 < DEPTH:
                sq["xf"], sq["xb"] = _combine(sq["route"]["gate_t"], sq["xf"], sq["xb"], sq["yg"], p)
            else:
                out, _ = _combine(sq["route"]["gate_t"], sq["xf"], sq["xb"], sq["yg"], p,
                                  out_rows=t, out_row0=s * seq_len, out_buf=out)
    return out.reshape(batch, seq_len, d).astype(x.dtype)
```

```python
import functools

import jax
import jax.numpy as jnp
from jax import lax
from jax.experimental import pallas as pl
from jax.experimental.pallas import tpu as pltpu
from jax.experimental.pallas import tpu_sc as plsc

F32 = jnp.float32
BF16 = jnp.bfloat16
I32 = jnp.int32
U32 = jnp.uint32

D_MODEL = 2048
DEPTH = 2
CHUNK = 64
HEAD_DIM = 64
HALF_HEAD = HEAD_DIM // 2
N_HEADS = 16
N_KV_HEADS = 4
KV_GROUP = N_HEADS // N_KV_HEADS
ATTN_WIDTH = N_HEADS * HEAD_DIM
KV_WIDTH = N_KV_HEADS * HEAD_DIM
WINDOW_CHUNKS = 2
ROPE_THETA = 10000.0
POOL_WINDOWS = (2, 4, 8, 16)
POOL_WIDTH = 1024
POOL_GROUP = POOL_WIDTH // len(POOL_WINDOWS)
LRU_WIDTH = 1024
LRU_BLOCKS = 16
LRU_BLOCK = LRU_WIDTH // LRU_BLOCKS
CONV_WIDTH = 4
LRU_C = 8.0
N_BRANCH = 3
GATE_WIDTH = N_BRANCH * D_MODEL
IN_WIDTH = ATTN_WIDTH + 2 * KV_WIDTH + POOL_WIDTH + 2 * LRU_WIDTH + GATE_WIDTH
N_EXPERTS = 64
TOP_K = 8
D_EXPERT = 512
ROUTED_SCALE = 2.5
LN_EPS = 1e-5
DEEPNORM_ALPHA = (2 * DEPTH) ** 0.25

COL_GATE = 0
COL_Q = GATE_WIDTH
COL_POOL = COL_Q + ATTN_WIDTH
COL_LRU = COL_POOL + POOL_WIDTH
COL_GELU = COL_LRU + LRU_WIDTH
COL_K = COL_GELU + LRU_WIDTH
COL_V = COL_K + KV_WIDTH

LANES = 128
SUBLANES = 8
MXU_DIM = 256
VMEM_LIMIT_BYTES = 56 * 1024 * 1024

PROJ_TM = 1024
PROJ_TN = 1536
ATTN_TQ = 512
ATTN_HALO = WINDOW_CHUNKS * CHUNK
MIX_TB = 512
POOL_HALO = 16
CONV_HALO = 8
MERGE_TM = 256
ROUTER_TM = 512
DEST_TM = 2048
EXPERT_BM = 512
COMBINE_TM = 256
PROJ_SRC = 512
NEG_BIG = -1e30

SUB_WORDS = 256
SUB_COLS = 2 * SUB_WORDS
SUB_ROWS = D_MODEL // SUB_COLS
SC_WINDOW = 128


def _cparams(*sem):
    return pltpu.CompilerParams(dimension_semantics=sem, vmem_limit_bytes=VMEM_LIMIT_BYTES)


def _resident(shape, index_map):
    return pl.BlockSpec(shape, index_map, pipeline_mode=pl.Buffered(1))


def _pack_rows(lo, hi):
    lo_u = lax.bitcast_convert_type(lo.astype(BF16).astype(F32), U32)
    hi_u = lax.bitcast_convert_type(hi.astype(BF16).astype(F32), U32)
    return (lo_u >> 16) | (hi_u & jnp.uint32(0xFFFF0000))


def _unpack_rows(w):
    lo = lax.bitcast_convert_type(w << 16, F32)
    hi = lax.bitcast_convert_type(w & jnp.uint32(0xFFFF0000), F32)
    return lo, hi


def _piece_cols(j):
    return (slice(j * SUB_COLS, j * SUB_COLS + SUB_WORDS),
            slice(j * SUB_COLS + SUB_WORDS, (j + 1) * SUB_COLS))


def _store_pieces(ref, y):
    for j in range(SUB_ROWS):
        lo_cols, hi_cols = _piece_cols(j)
        ref[j] = _pack_rows(y[:, lo_cols], y[:, hi_cols])


def _layer_norm(y, g, b):
    mu = jnp.mean(y, axis=-1, keepdims=True)
    d = y - mu
    var = jnp.mean(d * d, axis=-1, keepdims=True)
    return d * lax.rsqrt(var + LN_EPS) * g + b


def _rope_kernel(pos_ref, inv_ref, cos_ref, sin_ref):
    ang = pos_ref[...].astype(F32) * inv_ref[...]
    lane = lax.broadcasted_iota(I32, ang.shape, 1)
    first_half = (lane % HEAD_DIM) < HALF_HEAD
    cos_ref[...] = jnp.cos(ang)
    s = jnp.sin(ang)
    sin_ref[...] = jnp.where(first_half, -s, s)


def _rope_tables(positions):
    t = positions.size
    tm = min(1024, t)
    inv_freq = 1.0 / (ROPE_THETA ** (jnp.arange(0, HEAD_DIM, 2, dtype=F32) / HEAD_DIM))
    inv = jnp.tile(inv_freq, LANES // HALF_HEAD).reshape(1, LANES)
    return pl.pallas_call(
        _rope_kernel,
        grid=(t // tm,),
        in_specs=[pl.BlockSpec((tm, 1), lambda i: (i, 0)),
                  pl.BlockSpec((1, LANES), lambda i: (0, 0))],
        out_specs=[pl.BlockSpec((tm, LANES), lambda i: (i, 0))] * 2,
        out_shape=[jax.ShapeDtypeStruct((t, LANES), F32)] * 2,
        compiler_params=_cparams("parallel"),
        name="rope_tables",
    )(positions.reshape(t, 1), inv)


def _proj_source_blocks():
    ref_order = (("q", ATTN_WIDTH), ("k", KV_WIDTH), ("v", KV_WIDTH), ("pool", POOL_WIDTH),
                 ("lru", LRU_WIDTH), ("gelu", LRU_WIDTH), ("gate", GATE_WIDTH))
    start, col = {}, 0
    for name, width in ref_order:
        start[name] = col
        col += width
    assert start["v"] == start["k"] + KV_WIDTH and 2 * KV_WIDTH == PROJ_SRC
    out_order = (("gate", GATE_WIDTH), ("q", ATTN_WIDTH), ("pool", POOL_WIDTH), ("lru", LRU_WIDTH),
                 ("gelu", LRU_WIDTH), ("k", 2 * KV_WIDTH))
    blocks = []
    for name, width in out_order:
        assert start[name] % PROJ_SRC == 0 and width % PROJ_SRC == 0
        blocks += [start[name] // PROJ_SRC + c for c in range(width // PROJ_SRC)]
    return blocks


def _in_proj_kernel(src_ref, x_ref, *refs):
    w_refs, o_ref, wb = refs[:-2], refs[-2], refs[-1]

    @pl.when(pl.program_id(1) == 0)
    def _():
        for c, w_ref in enumerate(w_refs):
            wb[:, c * PROJ_SRC:(c + 1) * PROJ_SRC] = w_ref[0].astype(BF16)

    o_ref[...] = jnp.dot(x_ref[...], wb[...], preferred_element_type=F32).astype(o_ref.dtype)


def _in_proj(xb, xb_row0, rows, w_in, layer):
    d = xb.shape[1]
    n = w_in.shape[2]
    tm = min(PROJ_TM, rows)
    assert xb_row0 % tm == 0
    per = PROJ_TN // PROJ_SRC
    src = jnp.asarray(_proj_source_blocks(), I32)

    def w_spec(c):
        return pl.BlockSpec((1, d, PROJ_SRC), lambda j, i, s: (layer, 0, s[per * j + c]))

    grid_spec = pltpu.PrefetchScalarGridSpec(
        num_scalar_prefetch=1,
        grid=(n // PROJ_TN, rows // tm),
        in_specs=[pl.BlockSpec((tm, d), lambda j, i, s: (i + xb_row0 // tm, 0))]
        + [w_spec(c) for c in range(per)],
        out_specs=pl.BlockSpec((tm, PROJ_TN), lambda j, i, s: (i, j)),
        scratch_shapes=[pltpu.VMEM((d, PROJ_TN), BF16)],
    )
    return pl.pallas_call(
        _in_proj_kernel,
        grid_spec=grid_spec,
        out_shape=jax.ShapeDtypeStruct((rows, n), BF16),
        compiler_params=_cparams("arbitrary", "arbitrary"),
        name="in_proj",
    )(src, xb, *([w_in] * per))


def _rope(x, cos, sin):
    w = x.shape[1]
    reps = w // LANES
    c = jnp.concatenate([cos] * reps, axis=1)
    s = jnp.concatenate([sin] * reps, axis=1)
    lane = lax.broadcasted_iota(I32, x.shape, 1)
    first_half = (lane % HEAD_DIM) < HALF_HEAD
    rot = jnp.where(first_half, pltpu.roll(x, w - HALF_HEAD, 1), pltpu.roll(x, HALF_HEAD, 1))
    return x * c + rot * s


def _attn_kernel(q_ref, k_ref, v_ref, kh_ref, vh_ref, cq_ref, sq_ref, ch_ref, sh_ref, sink_ref,
                 o_ref, s_buf, p_buf, r_buf, *, steps_per_seq):
    first = (pl.program_id(0) % steps_per_seq) == 0
    q = _rope(q_ref[...].astype(F32), cq_ref[...], sq_ref[...]) * (HEAD_DIM ** -0.5)
    q = q.astype(BF16)
    k_own = _rope(k_ref[...].astype(F32), cq_ref[...], sq_ref[...])
    k_halo = _rope(kh_ref[...].astype(F32), ch_ref[...], sh_ref[...])
    k_all = jnp.concatenate([k_halo, k_own], axis=0).astype(BF16)
    v_all = jnp.concatenate([vh_ref[...], v_ref[...]], axis=0)
    span = (WINDOW_CHUNKS + 1) * CHUNK
    rows = KV_GROUP * CHUNK
    chunk_rows = N_KV_HEADS * rows
    n_chunks = q.shape[0] // CHUNK
    win_chunk = lax.broadcasted_iota(I32, (rows, span), 1) // CHUNK
    oldest = jnp.where(first, 0, -WINDOW_CHUNKS)

    def tile_rows(c, g):
        return slice(c * chunk_rows + g * rows, c * chunk_rows + (g + 1) * rows)

    def kv_cols(g):
        return slice(g * HEAD_DIM, (g + 1) * HEAD_DIM)

    for c in range(n_chunks):
        q_c = q[c * CHUNK:(c + 1) * CHUNK, :]
        k_win = k_all[c * CHUNK:c * CHUNK + span, :]
        for g in range(N_KV_HEADS):
            qs = jnp.concatenate(
                [q_c[:, (g * KV_GROUP + i) * HEAD_DIM:(g * KV_GROUP + i + 1) * HEAD_DIM]
                 for i in range(KV_GROUP)], axis=0)
            s = lax.dot_general(qs, k_win[:, kv_cols(g)], (((1,), (1,)), ((), ())),
                                preferred_element_type=F32)
            if c < WINDOW_CHUNKS:
                s = jnp.where(win_chunk + (c - WINDOW_CHUNKS) >= oldest, s, NEG_BIG)
            s_buf[tile_rows(c, g), :] = s
    sink = sink_ref[...]
    for c in range(n_chunks):
        blk = slice(c * chunk_rows, (c + 1) * chunk_rows)
        s = s_buf[blk, :]
        m = jnp.maximum(jnp.max(s, axis=1, keepdims=True), sink)
        p = jnp.exp(s - m)
        p_buf[blk, :] = p.astype(BF16)
        r_buf[blk, :] = 1.0 / (jnp.sum(p, axis=1, keepdims=True) + jnp.exp(sink - m))
    for c in range(n_chunks):
        v_win = v_all[c * CHUNK:c * CHUNK + span, :]
        heads = []
        for g in range(N_KV_HEADS):
            o = jnp.dot(p_buf[tile_rows(c, g), :], v_win[:, kv_cols(g)], preferred_element_type=F32)
            o = o * r_buf[tile_rows(c, g), :]
            heads += [o[i * CHUNK:(i + 1) * CHUNK, :] for i in range(KV_GROUP)]
        o_ref[c * CHUNK:(c + 1) * CHUNK, :] = jnp.concatenate(heads, axis=1).astype(o_ref.dtype)


def _attention(proj, cos, sin, sinks, seq_len):
    t = proj.shape[0]
    tq = ATTN_TQ
    steps_per_seq = seq_len // tq
    halo_per_tq = tq // ATTN_HALO
    q_blk = COL_Q // ATTN_WIDTH
    k_blk = COL_K // KV_WIDTH
    v_blk = COL_V // KV_WIDTH

    def halo_row(i):
        return jnp.maximum(i * halo_per_tq - 1, 0)

    span = (WINDOW_CHUNKS + 1) * CHUNK
    tile_rows = (tq // CHUNK) * N_HEADS * CHUNK
    sink_rows = jnp.repeat(sinks, CHUNK).reshape(N_HEADS * CHUNK, 1)
    return pl.pallas_call(
        functools.partial(_attn_kernel, steps_per_seq=steps_per_seq),
        grid=(t // tq,),
        in_specs=[
            pl.BlockSpec((tq, ATTN_WIDTH), lambda i: (i, q_blk)),
            pl.BlockSpec((tq, KV_WIDTH), lambda i: (i, k_blk)),
            pl.BlockSpec((tq, KV_WIDTH), lambda i: (i, v_blk)),
            pl.BlockSpec((ATTN_HALO, KV_WIDTH), lambda i: (halo_row(i), k_blk)),
            pl.BlockSpec((ATTN_HALO, KV_WIDTH), lambda i: (halo_row(i), v_blk)),
            pl.BlockSpec((tq, LANES), lambda i: (i, 0)),
            pl.BlockSpec((tq, LANES), lambda i: (i, 0)),
            pl.BlockSpec((ATTN_HALO, LANES), lambda i: (halo_row(i), 0)),
            pl.BlockSpec((ATTN_HALO, LANES), lambda i: (halo_row(i), 0)),
            pl.BlockSpec((N_HEADS * CHUNK, 1), lambda i: (0, 0)),
        ],
        out_specs=pl.BlockSpec((tq, ATTN_WIDTH), lambda i: (i, 0)),
        out_shape=jax.ShapeDtypeStruct((t, ATTN_WIDTH), BF16),
        scratch_shapes=[pltpu.VMEM((tile_rows, span), F32), pltpu.VMEM((tile_rows, span), BF16),
                        pltpu.VMEM((tile_rows, 1), F32)],
        compiler_params=_cparams("parallel"),
        name="swa_attention",
    )(proj, proj, proj, proj, proj, cos, sin, cos, sin, sink_rows)


def _mix_kernel(up_ref, ul_ref, ug_ref, pw_ref, ps_ref, cw_ref, cb_ref, wa_ref, ba_ref, wx_ref,
                bx_ref, lam_ref, yp_ref, yl_ref, halo_p, halo_l, h_carry):
    s = pl.program_id(1)
    tb = up_ref.shape[0]

    @pl.when(s == 0)
    def _():
        halo_p[...] = jnp.zeros_like(halo_p)
        halo_l[...] = jnp.zeros_like(halo_l)
        h_carry[...] = jnp.zeros_like(h_carry)

    u = up_ref[...].astype(F32)
    ext = jnp.concatenate([halo_p[...], u], axis=0)
    sums = []
    ws = ext
    for g, win in enumerate(POOL_WINDOWS):
        shift = win // 2
        ws = ws + pltpu.roll(ws, shift, 0)
        sums.append(ws[POOL_HALO:, :POOL_GROUP])
        if g + 1 < len(POOL_WINDOWS):
            ws = ws[:, POOL_GROUP:]
    t_idx = (s * tb + lax.broadcasted_iota(I32, (tb, 1), 0)).astype(F32)
    for g, win in enumerate(POOL_WINDOWS):
        cols = slice(g * POOL_GROUP, (g + 1) * POOL_GROUP)
        count = jnp.minimum(t_idx + 1.0, float(win))
        pooled = sums[g] / count - u[:, cols]
        y = jnp.dot(pooled.astype(BF16), pw_ref[g], preferred_element_type=F32)
        yp_ref[:, cols] = (y * ps_ref[:, cols]).astype(yp_ref.dtype)
    halo_p[...] = u[tb - POOL_HALO:, :]

    ul = ul_ref[...].astype(F32)
    extl = jnp.concatenate([halo_l[...], ul], axis=0)
    xc = cb_ref[...] + cw_ref[CONV_WIDTH - 1:CONV_WIDTH, :] * ul
    for j in range(1, CONV_WIDTH):
        w_j = cw_ref[CONV_WIDTH - 1 - j:CONV_WIDTH - j, :]
        xc = xc + w_j * pltpu.roll(extl, j, 0)[CONV_HALO:, :]
    halo_l[...] = ul[tb - CONV_HALO:, :]
    xcb = xc.astype(BF16)
    r_parts, i_parts = [], []
    for j in range(LRU_WIDTH // MXU_DIM):
        cols = slice(j * MXU_DIM, (j + 1) * MXU_DIM)
        r_parts.append(jnp.dot(xcb[:, cols], wa_ref[j], preferred_element_type=F32))
        i_parts.append(jnp.dot(xcb[:, cols], wx_ref[j], preferred_element_type=F32))
    r = jax.nn.sigmoid(jnp.concatenate(r_parts, axis=1) + ba_ref[...])
    ig = jax.nn.sigmoid(jnp.concatenate(i_parts, axis=1) + bx_ref[...])
    z = -lam_ref[...]
    softplus = jnp.maximum(z, 0.0) + jnp.log1p(jnp.exp(-jnp.abs(z)))
    log_a = (-LRU_C) * r * softplus
    a = jnp.exp(log_a)
    one_minus_a2 = 1.0 - a * a
    root = jnp.where(one_minus_a2 > 0.0, one_minus_a2 * lax.rsqrt(one_minus_a2), 0.0)
    b = root * (ig * xc)
    row_in_group = lax.broadcasted_iota(I32, a.shape, 0) % SUBLANES
    d = 1
    while d < SUBLANES:
        keep = row_in_group >= d
        a_prev = jnp.where(keep, pltpu.roll(a, d, 0), 1.0)
        b_prev = jnp.where(keep, pltpu.roll(b, d, 0), 0.0)
        b = a * b_prev + b
        a = a * a_prev
        d *= 2
    gelu = jax.nn.gelu(ug_ref[...].astype(F32))
    h_prev = h_carry[...]
    pair = 2 * SUBLANES
    for g in range(tb // pair):
        hs = []
        for half in range(2):
            rows = slice(g * pair + half * SUBLANES, g * pair + (half + 1) * SUBLANES)
            h = a[rows, :] * h_prev + b[rows, :]
            h_prev = h[SUBLANES - 1:SUBLANES, :]
            hs.append(h)
        rows = slice(g * pair, (g + 1) * pair)
        yl_ref[rows, :] = (jnp.concatenate(hs, axis=0) * gelu[rows, :]).astype(yl_ref.dtype)
    h_carry[...] = h_prev


def _block_diag(w):
    per = MXU_DIM // LRU_BLOCK
    w = w.reshape(LRU_BLOCKS // per, per, LRU_BLOCK, LRU_BLOCK)
    eye = jnp.eye(per, dtype=w.dtype)
    out = jnp.einsum("gpij,pq->gpiqj", w, eye)
    return out.reshape(LRU_BLOCKS // per, MXU_DIM, MXU_DIM)


def _mixers(proj, p, batch, seq_len):
    t = proj.shape[0]
    tb = min(MIX_TB, seq_len)
    steps = seq_len // tb

    def rows(bi, si):
        return bi * steps + si

    def col_spec(col):
        blk = col // POOL_WIDTH
        return pl.BlockSpec((tb, POOL_WIDTH), lambda bi, si: (rows(bi, si), blk))

    def const(shape):
        nd = len(shape)
        return pl.BlockSpec(shape, lambda bi, si: (0,) * nd)

    out_spec = pl.BlockSpec((tb, POOL_WIDTH), lambda bi, si: (rows(bi, si), 0))
    return pl.pallas_call(
        _mix_kernel,
        grid=(batch, steps),
        in_specs=[col_spec(COL_POOL), col_spec(COL_LRU), col_spec(COL_GELU),
                  const((len(POOL_WINDOWS), POOL_GROUP, POOL_GROUP)), const((1, POOL_WIDTH)),
                  const((CONV_WIDTH, LRU_WIDTH)), const((1, LRU_WIDTH)),
                  const((LRU_WIDTH // MXU_DIM, MXU_DIM, MXU_DIM)), const((1, LRU_WIDTH)),
                  const((LRU_WIDTH // MXU_DIM, MXU_DIM, MXU_DIM)), const((1, LRU_WIDTH)),
                  const((1, LRU_WIDTH))],
        out_specs=[out_spec, out_spec],
        out_shape=[jax.ShapeDtypeStruct((t, POOL_WIDTH), BF16),
                   jax.ShapeDtypeStruct((t, LRU_WIDTH), BF16)],
        scratch_shapes=[pltpu.VMEM((POOL_HALO, POOL_WIDTH), F32),
                        pltpu.VMEM((CONV_HALO, LRU_WIDTH), F32),
                        pltpu.VMEM((1, LRU_WIDTH), F32)],
        compiler_params=_cparams("arbitrary", "arbitrary"),
        name="pool_rglru",
    )(proj, proj, proj, p["pool_w"], p["pool_scale"], p["conv_w"], p["conv_b"],
      p["lru_wa"], p["lru_ba"], p["lru_wx"], p["lru_bx"], p["lru_lambda"])


def _merge_kernel(gl_ref, ya_ref, yp_ref, yl_ref, x_ref, bg_ref, wb_ref, wo_ref, g_ref, b_ref,
                  xo_ref, xb_ref, xp_ref):
    merged = None
    for j, y_ref in enumerate((ya_ref, yp_ref, yl_ref)):
        cols = slice(j * D_MODEL, (j + 1) * D_MODEL)
        z = jnp.dot(y_ref[...], wb_ref[j], preferred_element_type=F32)
        gate = jax.nn.sigmoid(gl_ref[:, cols].astype(F32) + bg_ref[:, cols])
        merged = gate * z if merged is None else merged + gate * z
    h = jnp.dot(merged.astype(BF16), wo_ref[...], preferred_element_type=F32)
    y = _layer_norm(DEEPNORM_ALPHA * x_ref[...] + h, g_ref[...], b_ref[...])
    xo_ref[...] = y
    xb_ref[...] = y.astype(BF16)
    _store_pieces(xp_ref, y)


def _merge(proj, ya, yp, yl, x, x_row0, p):
    t = proj.shape[0]
    tm = min(MERGE_TM, t)
    assert x_row0 % tm == 0
    row = lambda i: (i, 0)
    return pl.pallas_call(
        _merge_kernel,
        grid=(t // tm,),
        in_specs=[pl.BlockSpec((tm, GATE_WIDTH), row),
                  pl.BlockSpec((tm, ATTN_WIDTH), row),
                  pl.BlockSpec((tm, POOL_WIDTH), row),
                  pl.BlockSpec((tm, LRU_WIDTH), row),
                  pl.BlockSpec((tm, D_MODEL), lambda i: (i + x_row0 // tm, 0)),
                  _resident((1, GATE_WIDTH), lambda i: (0, 0)),
                  _resident((N_BRANCH, ATTN_WIDTH, D_MODEL), lambda i: (0, 0, 0)),
                  _resident((D_MODEL, D_MODEL), lambda i: (0, 0)),
                  _resident((1, D_MODEL), lambda i: (0, 0)),
                  _resident((1, D_MODEL), lambda i: (0, 0))],
        out_specs=[pl.BlockSpec((tm, D_MODEL), row), pl.BlockSpec((tm, D_MODEL), row),
                   pl.BlockSpec((SUB_ROWS, tm, SUB_WORDS), lambda i: (0, i, 0))],
        out_shape=[jax.ShapeDtypeStruct((t, D_MODEL), F32),
                   jax.ShapeDtypeStruct((t, D_MODEL), BF16),
                   jax.ShapeDtypeStruct((SUB_ROWS, t, SUB_WORDS), U32)],
        compiler_params=_cparams("parallel"),
        name="merge_ln",
    )(proj, ya, yp, yl, x, p["b_gate"], p["w_branch"], p["w_out"], p["ln1_g"], p["ln1_b"])


def _router_kernel(x_ref, rw_ref, rb_ref, idx_ref, gate_ref, rank_ref, cnt_ref, carry):
    @pl.when(pl.program_id(0) == 0)
    def _():
        carry[...] = jnp.zeros_like(carry)

    tm = x_ref.shape[0]
    logits = lax.dot_general(rw_ref[...], x_ref[...], (((1,), (1,)), ((), ())),
                             preferred_element_type=F32)
    scores = jax.nn.sigmoid(logits)
    sel = scores + rb_ref[...]
    expert = lax.broadcasted_iota(I32, scores.shape, 0)
    idxs, tops = [], []
    chosen = jnp.zeros(scores.shape, F32)
    for _ in range(TOP_K):
        best = jnp.max(sel, axis=0, keepdims=True)
        ik = jnp.min(jnp.where(sel == best, expert, N_EXPERTS), axis=0, keepdims=True)
        hit = expert == ik
        tops.append(jnp.sum(jnp.where(hit, scores, 0.0), axis=0, keepdims=True))
        idxs.append(ik)
        chosen = jnp.where(hit, 1.0, chosen)
        sel = jnp.where(hit, -jnp.inf, sel)
    top = jnp.concatenate(tops, axis=0)
    idx = jnp.concatenate(idxs, axis=0)
    gates = top / jnp.sum(top, axis=0, keepdims=True) * ROUTED_SCALE
    gate_ref[...] = jnp.transpose(gates)
    idx_ref[...] = idx
    earlier = (lax.broadcasted_iota(I32, (tm, tm), 0) < lax.broadcasted_iota(I32, (tm, tm), 1))
    before = jnp.dot(chosen.astype(BF16), jnp.where(earlier, 1.0, 0.0).astype(BF16),
                     preferred_element_type=F32)
    base = before + carry[:, 0:1]
    ranks = [jnp.sum(jnp.where(expert == idxs[k], base, 0.0), axis=0, keepdims=True)
             for k in range(TOP_K)]
    rank_ref[...] = jnp.concatenate(ranks, axis=0).astype(I32)
    carry[...] = carry[...] + jnp.sum(chosen, axis=1, keepdims=True)
    cnt_ref[...] = carry[...].astype(I32)


def _router(x, rw_t, rbias):
    t = x.shape[0]
    tm = min(ROUTER_TM, t)
    tok = lambda i: (0, i)
    return pl.pallas_call(
        _router_kernel,
        grid=(t // tm,),
        in_specs=[pl.BlockSpec((tm, D_MODEL), lambda i: (i, 0)),
                  pl.BlockSpec((N_EXPERTS, D_MODEL), lambda i: (0, 0)),
                  pl.BlockSpec((N_EXPERTS, 1), lambda i: (0, 0))],
        out_specs=[pl.BlockSpec((TOP_K, tm), tok), pl.BlockSpec((tm, TOP_K), lambda i: (i, 0)),
                   pl.BlockSpec((TOP_K, tm), tok),
                   pl.BlockSpec((N_EXPERTS, LANES), lambda i: (0, 0))],
        out_shape=[jax.ShapeDtypeStruct((TOP_K, t), I32), jax.ShapeDtypeStruct((t, TOP_K), F32),
                   jax.ShapeDtypeStruct((TOP_K, t), I32),
                   jax.ShapeDtypeStruct((N_EXPERTS, LANES), I32)],
        scratch_shapes=[pltpu.VMEM((N_EXPERTS, LANES), F32)],
        compiler_params=_cparams("arbitrary"),
        name="router_topk",
    )(x, rw_t, rbias)


def _dest_kernel(start_ref, idx_ref, rank_ref, ids_ref, *, n_rows):
    idx = idx_ref[...]
    dest = rank_ref[...] + pl.program_id(0) * n_rows
    for e in range(N_EXPERTS):
        dest = dest + jnp.where(idx == e, start_ref[e], 0)
    ids_ref[...] = dest


def _piece_ids(pad_start, idx, rank, n_rows):
    t = idx.shape[1]
    tm = min(DEST_TM, t)
    spec = pl.BlockSpec((TOP_K, tm), lambda j, i, s: (0, i))
    return pl.pallas_call(
        functools.partial(_dest_kernel, n_rows=n_rows),
        grid_spec=pltpu.PrefetchScalarGridSpec(
            num_scalar_prefetch=1, grid=(SUB_ROWS, t // tm), in_specs=[spec, spec],
            out_specs=pl.BlockSpec((TOP_K, tm), lambda j, i, s: (j, i))),
        out_shape=jax.ShapeDtypeStruct((SUB_ROWS * TOP_K, t), I32),
        compiler_params=_cparams("parallel", "parallel"),
        name="piece_ids",
    )(pad_start, idx, rank)


def _swiglu(xb, w1_ref, w3_ref, w2_ref):
    a = jnp.dot(xb, w1_ref[...], preferred_element_type=F32)
    g = jnp.dot(xb, w3_ref[...], preferred_element_type=F32)
    hid = (a * jax.nn.sigmoid(a) * g).astype(BF16)
    return jnp.dot(hid, w2_ref[...], preferred_element_type=F32)


def _sc_mesh():
    return plsc.VectorSubcoreMesh(core_axis_name="core", subcore_axis_name="subcore")


def _sc_dispatch(xp, ids, n_rows):
    t = xp.shape[1]
    per_plane = t // SC_WINDOW

    @pl.kernel(out_type=jax.ShapeDtypeStruct((SUB_ROWS * n_rows, SUB_WORDS), U32), mesh=_sc_mesh(),
               scratch_types=[])
    def scatter(x_hbm, i_hbm, o_hbm):
        def body(x_vmem, i_vmem):
            for k in range(TOP_K):
                pltpu.sync_copy(x_vmem, o_hbm.at[i_vmem.at[k]])

        pltpu.emit_pipeline(
            body,
            grid=(SUB_ROWS * per_plane,),
            in_specs=[pl.BlockSpec((SC_WINDOW, SUB_WORDS), lambda w: (w, 0)),
                      pl.BlockSpec((TOP_K, SC_WINDOW), lambda w: (w // per_plane, w % per_plane))],
            out_specs=[],
            core_axis_name=("core", "subcore"),
            dimension_semantics=(pltpu.PARALLEL,),
        )(x_hbm, i_hbm)

    out = scatter(xp.reshape(SUB_ROWS * t, SUB_WORDS), ids)
    return out.reshape(SUB_ROWS, n_rows, SUB_WORDS)


def _sc_gather(ys, ids):
    n_rows = ys.shape[1]
    t = ids.shape[1]
    per_row = t // SC_WINDOW
    pieces = SUB_ROWS * TOP_K * t

    @pl.kernel(out_type=jax.ShapeDtypeStruct((pieces, SUB_WORDS), U32), mesh=_sc_mesh(),
               scratch_types=[])
    def gather(y_hbm, i_hbm, o_hbm):
        def body(i_vmem, o_vmem):
            pltpu.sync_copy(y_hbm.at[i_vmem.at[0]], o_vmem)

        pltpu.emit_pipeline(
            body,
            grid=(pieces // SC_WINDOW,),
            in_specs=[pl.BlockSpec((1, SC_WINDOW), lambda w: (w // per_row, w % per_row))],
            out_specs=[pl.BlockSpec((SC_WINDOW, SUB_WORDS), lambda w: (w, 0))],
            core_axis_name=("core", "subcore"),
            dimension_semantics=(pltpu.PARALLEL,),
        )(i_hbm, o_hbm)

    out = gather(ys.reshape(SUB_ROWS * n_rows, SUB_WORDS), ids)
    return out.reshape(SUB_ROWS, TOP_K * t, SUB_WORDS)


def _expert_kernel(bexp_ref, blive_ref, nused_ref, bfirst_ref, bslot_ref, bnext_ref, xs_ref, w1_hbm,
                   w3_hbm, w2_hbm, ys_ref, st1, st3, st2, w1b, w3b, w2b, xb, sem, *, layer):
    b = pl.program_id(0)
    used = b < nused_ref[0]

    def weight_copies(e, slot):
        return (pltpu.make_async_copy(w1_hbm.at[layer, e], st1.at[slot], sem.at[slot, 0]),
                pltpu.make_async_copy(w3_hbm.at[layer, e], st3.at[slot], sem.at[slot, 1]),
                pltpu.make_async_copy(w2_hbm.at[layer, e], st2.at[slot], sem.at[slot, 2]))

    @pl.when(b == 0)
    def _():
        for cp in weight_copies(bexp_ref[0], 0):
            cp.start()

    @pl.when(jnp.logical_and(used, bfirst_ref[b] == 1))
    def _():
        slot = bslot_ref[b]
        for cp in weight_copies(bexp_ref[b], slot):
            cp.wait()

        @pl.when(bnext_ref[b] >= 0)
        def _():
            for cp in weight_copies(bnext_ref[b], 1 - slot):
                cp.start()

        w1b[...] = st1[slot].astype(BF16)
        w3b[...] = st3[slot].astype(BF16)
        w2b[...] = st2[slot].astype(BF16)

    bm = xb.shape[0]
    live_rows = blive_ref[b]

    def run(m):
        live = lax.broadcasted_iota(I32, (m, SUB_WORDS), 0) < live_rows
        for j in range(SUB_ROWS):
            lo, hi = _unpack_rows(jnp.where(live, xs_ref[j, :m, :], jnp.uint32(0)))
            lo_cols, hi_cols = _piece_cols(j)
            xb[:m, lo_cols] = lo.astype(BF16)
            xb[:m, hi_cols] = hi.astype(BF16)
        y = _swiglu(xb[:m, :], w1b, w3b, w2b)
        for j in range(SUB_ROWS):
            lo_cols, hi_cols = _piece_cols(j)
            ys_ref[j, :m, :] = _pack_rows(y[:, lo_cols], y[:, hi_cols])

    @pl.when(jnp.logical_and(used, live_rows > bm // 2))
    def _():
        run(bm)

    @pl.when(jnp.logical_and(used, live_rows <= bm // 2))
    def _():
        run(bm // 2)
        ys_ref[:, bm // 2:, :] = jnp.zeros((SUB_ROWS, bm - bm // 2, SUB_WORDS), U32)

    @pl.when(jnp.logical_not(used))
    def _():
        ys_ref[...] = jnp.zeros_like(ys_ref)


def _experts(r, xs, w1, w3, w2, layer):
    n_blocks = xs.shape[1] // EXPERT_BM
    n_tables = 6

    def rows_spec(index):
        return pl.BlockSpec((SUB_ROWS, EXPERT_BM, SUB_WORDS), lambda b, *tables: (0, index(b, tables), 0))

    grid_spec = pltpu.PrefetchScalarGridSpec(
        num_scalar_prefetch=n_tables,
        grid=(n_blocks,),
        in_specs=[rows_spec(lambda b, tables: jnp.minimum(b, tables[2][0] - 1)),
                  pl.BlockSpec(memory_space=pl.ANY), pl.BlockSpec(memory_space=pl.ANY),
                  pl.BlockSpec(memory_space=pl.ANY)],
        out_specs=rows_spec(lambda b, tables: b),
        scratch_shapes=[pltpu.VMEM((2, D_MODEL, D_EXPERT), F32), pltpu.VMEM((2, D_MODEL, D_EXPERT), F32),
                        pltpu.VMEM((2, D_EXPERT, D_MODEL), F32),
                        pltpu.VMEM((D_MODEL, D_EXPERT), BF16), pltpu.VMEM((D_MODEL, D_EXPERT), BF16),
                        pltpu.VMEM((D_EXPERT, D_MODEL), BF16), pltpu.VMEM((EXPERT_BM, D_MODEL), BF16),
                        pltpu.SemaphoreType.DMA((2, 3))],
    )
    return pl.pallas_call(
        functools.partial(_expert_kernel, layer=layer),
        grid_spec=grid_spec,
        out_shape=jax.ShapeDtypeStruct(xs.shape, U32),
        compiler_params=_cparams("arbitrary"),
        name="moe_experts",
    )(r["block_exp"], r["block_live"], r["n_used"], r["block_first"], r["block_slot"], r["block_next"],
      xs, w1, w3, w2)


def _combine_kernel(gate_ref, x_ref, xb_ref, s1_ref, s3_ref, s2_ref, g_ref, b_ref, *refs):
    yk_refs = refs[:TOP_K]
    xo_ref, xbo_ref, hbuf = refs[-3:]
    tm = x_ref.shape[0]
    shared = _swiglu(xb_ref[...], s1_ref, s3_ref, s2_ref)
    gates = [jnp.broadcast_to(gate_ref[:, k:k + 1], (tm, SUB_WORDS)) for k in range(TOP_K)]
    for j in range(SUB_ROWS):
        lo_cols, hi_cols = _piece_cols(j)
        acc_lo = shared[:, lo_cols]
        acc_hi = shared[:, hi_cols]
        for k in range(TOP_K):
            lo, hi = _unpack_rows(yk_refs[k][j])
            acc_lo = acc_lo + gates[k] * lo
            acc_hi = acc_hi + gates[k] * hi
        hbuf[:, lo_cols] = acc_lo
        hbuf[:, hi_cols] = acc_hi
    y = _layer_norm(DEEPNORM_ALPHA * x_ref[...] + hbuf[...], g_ref[...], b_ref[...])
    xo_ref[...] = y
    xbo_ref[...] = y.astype(BF16)


def _combine(gate_t, x, xb, yg, p, out_rows=None, out_row0=0, out_buf=None):
    t = x.shape[0]
    tm = min(COMBINE_TM, t)
    steps = t // tm
    row = lambda i: (i, 0)
    assert out_row0 % tm == 0

    def slot_spec(k):
        return pl.BlockSpec((SUB_ROWS, tm, SUB_WORDS), lambda i: (0, k * steps + i, 0))

    in_specs = [pl.BlockSpec((tm, TOP_K), row),
                pl.BlockSpec((tm, D_MODEL), row),
                pl.BlockSpec((tm, D_MODEL), row),
                _resident((D_MODEL, D_EXPERT), lambda i: (0, 0)),
                _resident((D_MODEL, D_EXPERT), lambda i: (0, 0)),
                _resident((D_EXPERT, D_MODEL), lambda i: (0, 0)),
                _resident((1, D_MODEL), lambda i: (0, 0)),
                _resident((1, D_MODEL), lambda i: (0, 0))] + [slot_spec(k) for k in range(TOP_K)]
    args = [gate_t, x, xb, p["sh_w1"], p["sh_w3"], p["sh_w2"], p["ln2_g"], p["ln2_b"]] + [yg] * TOP_K
    aliases = {}
    if out_buf is not None:
        aliases = {len(args): 0}
        in_specs.append(pl.BlockSpec(memory_space=pl.ANY))
        args.append(out_buf)
    return pl.pallas_call(
        _combine_kernel,
        grid=(steps,),
        in_specs=in_specs,
        out_specs=[pl.BlockSpec((tm, D_MODEL), lambda i: (i + out_row0 // tm, 0)),
                   pl.BlockSpec((tm, D_MODEL), row)],
        out_shape=[jax.ShapeDtypeStruct((out_rows or t, D_MODEL), F32),
                   jax.ShapeDtypeStruct((t, D_MODEL), BF16)],
        scratch_shapes=[pltpu.VMEM((tm, D_MODEL), F32)],
        input_output_aliases=aliases,
        compiler_params=_cparams("parallel"),
        name="moe_combine_ln",
    )(*args)


def _n_expert_blocks(t):
    return -(-(t * TOP_K + N_EXPERTS * (EXPERT_BM - 1)) // EXPERT_BM)


def _route(xb, p):
    t = xb.shape[0]
    n_blocks = _n_expert_blocks(t)
    idx, gate_t, rank, cnt = _router(xb, p["router_w_t"], p["router_bias"])
    counts = cnt[:, 0]
    padded = (counts + EXPERT_BM - 1) // EXPERT_BM * EXPERT_BM
    pad_end = jnp.cumsum(padded)
    pad_start = pad_end - padded
    n_used = (pad_end[-1:] // EXPERT_BM).astype(I32)
    block = jnp.arange(n_blocks, dtype=I32)
    block_exp = jnp.minimum(jnp.sum(pad_end[None, :] <= block[:, None] * EXPERT_BM, axis=1),
                            N_EXPERTS - 1).astype(I32)
    expert = jnp.arange(N_EXPERTS, dtype=I32)
    of_block = block_exp[:, None] == expert[None, :]

    def per_block(table):
        return jnp.sum(jnp.where(of_block, table[None, :], 0), axis=1).astype(I32)

    block_live = jnp.clip(per_block(counts) - (block * EXPERT_BM - per_block(pad_start)), 0, EXPERT_BM)
    active = counts > 0
    ordinal = jnp.cumsum(active.astype(I32)) - 1
    later = active[None, :] & (expert[None, :] > expert[:, None])
    next_active = jnp.min(jnp.where(later, expert[None, :], N_EXPERTS), axis=1)
    next_active = jnp.where(next_active < N_EXPERTS, next_active, -1)
    used = block < n_used[0]
    first = used & (block * EXPERT_BM == per_block(pad_start))
    n_rows = n_blocks * EXPERT_BM
    ids = _piece_ids(pad_start.astype(I32), idx, rank, n_rows)
    return dict(ids=ids, gate_t=gate_t, block_exp=block_exp, block_live=block_live.astype(I32),
                n_used=n_used, block_first=first.astype(I32), block_slot=per_block(ordinal) % 2,
                block_next=per_block(next_active), n_rows=n_rows)


def _layer_params(l, b_gate, sinks, pool_w, pool_scale, conv_w, conv_b, lru_wa, lru_ba, lru_wx,
                  lru_bx, lru_lambda, w_branch, w_out, ln1_g, ln1_b, router_w, router_bias,
                  sh_w1, sh_w3, sh_w2, ln2_g, ln2_b):
    row = lambda a: a[l].reshape(1, -1).astype(F32)
    return dict(
        b_gate=row(b_gate), sinks=sinks[l].astype(F32),
        pool_w=pool_w[l].astype(BF16), pool_scale=row(pool_scale),
        conv_w=conv_w[l].astype(F32), conv_b=row(conv_b),
        lru_wa=_block_diag(lru_wa[l]).astype(BF16), lru_ba=row(lru_ba),
        lru_wx=_block_diag(lru_wx[l]).astype(BF16), lru_bx=row(lru_bx),
        lru_lambda=row(lru_lambda),
        w_branch=w_branch[l].astype(BF16), w_out=w_out[l].astype(BF16),
        ln1_g=row(ln1_g), ln1_b=row(ln1_b),
        router_w_t=router_w[l].T.astype(BF16), router_bias=router_bias[l].reshape(-1, 1).astype(F32),
        sh_w1=sh_w1[l].astype(BF16), sh_w3=sh_w3[l].astype(BF16), sh_w2=sh_w2[l].astype(BF16),
        ln2_g=row(ln2_g), ln2_b=row(ln2_b),
    )


def kernel(x, positions, w_in, b_gate, sinks, pool_w, pool_scale, conv_w, conv_b, lru_wa, lru_ba, lru_wx, lru_bx, lru_lambda, w_branch, w_out, ln1_g, ln1_b, router_w, router_bias, exp_w1, exp_w3, exp_w2, sh_w1, sh_w3, sh_w2, ln2_g, ln2_b):
    batch, seq_len, d = x.shape
    assert d == D_MODEL and seq_len % max(ATTN_TQ, MIX_TB) == 0 and seq_len % min(PROJ_TM, seq_len) == 0
    t = batch * seq_len
    cos_all, sin_all = _rope_tables(positions)
    x_all = x.reshape(t, d).astype(F32)
    xb_all = x_all.astype(BF16)
    seqs = []
    for s in range(batch):
        rows = slice(s * seq_len, (s + 1) * seq_len)
        seqs.append(dict(xf=x_all, xb=xb_all, row0=s * seq_len, cos=cos_all[rows], sin=sin_all[rows]))
    for l in range(DEPTH):
        p = _layer_params(l, b_gate, sinks, pool_w, pool_scale, conv_w, conv_b, lru_wa, lru_ba,
                          lru_wx, lru_bx, lru_lambda, w_branch, w_out, ln1_g, ln1_b, router_w,
                          router_bias, sh_w1, sh_w3, sh_w2, ln2_g, ln2_b)
        for sq in seqs:
            proj = _in_proj(sq["xb"], sq["row0"], seq_len, w_in, l)
            ya = _attention(proj, sq["cos"], sq["sin"], p["sinks"], seq_len)
            yp, yl = _mixers(proj, p, 1, seq_len)
            sq["xf"], sq["xb"], xp = _merge(proj, ya, yp, yl, sq["xf"], sq["row0"], p)
            sq["row0"] = 0
            sq["route"] = _route(sq["xb"], p)
            sq["xs"] = _sc_dispatch(xp, sq["route"]["ids"], sq["route"]["n_rows"])
        for sq in seqs:
            ys = _experts(sq["route"], sq["xs"], exp_w1, exp_w3, exp_w2, l)
            sq["yg"] = _sc_gather(ys, sq["route"]["ids"])
        out = None
        for s, sq in enumerate(seqs):
            if l + 1 < DEPTH:
                sq["xf"], sq["xb"] = _combine(sq["route"]["gate_t"], sq["xf"], sq["xb"], sq["yg"], p)
            else:
                out, _ = _combine(sq["route"]["gate_t"], sq["xf"], sq["xb"], sq["yg"], p,
                                  out_rows=t, out_row0=s * seq_len, out_buf=out)
    return out.reshape(batch, seq_len, d).astype(x.dtype)
```

```python
import functools

import jax
import jax.numpy as jnp
from jax import lax
from jax.experimental import pallas as pl
from jax.experimental.pallas import tpu as pltpu
from jax.experimental.pallas import tpu_sc as plsc

F32 = jnp.float32
BF16 = jnp.bfloat16
I32 = jnp.int32
U32 = jnp.uint32

D_MODEL = 2048
DEPTH = 2
CHUNK = 64
HEAD_DIM = 64
HALF_HEAD = HEAD_DIM // 2
N_HEADS = 16
N_KV_HEADS = 4
KV_GROUP = N_HEADS // N_KV_HEADS
ATTN_WIDTH = N_HEADS * HEAD_DIM
KV_WIDTH = N_KV_HEADS * HEAD_DIM
WINDOW_CHUNKS = 2
ROPE_THETA = 10000.0
POOL_WINDOWS = (2, 4, 8, 16)
POOL_WIDTH = 1024
POOL_GROUP = POOL_WIDTH // len(POOL_WINDOWS)
LRU_WIDTH = 1024
LRU_BLOCKS = 16
LRU_BLOCK = LRU_WIDTH // LRU_BLOCKS
CONV_WIDTH = 4
LRU_C = 8.0
N_BRANCH = 3
GATE_WIDTH = N_BRANCH * D_MODEL
IN_WIDTH = ATTN_WIDTH + 2 * KV_WIDTH + POOL_WIDTH + 2 * LRU_WIDTH + GATE_WIDTH
N_EXPERTS = 64
TOP_K = 8
D_EXPERT = 512
ROUTED_SCALE = 2.5
LN_EPS = 1e-5
DEEPNORM_ALPHA = (2 * DEPTH) ** 0.25

COL_GATE = 0
COL_Q = GATE_WIDTH
COL_POOL = COL_Q + ATTN_WIDTH
COL_LRU = COL_POOL + POOL_WIDTH
COL_GELU = COL_LRU + LRU_WIDTH
COL_K = COL_GELU + LRU_WIDTH
COL_V = COL_K + KV_WIDTH

LANES = 128
SUBLANES = 8
MXU_DIM = 256
VMEM_LIMIT_BYTES = 56 * 1024 * 1024

PROJ_TM = 1024
PROJ_TN = 1536
ATTN_TQ = 512
ATTN_HALO = WINDOW_CHUNKS * CHUNK
MIX_TB = 512
POOL_HALO = 16
CONV_HALO = 8
MERGE_TM = 256
ROUTER_TM = 512
DEST_TM = 2048
EXPERT_BM = 512
COMBINE_TM = 256
PROJ_SRC = 512
NEG_BIG = -1e30

SUB_WORDS = 256
SUB_COLS = 2 * SUB_WORDS
SUB_ROWS = D_MODEL // SUB_COLS
SC_WINDOW = 128


def _cparams(*sem):
    return pltpu.CompilerParams(dimension_semantics=sem, vmem_limit_bytes=VMEM_LIMIT_BYTES)


def _resident(shape, index_map):
    return pl.BlockSpec(shape, index_map, pipeline_mode=pl.Buffered(1))


def _pack_rows(lo, hi):
    lo_u = lax.bitcast_convert_type(lo.astype(BF16).astype(F32), U32)
    hi_u = lax.bitcast_convert_type(hi.astype(BF16).astype(F32), U32)
    return (lo_u >> 16) | (hi_u & jnp.uint32(0xFFFF0000))


def _unpack_rows(w):
    lo = lax.bitcast_convert_type(w << 16, F32)
    hi = lax.bitcast_convert_type(w & jnp.uint32(0xFFFF0000), F32)
    return lo, hi


def _piece_cols(j):
    return (slice(j * SUB_COLS, j * SUB_COLS + SUB_WORDS),
            slice(j * SUB_COLS + SUB_WORDS, (j + 1) * SUB_COLS))


def _store_pieces(ref, y):
    for j in range(SUB_ROWS):
        lo_cols, hi_cols = _piece_cols(j)
        ref[j] = _pack_rows(y[:, lo_cols], y[:, hi_cols])


def _layer_norm(y, g, b):
    mu = jnp.mean(y, axis=-1, keepdims=True)
    d = y - mu
    var = jnp.mean(d * d, axis=-1, keepdims=True)
    return d * lax.rsqrt(var + LN_EPS) * g + b


def _rope_kernel(pos_ref, inv_ref, cos_ref, sin_ref):
    ang = pos_ref[...].astype(F32) * inv_ref[...]
    lane = lax.broadcasted_iota(I32, ang.shape, 1)
    first_half = (lane % HEAD_DIM) < HALF_HEAD
    cos_ref[...] = jnp.cos(ang)
    s = jnp.sin(ang)
    sin_ref[...] = jnp.where(first_half, -s, s)


def _rope_tables(positions):
    t = positions.size
    tm = min(1024, t)
    inv_freq = 1.0 / (ROPE_THETA ** (jnp.arange(0, HEAD_DIM, 2, dtype=F32) / HEAD_DIM))
    inv = jnp.tile(inv_freq, LANES // HALF_HEAD).reshape(1, LANES)
    return pl.pallas_call(
        _rope_kernel,
        grid=(t // tm,),
        in_specs=[pl.BlockSpec((tm, 1), lambda i: (i, 0)),
                  pl.BlockSpec((1, LANES), lambda i: (0, 0))],
        out_specs=[pl.BlockSpec((tm, LANES), lambda i: (i, 0))] * 2,
        out_shape=[jax.ShapeDtypeStruct((t, LANES), F32)] * 2,
        compiler_params=_cparams("parallel"),
        name="rope_tables",
    )(positions.reshape(t, 1), inv)


def _proj_source_blocks():
    ref_order = (("q", ATTN_WIDTH), ("k", KV_WIDTH), ("v", KV_WIDTH), ("pool", POOL_WIDTH),
                 ("lru", LRU_WIDTH), ("gelu", LRU_WIDTH), ("gate", GATE_WIDTH))
    start, col = {}, 0
    for name, width in ref_order:
        start[name] = col
        col += width
    assert start["v"] == start["k"] + KV_WIDTH and 2 * KV_WIDTH == PROJ_SRC
    out_order = (("gate", GATE_WIDTH), ("q", ATTN_WIDTH), ("pool", POOL_WIDTH), ("lru", LRU_WIDTH),
                 ("gelu", LRU_WIDTH), ("k", 2 * KV_WIDTH))
    blocks = []
    for name, width in out_order:
        assert start[name] % PROJ_SRC == 0 and width % PROJ_SRC == 0
        blocks += [start[name] // PROJ_SRC + c for c in range(width // PROJ_SRC)]
    return blocks


def _in_proj_kernel(src_ref, x_ref, *refs):
    w_refs, o_ref, wb = refs[:-2], refs[-2], refs[-1]

    @pl.when(pl.program_id(1) == 0)
    def _():
        for c, w_ref in enumerate(w_refs):
            wb[:, c * PROJ_SRC:(c + 1) * PROJ_SRC] = w_ref[0].astype(BF16)

    o_ref[...] = jnp.dot(x_ref[...], wb[...], preferred_element_type=F32).astype(o_ref.dtype)


def _in_proj(xb, xb_row0, rows, w_in, layer):
    d = xb.shape[1]
    n = w_in.shape[2]
    tm = min(PROJ_TM, rows)
    assert xb_row0 % tm == 0
    per = PROJ_TN // PROJ_SRC
    src = jnp.asarray(_proj_source_blocks(), I32)

    def w_spec(c):
        return pl.BlockSpec((1, d, PROJ_SRC), lambda j, i, s: (layer, 0, s[per * j + c]))

    grid_spec = pltpu.PrefetchScalarGridSpec(
        num_scalar_prefetch=1,
        grid=(n // PROJ_TN, rows // tm),
        in_specs=[pl.BlockSpec((tm, d), lambda j, i, s: (i + xb_row0 // tm, 0))]
        + [w_spec(c) for c in range(per)],
        out_specs=pl.BlockSpec((tm, PROJ_TN), lambda j, i, s: (i, j)),
        scratch_shapes=[pltpu.VMEM((d, PROJ_TN), BF16)],
    )
    return pl.pallas_call(
        _in_proj_kernel,
        grid_spec=grid_spec,
        out_shape=jax.ShapeDtypeStruct((rows, n), BF16),
        compiler_params=_cparams("arbitrary", "arbitrary"),
        name="in_proj",
    )(src, xb, *([w_in] * per))


def _rope(x, cos, sin):
    w = x.shape[1]
    reps = w // LANES
    c = jnp.concatenate([cos] * reps, axis=1)
    s = jnp.concatenate([sin] * reps, axis=1)
    lane = lax.broadcasted_iota(I32, x.shape, 1)
    first_half = (lane % HEAD_DIM) < HALF_HEAD
    rot = jnp.where(first_half, pltpu.roll(x, w - HALF_HEAD, 1), pltpu.roll(x, HALF_HEAD, 1))
    return x * c + rot * s


def _attn_kernel(q_ref, k_ref, v_ref, kh_ref, vh_ref, cq_ref, sq_ref, ch_ref, sh_ref, sink_ref,
                 o_ref, s_buf, p_buf, r_buf, *, steps_per_seq):
    first = (pl.program_id(0) % steps_per_seq) == 0
    q = _rope(q_ref[...].astype(F32), cq_ref[...], sq_ref[...]) * (HEAD_DIM ** -0.5)
    q = q.astype(BF16)
    k_own = _rope(k_ref[...].astype(F32), cq_ref[...], sq_ref[...])
    k_halo = _rope(kh_ref[...].astype(F32), ch_ref[...], sh_ref[...])
    k_all = jnp.concatenate([k_halo, k_own], axis=0).astype(BF16)
    v_all = jnp.concatenate([vh_ref[...], v_ref[...]], axis=0)
    span = (WINDOW_CHUNKS + 1) * CHUNK
    rows = KV_GROUP * CHUNK
    chunk_rows = N_KV_HEADS * rows
    n_chunks = q.shape[0] // CHUNK
    win_chunk = lax.broadcasted_iota(I32, (rows, span), 1) // CHUNK
    oldest = jnp.where(first, 0, -WINDOW_CHUNKS)

    def tile_rows(c, g):
        return slice(c * chunk_rows + g * rows, c * chunk_rows + (g + 1) * rows)

    def kv_cols(g):
        return slice(g * HEAD_DIM, (g + 1) * HEAD_DIM)

    for c in range(n_chunks):
        q_c = q[c * CHUNK:(c + 1) * CHUNK, :]
        k_win = k_all[c * CHUNK:c * CHUNK + span, :]
        for g in range(N_KV_HEADS):
            qs = jnp.concatenate(
                [q_c[:, (g * KV_GROUP + i) * HEAD_DIM:(g * KV_GROUP + i + 1) * HEAD_DIM]
                 for i in range(KV_GROUP)], axis=0)
            s = lax.dot_general(qs, k_win[:, kv_cols(g)], (((1,), (1,)), ((), ())),
                                preferred_element_type=F32)
            if c < WINDOW_CHUNKS:
                s = jnp.where(win_chunk + (c - WINDOW_CHUNKS) >= oldest, s, NEG_BIG)
            s_buf[tile_rows(c, g), :] = s
    sink = sink_ref[...]
    for c in range(n_chunks):
        blk = slice(c * chunk_rows, (c + 1) * chunk_rows)
        s = s_buf[blk, :]
        m = jnp.maximum(jnp.max(s, axis=1, keepdims=True), sink)
        p = jnp.exp(s - m)
        p_buf[blk, :] = p.astype(BF16)
        r_buf[blk, :] = 1.0 / (jnp.sum(p, axis=1, keepdims=True) + jnp.exp(sink - m))
    for c in range(n_chunks):
        v_win = v_all[c * CHUNK:c * CHUNK + span, :]
        heads = []
        for g in range(N_KV_HEADS):
            o = jnp.dot(p_buf[tile_rows(c, g), :], v_win[:, kv_cols(g)], preferred_element_type=F32)
            o = o * r_buf[tile_rows(c, g), :]
            heads += [o[i * CHUNK:(i + 1) * CHUNK, :] for i in range(KV_GROUP)]
        o_ref[c * CHUNK:(c + 1) * CHUNK, :] = jnp.concatenate(heads, axis=1).astype(o_ref.dtype)


def _attention(proj, cos, sin, sinks, seq_len):
    t = proj.shape[0]
    tq = ATTN_TQ
    steps_per_seq = seq_len // tq
    halo_per_tq = tq // ATTN_HALO
    q_blk = COL_Q // ATTN_WIDTH
    k_blk = COL_K // KV_WIDTH
    v_blk = COL_V // KV_WIDTH

    def halo_row(i):
        return jnp.maximum(i * halo_per_tq - 1, 0)

    span = (WINDOW_CHUNKS + 1) * CHUNK
    tile_rows = (tq // CHUNK) * N_HEADS * CHUNK
    sink_rows = jnp.repeat(sinks, CHUNK).reshape(N_HEADS * CHUNK, 1)
    return pl.pallas_call(
        functools.partial(_attn_kernel, steps_per_seq=steps_per_seq),
        grid=(t // tq,),
        in_specs=[
            pl.BlockSpec((tq, ATTN_WIDTH), lambda i: (i, q_blk)),
            pl.BlockSpec((tq, KV_WIDTH), lambda i: (i, k_blk)),
            pl.BlockSpec((tq, KV_WIDTH), lambda i: (i, v_blk)),
            pl.BlockSpec((ATTN_HALO, KV_WIDTH), lambda i: (halo_row(i), k_blk)),
            pl.BlockSpec((ATTN_HALO, KV_WIDTH), lambda i: (halo_row(i), v_blk)),
            pl.BlockSpec((tq, LANES), lambda i: (i, 0)),
            pl.BlockSpec((tq, LANES), lambda i: (i, 0)),
            pl.BlockSpec((ATTN_HALO, LANES), lambda i: (halo_row(i), 0)),
            pl.BlockSpec((ATTN_HALO, LANES), lambda i: (halo_row(i), 0)),
            pl.BlockSpec((N_HEADS * CHUNK, 1), lambda i: (0, 0)),
        ],
        out_specs=pl.BlockSpec((tq, ATTN_WIDTH), lambda i: (i, 0)),
        out_shape=jax.ShapeDtypeStruct((t, ATTN_WIDTH), BF16),
        scratch_shapes=[pltpu.VMEM((tile_rows, span), F32), pltpu.VMEM((tile_rows, span), BF16),
                        pltpu.VMEM((tile_rows, 1), F32)],
        compiler_params=_cparams("parallel"),
        name="swa_attention",
    )(proj, proj, proj, proj, proj, cos, sin, cos, sin, sink_rows)


def _mix_kernel(up_ref, ul_ref, ug_ref, pw_ref, ps_ref, cw_ref, cb_ref, wa_ref, ba_ref, wx_ref,
                bx_ref, lam_ref, yp_ref, yl_ref, halo_p, halo_l, h_carry):
    s = pl.program_id(1)
    tb = up_ref.shape[0]

    @pl.when(s == 0)
    def _():
        halo_p[...] = jnp.zeros_like(halo_p)
        halo_l[...] = jnp.zeros_like(halo_l)
        h_carry[...] = jnp.zeros_like(h_carry)

    u = up_ref[...].astype(F32)
    ext = jnp.concatenate([halo_p[...], u], axis=0)
    sums = []
    ws = ext
    for g, win in enumerate(POOL_WINDOWS):
        shift = win // 2
        ws = ws + pltpu.roll(ws, shift, 0)
        sums.append(ws[POOL_HALO:, :POOL_GROUP])
        if g + 1 < len(POOL_WINDOWS):
            ws = ws[:, POOL_GROUP:]
    t_idx = (s * tb + lax.broadcasted_iota(I32, (tb, 1), 0)).astype(F32)
    for g, win in enumerate(POOL_WINDOWS):
        cols = slice(g * POOL_GROUP, (g + 1) * POOL_GROUP)
        count = jnp.minimum(t_idx + 1.0, float(win))
        pooled = sums[g] / count - u[:, cols]
        y = jnp.dot(pooled.astype(BF16), pw_ref[g], preferred_element_type=F32)
        yp_ref[:, cols] = (y * ps_ref[:, cols]).astype(yp_ref.dtype)
    halo_p[...] = u[tb - POOL_HALO:, :]

    ul = ul_ref[...].astype(F32)
    extl = jnp.concatenate([halo_l[...], ul], axis=0)
    xc = cb_ref[...] + cw_ref[CONV_WIDTH - 1:CONV_WIDTH, :] * ul
    for j in range(1, CONV_WIDTH):
        w_j = cw_ref[CONV_WIDTH - 1 - j:CONV_WIDTH - j, :]
        xc = xc + w_j * pltpu.roll(extl, j, 0)[CONV_HALO:, :]
    halo_l[...] = ul[tb - CONV_HALO:, :]
    xcb = xc.astype(BF16)
    r_parts, i_parts = [], []
    for j in range(LRU_WIDTH // MXU_DIM):
        cols = slice(j * MXU_DIM, (j + 1) * MXU_DIM)
        r_parts.append(jnp.dot(xcb[:, cols], wa_ref[j], preferred_element_type=F32))
        i_parts.append(jnp.dot(xcb[:, cols], wx_ref[j], preferred_element_type=F32))
    r = jax.nn.sigmoid(jnp.concatenate(r_parts, axis=1) + ba_ref[...])
    ig = jax.nn.sigmoid(jnp.concatenate(i_parts, axis=1) + bx_ref[...])
    z = -lam_ref[...]
    softplus = jnp.maximum(z, 0.0) + jnp.log1p(jnp.exp(-jnp.abs(z)))
    log_a = (-LRU_C) * r * softplus
    a = jnp.exp(log_a)
    one_minus_a2 = 1.0 - a * a
    root = jnp.where(one_minus_a2 > 0.0, one_minus_a2 * lax.rsqrt(one_minus_a2), 0.0)
    b = root * (ig * xc)
    row_in_group = lax.broadcasted_iota(I32, a.shape, 0) % SUBLANES
    d = 1
    while d < SUBLANES:
        keep = row_in_group >= d
        a_prev = jnp.where(keep, pltpu.roll(a, d, 0), 1.0)
        b_prev = jnp.where(keep, pltpu.roll(b, d, 0), 0.0)
        b = a * b_prev + b
        a = a * a_prev
        d *= 2
    gelu = jax.nn.gelu(ug_ref[...].astype(F32))
    h_prev = h_carry[...]
    pair = 2 * SUBLANES
    for g in range(tb // pair):
        hs = []
        for half in range(2):
            rows = slice(g * pair + half * SUBLANES, g * pair + (half + 1) * SUBLANES)
            h = a[rows, :] * h_prev + b[rows, :]
            h_prev = h[SUBLANES - 1:SUBLANES, :]
            hs.append(h)
        rows = slice(g * pair, (g + 1) * pair)
        yl_ref[rows, :] = (jnp.concatenate(hs, axis=0) * gelu[rows, :]).astype(yl_ref.dtype)
    h_carry[...] = h_prev


def _block_diag(w):
    per = MXU_DIM // LRU_BLOCK
    w = w.reshape(LRU_BLOCKS // per, per, LRU_BLOCK, LRU_BLOCK)
    eye = jnp.eye(per, dtype=w.dtype)
    out = jnp.einsum("gpij,pq->gpiqj", w, eye)
    return out.reshape(LRU_BLOCKS // per, MXU_DIM, MXU_DIM)


def _mixers(proj, p, batch, seq_len):
    t = proj.shape[0]
    tb = min(MIX_TB, seq_len)
    steps = seq_len // tb

    def rows(bi, si):
        return bi * steps + si

    def col_spec(col):
        blk = col // POOL_WIDTH
        return pl.BlockSpec((tb, POOL_WIDTH), lambda bi, si: (rows(bi, si), blk))

    def const(shape):
        nd = len(shape)
        return pl.BlockSpec(shape, lambda bi, si: (0,) * nd)

    out_spec = pl.BlockSpec((tb, POOL_WIDTH), lambda bi, si: (rows(bi, si), 0))
    return pl.pallas_call(
        _mix_kernel,
        grid=(batch, steps),
        in_specs=[col_spec(COL_POOL), col_spec(COL_LRU), col_spec(COL_GELU),
                  const((len(POOL_WINDOWS), POOL_GROUP, POOL_GROUP)), const((1, POOL_WIDTH)),
                  const((CONV_WIDTH, LRU_WIDTH)), const((1, LRU_WIDTH)),
                  const((LRU_WIDTH // MXU_DIM, MXU_DIM, MXU_DIM)), const((1, LRU_WIDTH)),
                  const((LRU_WIDTH // MXU_DIM, MXU_DIM, MXU_DIM)), const((1, LRU_WIDTH)),
                  const((1, LRU_WIDTH))],
        out_specs=[out_spec, out_spec],
        out_shape=[jax.ShapeDtypeStruct((t, POOL_WIDTH), BF16),
                   jax.ShapeDtypeStruct((t, LRU_WIDTH), BF16)],
        scratch_shapes=[pltpu.VMEM((POOL_HALO, POOL_WIDTH), F32),
                        pltpu.VMEM((CONV_HALO, LRU_WIDTH), F32),
                        pltpu.VMEM((1, LRU_WIDTH), F32)],
        compiler_params=_cparams("arbitrary", "arbitrary"),
        name="pool_rglru",
    )(proj, proj, proj, p["pool_w"], p["pool_scale"], p["conv_w"], p["conv_b"],
      p["lru_wa"], p["lru_ba"], p["lru_wx"], p["lru_bx"], p["lru_lambda"])


def _merge_kernel(gl_ref, ya_ref, yp_ref, yl_ref, x_ref, bg_ref, wb_ref, wo_ref, g_ref, b_ref,
                  xo_ref, xb_ref, xp_ref):
    merged = None
    for j, y_ref in enumerate((ya_ref, yp_ref, yl_ref)):
        cols = slice(j * D_MODEL, (j + 1) * D_MODEL)
        z = jnp.dot(y_ref[...], wb_ref[j], preferred_element_type=F32)
        gate = jax.nn.sigmoid(gl_ref[:, cols].astype(F32) + bg_ref[:, cols])
        merged = gate * z if merged is None else merged + gate * z
    h = jnp.dot(merged.astype(BF16), wo_ref[...], preferred_element_type=F32)
    y = _layer_norm(DEEPNORM_ALPHA * x_ref[...] + h, g_ref[...], b_ref[...])
    xo_ref[...] = y
    xb_ref[...] = y.astype(BF16)
    _store_pieces(xp_ref, y)


def _merge(proj, ya, yp, yl, x, x_row0, p):
    t = proj.shape[0]
    tm = min(MERGE_TM, t)
    assert x_row0 % tm == 0
    row = lambda i: (i, 0)
    return pl.pallas_call(
        _merge_kernel,
        grid=(t // tm,),
        in_specs=[pl.BlockSpec((tm, GATE_WIDTH), row),
                  pl.BlockSpec((tm, ATTN_WIDTH), row),
                  pl.BlockSpec((tm, POOL_WIDTH), row),
                  pl.BlockSpec((tm, LRU_WIDTH), row),
                  pl.BlockSpec((tm, D_MODEL), lambda i: (i + x_row0 // tm, 0)),
                  _resident((1, GATE_WIDTH), lambda i: (0, 0)),
                  _resident((N_BRANCH, ATTN_WIDTH, D_MODEL), lambda i: (0, 0, 0)),
                  _resident((D_MODEL, D_MODEL), lambda i: (0, 0)),
                  _resident((1, D_MODEL), lambda i: (0, 0)),
                  _resident((1, D_MODEL), lambda i: (0, 0))],
        out_specs=[pl.BlockSpec((tm, D_MODEL), row), pl.BlockSpec((tm, D_MODEL), row),
                   pl.BlockSpec((SUB_ROWS, tm, SUB_WORDS), lambda i: (0, i, 0))],
        out_shape=[jax.ShapeDtypeStruct((t, D_MODEL), F32),
                   jax.ShapeDtypeStruct((t, D_MODEL), BF16),
                   jax.ShapeDtypeStruct((SUB_ROWS, t, SUB_WORDS), U32)],
        compiler_params=_cparams("parallel"),
        name="merge_ln",
    )(proj, ya, yp, yl, x, p["b_gate"], p["w_branch"], p["w_out"], p["ln1_g"], p["ln1_b"])


def _router_kernel(x_ref, rw_ref, rb_ref, idx_ref, gate_ref, rank_ref, cnt_ref, carry):
    @pl.when(pl.program_id(0) == 0)
    def _():
        carry[...] = jnp.zeros_like(carry)

    tm = x_ref.shape[0]
    logits = lax.dot_general(rw_ref[...], x_ref[...], (((1,), (1,)), ((), ())),
                             preferred_element_type=F32)
    scores = jax.nn.sigmoid(logits)
    sel = scores + rb_ref[...]
    expert = lax.broadcasted_iota(I32, scores.shape, 0)
    idxs, tops = [], []
    chosen = jnp.zeros(scores.shape, F32)
    for _ in range(TOP_K):
        best = jnp.max(sel, axis=0, keepdims=True)
        ik = jnp.min(jnp.where(sel == best, expert, N_EXPERTS), axis=0, keepdims=True)
        hit = expert == ik
        tops.append(jnp.sum(jnp.where(hit, scores, 0.0), axis=0, keepdims=True))
        idxs.append(ik)
        chosen = jnp.where(hit, 1.0, chosen)
        sel = jnp.where(hit, -jnp.inf, sel)
    top = jnp.concatenate(tops, axis=0)
    idx = jnp.concatenate(idxs, axis=0)
    gates = top / jnp.sum(top, axis=0, keepdims=True) * ROUTED_SCALE
    gate_ref[...] = jnp.transpose(gates)
    idx_ref[...] = idx
    earlier = (lax.broadcasted_iota(I32, (tm, tm), 0) < lax.broadcasted_iota(I32, (tm, tm), 1))
    before = jnp.dot(chosen.astype(BF16), jnp.where(earlier, 1.0, 0.0).astype(BF16),
                     preferred_element_type=F32)
    base = before + carry[:, 0:1]
    ranks = [jnp.sum(jnp.where(expert == idxs[k], base, 0.0), axis=0, keepdims=True)
             for k in range(TOP_K)]
    rank_ref[...] = jnp.concatenate(ranks, axis=0).astype(I32)
    carry[...] = carry[...] + jnp.sum(chosen, axis=1, keepdims=True)
    cnt_ref[...] = carry[...].astype(I32)


def _router(x, rw_t, rbias):
    t = x.shape[0]
    tm = min(ROUTER_TM, t)
    tok = lambda i: (0, i)
    return pl.pallas_call(
        _router_kernel,
        grid=(t // tm,),
        in_specs=[pl.BlockSpec((tm, D_MODEL), lambda i: (i, 0)),
                  pl.BlockSpec((N_EXPERTS, D_MODEL), lambda i: (0, 0)),
                  pl.BlockSpec((N_EXPERTS, 1), lambda i: (0, 0))],
        out_specs=[pl.BlockSpec((TOP_K, tm), tok), pl.BlockSpec((tm, TOP_K), lambda i: (i, 0)),
                   pl.BlockSpec((TOP_K, tm), tok),
                   pl.BlockSpec((N_EXPERTS, LANES), lambda i: (0, 0))],
        out_shape=[jax.ShapeDtypeStruct((TOP_K, t), I32), jax.ShapeDtypeStruct((t, TOP_K), F32),
                   jax.ShapeDtypeStruct((TOP_K, t), I32),
                   jax.ShapeDtypeStruct((N_EXPERTS, LANES), I32)],
        scratch_shapes=[pltpu.VMEM((N_EXPERTS, LANES), F32)],
        compiler_params=_cparams("arbitrary"),
        name="router_topk",
    )(x, rw_t, rbias)


def _dest_kernel(start_ref, idx_ref, rank_ref, ids_ref, *, n_rows):
    idx = idx_ref[...]
    dest = rank_ref[...] + pl.program_id(0) * n_rows
    for e in range(N_EXPERTS):
        dest = dest + jnp.where(idx == e, start_ref[e], 0)
    ids_ref[...] = dest


def _piece_ids(pad_start, idx, rank, n_rows):
    t = idx.shape[1]
    tm = min(DEST_TM, t)
    spec = pl.BlockSpec((TOP_K, tm), lambda j, i, s: (0, i))
    return pl.pallas_call(
        functools.partial(_dest_kernel, n_rows=n_rows),
        grid_spec=pltpu.PrefetchScalarGridSpec(
            num_scalar_prefetch=1, grid=(SUB_ROWS, t // tm), in_specs=[spec, spec],
            out_specs=pl.BlockSpec((TOP_K, tm), lambda j, i, s: (j, i))),
        out_shape=jax.ShapeDtypeStruct((SUB_ROWS * TOP_K, t), I32),
        compiler_params=_cparams("parallel", "parallel"),
        name="piece_ids",
    )(pad_start, idx, rank)


def _swiglu(xb, w1_ref, w3_ref, w2_ref):
    a = jnp.dot(xb, w1_ref[...], preferred_element_type=F32)
    g = jnp.dot(xb, w3_ref[...], preferred_element_type=F32)
    hid = (a * jax.nn.sigmoid(a) * g).astype(BF16)
    return jnp.dot(hid, w2_ref[...], preferred_element_type=F32)


def _sc_mesh():
    return plsc.VectorSubcoreMesh(core_axis_name="core", subcore_axis_name="subcore")


def _sc_dispatch(xp, ids, n_rows):
    t = xp.shape[1]
    per_plane = t // SC_WINDOW

    @pl.kernel(out_type=jax.ShapeDtypeStruct((SUB_ROWS * n_rows, SUB_WORDS), U32), mesh=_sc_mesh(),
               scratch_types=[])
    def scatter(x_hbm, i_hbm, o_hbm):
        def body(x_vmem, i_vmem):
            for k in range(TOP_K):
                pltpu.sync_copy(x_vmem, o_hbm.at[i_vmem.at[k]])

        pltpu.emit_pipeline(
            body,
            grid=(SUB_ROWS * per_plane,),
            in_specs=[pl.BlockSpec((SC_WINDOW, SUB_WORDS), lambda w: (w, 0)),
                      pl.BlockSpec((TOP_K, SC_WINDOW), lambda w: (w // per_plane, w % per_plane))],
            out_specs=[],
            core_axis_name=("core", "subcore"),
            dimension_semantics=(pltpu.PARALLEL,),
        )(x_hbm, i_hbm)

    out = scatter(xp.reshape(SUB_ROWS * t, SUB_WORDS), ids)
    return out.reshape(SUB_ROWS, n_rows, SUB_WORDS)


def _sc_gather(ys, ids):
    n_rows = ys.shape[1]
    t = ids.shape[1]
    per_row = t // SC_WINDOW
    pieces = SUB_ROWS * TOP_K * t

    @pl.kernel(out_type=jax.ShapeDtypeStruct((pieces, SUB_WORDS), U32), mesh=_sc_mesh(),
               scratch_types=[])
    def gather(y_hbm, i_hbm, o_hbm):
        def body(i_vmem, o_vmem):
            pltpu.sync_copy(y_hbm.at[i_vmem.at[0]], o_vmem)

        pltpu.emit_pipeline(
            body,
            grid=(pieces // SC_WINDOW,),
            in_specs=[pl.BlockSpec((1, SC_WINDOW), lambda w: (w // per_row, w % per_row))],
            out_specs=[pl.BlockSpec((SC_WINDOW, SUB_WORDS), lambda w: (w, 0))],
            core_axis_name=("core", "subcore"),
            dimension_semantics=(pltpu.PARALLEL,),
        )(i_hbm, o_hbm)

    out = gather(ys.reshape(SUB_ROWS * n_rows, SUB_WORDS), ids)
    return out.reshape(SUB_ROWS, TOP_K * t, SUB_WORDS)


def _expert_kernel(bexp_ref, blive_ref, nused_ref, bfirst_ref, bslot_ref, bnext_ref, xs_ref, w1_hbm,
                   w3_hbm, w2_hbm, ys_ref, st1, st3, st2, w1b, w3b, w2b, xb, sem, *, layer):
    b = pl.program_id(0)
    used = b < nused_ref[0]

    def weight_copies(e, slot):
        return (pltpu.make_async_copy(w1_hbm.at[layer, e], st1.at[slot], sem.at[slot, 0]),
                pltpu.make_async_copy(w3_hbm.at[layer, e], st3.at[slot], sem.at[slot, 1]),
                pltpu.make_async_copy(w2_hbm.at[layer, e], st2.at[slot], sem.at[slot, 2]))

    @pl.when(b == 0)
    def _():
        for cp in weight_copies(bexp_ref[0], 0):
            cp.start()

    @pl.when(jnp.logical_and(used, bfirst_ref[b] == 1))
    def _():
        slot = bslot_ref[b]
        for cp in weight_copies(bexp_ref[b], slot):
            cp.wait()

        @pl.when(bnext_ref[b] >= 0)
        def _():
            for cp in weight_copies(bnext_ref[b], 1 - slot):
                cp.start()

        w1b[...] = st1[slot].astype(BF16)
        w3b[...] = st3[slot].astype(BF16)
        w2b[...] = st2[slot].astype(BF16)

    bm = xb.shape[0]
    live_rows = blive_ref[b]

    def run(m):
        live = lax.broadcasted_iota(I32, (m, SUB_WORDS), 0) < live_rows
        for j in range(SUB_ROWS):
            lo, hi = _unpack_rows(jnp.where(live, xs_ref[j, :m, :], jnp.uint32(0)))
            lo_cols, hi_cols = _piece_cols(j)
            xb[:m, lo_cols] = lo.astype(BF16)
            xb[:m, hi_cols] = hi.astype(BF16)
        y = _swiglu(xb[:m, :], w1b, w3b, w2b)
        for j in range(SUB_ROWS):
            lo_cols, hi_cols = _piece_cols(j)
            ys_ref[j, :m, :] = _pack_rows(y[:, lo_cols], y[:, hi_cols])

    @pl.when(jnp.logical_and(used, live_rows > bm // 2))
    def _():
        run(bm)

    @pl.when(jnp.logical_and(used, live_rows <= bm // 2))
    def _():
        run(bm // 2)
        ys_ref[:, bm // 2:, :] = jnp.zeros((SUB_ROWS, bm - bm // 2, SUB_WORDS), U32)

    @pl.when(jnp.logical_not(used))
    def _():
        ys_ref[...] = jnp.zeros_like(ys_ref)


def _experts(r, xs, w1, w3, w2, layer):
    n_blocks = xs.shape[1] // EXPERT_BM
    n_tables = 6

    def rows_spec(index):
        return pl.BlockSpec((SUB_ROWS, EXPERT_BM, SUB_WORDS), lambda b, *tables: (0, index(b, tables), 0))

    grid_spec = pltpu.PrefetchScalarGridSpec(
        num_scalar_prefetch=n_tables,
        grid=(n_blocks,),
        in_specs=[rows_spec(lambda b, tables: jnp.minimum(b, tables[2][0] - 1)),
                  pl.BlockSpec(memory_space=pl.ANY), pl.BlockSpec(memory_space=pl.ANY),
                  pl.BlockSpec(memory_space=pl.ANY)],
        out_specs=rows_spec(lambda b, tables: b),
        scratch_shapes=[pltpu.VMEM((2, D_MODEL, D_EXPERT), F32), pltpu.VMEM((2, D_MODEL, D_EXPERT), F32),
                        pltpu.VMEM((2, D_EXPERT, D_MODEL), F32),
                        pltpu.VMEM((D_MODEL, D_EXPERT), BF16), pltpu.VMEM((D_MODEL, D_EXPERT), BF16),
                        pltpu.VMEM((D_EXPERT, D_MODEL), BF16), pltpu.VMEM((EXPERT_BM, D_MODEL), BF16),
                        pltpu.SemaphoreType.DMA((2, 3))],
    )
    return pl.pallas_call(
        functools.partial(_expert_kernel, layer=layer),
        grid_spec=grid_spec,
        out_shape=jax.ShapeDtypeStruct(xs.shape, U32),
        compiler_params=_cparams("arbitrary"),
        name="moe_experts",
    )(r["block_exp"], r["block_live"], r["n_used"], r["block_first"], r["block_slot"], r["block_next"],
      xs, w1, w3, w2)


def _combine_kernel(gate_ref, x_ref, xb_ref, s1_ref, s3_ref, s2_ref, g_ref, b_ref, *refs):
    yk_refs = refs[:TOP_K]
    xo_ref, xbo_ref, hbuf = refs[-3:]
    tm = x_ref.shape[0]
    shared = _swiglu(xb_ref[...], s1_ref, s3_ref, s2_ref)
    gates = [jnp.broadcast_to(gate_ref[:, k:k + 1], (tm, SUB_WORDS)) for k in range(TOP_K)]
    for j in range(SUB_ROWS):
        lo_cols, hi_cols = _piece_cols(j)
        acc_lo = shared[:, lo_cols]
        acc_hi = shared[:, hi_cols]
        for k in range(TOP_K):
            lo, hi = _unpack_rows(yk_refs[k][j])
            acc_lo = acc_lo + gates[k] * lo
            acc_hi = acc_hi + gates[k] * hi
        hbuf[:, lo_cols] = acc_lo
        hbuf[:, hi_cols] = acc_hi
    y = _layer_norm(DEEPNORM_ALPHA * x_ref[...] + hbuf[...], g_ref[...], b_ref[...])
    xo_ref[...] = y
    xbo_ref[...] = y.astype(BF16)


def _combine(gate_t, x, xb, yg, p, out_rows=None, out_row0=0, out_buf=None):
    t = x.shape[0]
    tm = min(COMBINE_TM, t)
    steps = t // tm
    row = lambda i: (i, 0)
    assert out_row0 % tm == 0

    def slot_spec(k):
        return pl.BlockSpec((SUB_ROWS, tm, SUB_WORDS), lambda i: (0, k * steps + i, 0))

    in_specs = [pl.BlockSpec((tm, TOP_K), row),
                pl.BlockSpec((tm, D_MODEL), row),
                pl.BlockSpec((tm, D_MODEL), row),
                _resident((D_MODEL, D_EXPERT), lambda i: (0, 0)),
                _resident((D_MODEL, D_EXPERT), lambda i: (0, 0)),
                _resident((D_EXPERT, D_MODEL), lambda i: (0, 0)),
                _resident((1, D_MODEL), lambda i: (0, 0)),
                _resident((1, D_MODEL), lambda i: (0, 0))] + [slot_spec(k) for k in range(TOP_K)]
    args = [gate_t, x, xb, p["sh_w1"], p["sh_w3"], p["sh_w2"], p["ln2_g"], p["ln2_b"]] + [yg] * TOP_K
    aliases = {}
    if out_buf is not None:
        aliases = {len(args): 0}
        in_specs.append(pl.BlockSpec(memory_space=pl.ANY))
        args.append(out_buf)
    return pl.pallas_call(
        _combine_kernel,
        grid=(steps,),
        in_specs=in_specs,
        out_specs=[pl.BlockSpec((tm, D_MODEL), lambda i: (i + out_row0 // tm, 0)),
                   pl.BlockSpec((tm, D_MODEL), row)],
        out_shape=[jax.ShapeDtypeStruct((out_rows or t, D_MODEL), F32),
                   jax.ShapeDtypeStruct((t, D_MODEL), BF16)],
        scratch_shapes=[pltpu.VMEM((tm, D_MODEL), F32)],
        input_output_aliases=aliases,
        compiler_params=_cparams("parallel"),
        name="moe_combine_ln",
    )(*args)


def _n_expert_blocks(t):
    return -(-(t * TOP_K + N_EXPERTS * (EXPERT_BM - 1)) // EXPERT_BM)


def _route(xb, p):
    t = xb.shape[0]
    n_blocks = _n_expert_blocks(t)
    idx, gate_t, rank, cnt = _router(xb, p["router_w_t"], p["router_bias"])
    counts = cnt[:, 0]
    padded = (counts + EXPERT_BM - 1) // EXPERT_BM * EXPERT_BM
    pad_end = jnp.cumsum(padded)
    pad_start = pad_end - padded
    n_used = (pad_end[-1:] // EXPERT_BM).astype(I32)
    block = jnp.arange(n_blocks, dtype=I32)
    block_exp = jnp.minimum(jnp.sum(pad_end[None, :] <= block[:, None] * EXPERT_BM, axis=1),
                            N_EXPERTS - 1).astype(I32)
    expert = jnp.arange(N_EXPERTS, dtype=I32)
    of_block = block_exp[:, None] == expert[None, :]

    def per_block(table):
        return jnp.sum(jnp.where(of_block, table[None, :], 0), axis=1).astype(I32)

    block_live = jnp.clip(per_block(counts) - (block * EXPERT_BM - per_block(pad_start)), 0, EXPERT_BM)
    active = counts > 0
    ordinal = jnp.cumsum(active.astype(I32)) - 1
    later = active[None, :] & (expert[None, :] > expert[:, None])
    next_active = jnp.min(jnp.where(later, expert[None, :], N_EXPERTS), axis=1)
    next_active = jnp.where(next_active < N_EXPERTS, next_active, -1)
    used = block < n_used[0]
    first = used & (block * EXPERT_BM == per_block(pad_start))
    n_rows = n_blocks * EXPERT_BM
    ids = _piece_ids(pad_start.astype(I32), idx, rank, n_rows)
    return dict(ids=ids, gate_t=gate_t, block_exp=block_exp, block_live=block_live.astype(I32),
                n_used=n_used, block_first=first.astype(I32), block_slot=per_block(ordinal) % 2,
                block_next=per_block(next_active), n_rows=n_rows)


def _layer_params(l, b_gate, sinks, pool_w, pool_scale, conv_w, conv_b, lru_wa, lru_ba, lru_wx,
                  lru_bx, lru_lambda, w_branch, w_out, ln1_g, ln1_b, router_w, router_bias,
                  sh_w1, sh_w3, sh_w2, ln2_g, ln2_b):
    row = lambda a: a[l].reshape(1, -1).astype(F32)
    return dict(
        b_gate=row(b_gate), sinks=sinks[l].astype(F32),
        pool_w=pool_w[l].astype(BF16), pool_scale=row(pool_scale),
        conv_w=conv_w[l].astype(F32), conv_b=row(conv_b),
        lru_wa=_block_diag(lru_wa[l]).astype(BF16), lru_ba=row(lru_ba),
        lru_wx=_block_diag(lru_wx[l]).astype(BF16), lru_bx=row(lru_bx),
        lru_lambda=row(lru_lambda),
        w_branch=w_branch[l].astype(BF16), w_out=w_out[l].astype(BF16),
        ln1_g=row(ln1_g), ln1_b=row(ln1_b),
        router_w_t=router_w[l].T.astype(BF16), router_bias=router_bias[l].reshape(-1, 1).astype(F32),
        sh_w1=sh_w1[l].astype(BF16), sh_w3=sh_w3[l].astype(BF16), sh_w2=sh_w2[l].astype(BF16),
        ln2_g=row(ln2_g), ln2_b=row(ln2_b),
    )


def kernel(x, positions, w_in, b_gate, sinks, pool_w, pool_scale, conv_w, conv_b, lru_wa, lru_ba, lru_wx, lru_bx, lru_lambda, w_branch, w_out, ln1_g, ln1_b, router_w, router_bias, exp_w1, exp_w3, exp_w2, sh_w1, sh_w3, sh_w2, ln2_g, ln2_b):
    batch, seq_len, d = x.shape
    assert d == D_MODEL and seq_len % max(ATTN_TQ, MIX_TB) == 0 and seq_len % min(PROJ_TM, seq_len) == 0
    t = batch * seq_len
    cos_all, sin_all = _rope_tables(positions)
    x_all = x.reshape(t, d).astype(F32)
    xb_all = x_all.astype(BF16)
    seqs = []
    for s in range(batch):
        rows = slice(s * seq_len, (s + 1) * seq_len)
        seqs.append(dict(xf=x_all, xb=xb_all, row0=s * seq_len, cos=cos_all[rows], sin=sin_all[rows]))
    for l in range(DEPTH):
        p = _layer_params(l, b_gate, sinks, pool_w, pool_scale, conv_w, conv_b, lru_wa, lru_ba,
                          lru_wx, lru_bx, lru_lambda, w_branch, w_out, ln1_g, ln1_b, router_w,
                          router_bias, sh_w1, sh_w3, sh_w2, ln2_g, ln2_b)
        for sq in seqs:
            proj = _in_proj(sq["xb"], sq["row0"], seq_len, w_in, l)
            ya = _attention(proj, sq["cos"], sq["sin"], p["sinks"], seq_len)
            yp, yl = _mixers(proj, p, 1, seq_len)
            sq["xf"], sq["xb"], xp = _merge(proj, ya, yp, yl, sq["xf"], sq["row0"], p)
            sq["row0"] = 0
            sq["route"] = _route(sq["xb"], p)
            sq["xs"] = _sc_dispatch(xp, sq["route"]["ids"], sq["route"]["n_rows"])
        last = l + 1 == DEPTH
        out = None

        def finish(s):
            nonlocal out
            sq = seqs[s]
            if last:
                out, xb_new = _combine(sq["route"]["gate_t"], sq["xf"], sq["xb"], sq["yg"], p,
                                       out_rows=t, out_row0=s * seq_len, out_buf=out)
            else:
                sq["xf"], xb_new = _combine(sq["route"]["gate_t"], sq["xf"], sq["xb"], sq["yg"], p)
                sq["xb"] = xb_new
            return xb_new

        for s, sq in enumerate(seqs):
            ys = _experts(sq["route"], sq["xs"], exp_w1, exp_w3, exp_w2, l)
            ids = sq["route"]["ids"]
            if s > 0:
                done = finish(s - 1)
                if not last:
                    ids, _ = lax.optimization_barrier((ids, done))
            sq["yg"] = _sc_gather(ys, ids)
        finish(batch - 1)
    return out.reshape(batch, seq_len, d).astype(x.dtype)
```

```python
import functools

import jax
import jax.numpy as jnp
from jax import lax
from jax.experimental import pallas as pl
from jax.experimental.pallas import tpu as pltpu
from jax.experimental.pallas import tpu_sc as plsc

F32 = jnp.float32
BF16 = jnp.bfloat16
I32 = jnp.int32
U32 = jnp.uint32

D_MODEL = 2048
DEPTH = 2
CHUNK = 64
HEAD_DIM = 64
HALF_HEAD = HEAD_DIM // 2
N_HEADS = 16
N_KV_HEADS = 4
KV_GROUP = N_HEADS // N_KV_HEADS
ATTN_WIDTH = N_HEADS * HEAD_DIM
KV_WIDTH = N_KV_HEADS * HEAD_DIM
WINDOW_CHUNKS = 2
ROPE_THETA = 10000.0
POOL_WINDOWS = (2, 4, 8, 16)
POOL_WIDTH = 1024
POOL_GROUP = POOL_WIDTH // len(POOL_WINDOWS)
LRU_WIDTH = 1024
LRU_BLOCKS = 16
LRU_BLOCK = LRU_WIDTH // LRU_BLOCKS
CONV_WIDTH = 4
LRU_C = 8.0
N_BRANCH = 3
GATE_WIDTH = N_BRANCH * D_MODEL
IN_WIDTH = ATTN_WIDTH + 2 * KV_WIDTH + POOL_WIDTH + 2 * LRU_WIDTH + GATE_WIDTH
N_EXPERTS = 64
TOP_K = 8
D_EXPERT = 512
ROUTED_SCALE = 2.5
LN_EPS = 1e-5
DEEPNORM_ALPHA = (2 * DEPTH) ** 0.25

COL_GATE = 0
COL_Q = GATE_WIDTH
COL_POOL = COL_Q + ATTN_WIDTH
COL_LRU = COL_POOL + POOL_WIDTH
COL_GELU = COL_LRU + LRU_WIDTH
COL_K = COL_GELU + LRU_WIDTH
COL_V = COL_K + KV_WIDTH

LANES = 128
SUBLANES = 8
MXU_DIM = 256
VMEM_LIMIT_BYTES = 56 * 1024 * 1024

PROJ_TM = 1024
PROJ_TN = 1536
ATTN_TQ = 512
ATTN_HALO = WINDOW_CHUNKS * CHUNK
MIX_TB = 512
POOL_HALO = 16
CONV_HALO = 8
MERGE_TM = 256
ROUTER_TM = 512
DEST_TM = 2048
EXPERT_BM = 512
COMBINE_TM = 256
PROJ_SRC = 512
NEG_BIG = -1e30

SUB_WORDS = 256
SUB_COLS = 2 * SUB_WORDS
SUB_ROWS = D_MODEL // SUB_COLS
SC_WINDOW = 128


def _cparams(*sem):
    return pltpu.CompilerParams(dimension_semantics=sem, vmem_limit_bytes=VMEM_LIMIT_BYTES)


def _resident(shape, index_map):
    return pl.BlockSpec(shape, index_map, pipeline_mode=pl.Buffered(1))


def _pack_rows(lo, hi):
    lo_u = lax.bitcast_convert_type(lo.astype(BF16).astype(F32), U32)
    hi_u = lax.bitcast_convert_type(hi.astype(BF16).astype(F32), U32)
    return (lo_u >> 16) | (hi_u & jnp.uint32(0xFFFF0000))


def _unpack_rows(w):
    lo = lax.bitcast_convert_type(w << 16, F32)
    hi = lax.bitcast_convert_type(w & jnp.uint32(0xFFFF0000), F32)
    return lo, hi


def _piece_cols(j):
    return (slice(j * SUB_COLS, j * SUB_COLS + SUB_WORDS),
            slice(j * SUB_COLS + SUB_WORDS, (j + 1) * SUB_COLS))


def _store_pieces(ref, y):
    for j in range(SUB_ROWS):
        lo_cols, hi_cols = _piece_cols(j)
        ref[j] = _pack_rows(y[:, lo_cols], y[:, hi_cols])


def _layer_norm(y, g, b):
    mu = jnp.mean(y, axis=-1, keepdims=True)
    d = y - mu
    var = jnp.mean(d * d, axis=-1, keepdims=True)
    return d * lax.rsqrt(var + LN_EPS) * g + b


def _rope_kernel(pos_ref, inv_ref, cos_ref, sin_ref):
    ang = pos_ref[...].astype(F32) * inv_ref[...]
    lane = lax.broadcasted_iota(I32, ang.shape, 1)
    first_half = (lane % HEAD_DIM) < HALF_HEAD
    cos_ref[...] = jnp.cos(ang)
    s = jnp.sin(ang)
    sin_ref[...] = jnp.where(first_half, -s, s)


def _rope_tables(positions):
    t = positions.size
    tm = min(1024, t)
    inv_freq = 1.0 / (ROPE_THETA ** (jnp.arange(0, HEAD_DIM, 2, dtype=F32) / HEAD_DIM))
    inv = jnp.tile(inv_freq, LANES // HALF_HEAD).reshape(1, LANES)
    return pl.pallas_call(
        _rope_kernel,
        grid=(t // tm,),
        in_specs=[pl.BlockSpec((tm, 1), lambda i: (i, 0)),
                  pl.BlockSpec((1, LANES), lambda i: (0, 0))],
        out_specs=[pl.BlockSpec((tm, LANES), lambda i: (i, 0))] * 2,
        out_shape=[jax.ShapeDtypeStruct((t, LANES), F32)] * 2,
        compiler_params=_cparams("parallel"),
        name="rope_tables",
    )(positions.reshape(t, 1), inv)


def _proj_source_blocks():
    ref_order = (("q", ATTN_WIDTH), ("k", KV_WIDTH), ("v", KV_WIDTH), ("pool", POOL_WIDTH),
                 ("lru", LRU_WIDTH), ("gelu", LRU_WIDTH), ("gate", GATE_WIDTH))
    start, col = {}, 0
    for name, width in ref_order:
        start[name] = col
        col += width
    assert start["v"] == start["k"] + KV_WIDTH and 2 * KV_WIDTH == PROJ_SRC
    out_order = (("gate", GATE_WIDTH), ("q", ATTN_WIDTH), ("pool", POOL_WIDTH), ("lru", LRU_WIDTH),
                 ("gelu", LRU_WIDTH), ("k", 2 * KV_WIDTH))
    blocks = []
    for name, width in out_order:
        assert start[name] % PROJ_SRC == 0 and width % PROJ_SRC == 0
        blocks += [start[name] // PROJ_SRC + c for c in range(width // PROJ_SRC)]
    return blocks


def _in_proj_kernel(src_ref, x_ref, *refs):
    w_refs, o_ref, wb = refs[:-2], refs[-2], refs[-1]

    @pl.when(pl.program_id(1) == 0)
    def _():
        for c, w_ref in enumerate(w_refs):
            wb[:, c * PROJ_SRC:(c + 1) * PROJ_SRC] = w_ref[0].astype(BF16)

    o_ref[...] = jnp.dot(x_ref[...], wb[...], preferred_element_type=F32).astype(o_ref.dtype)


def _in_proj(xb, xb_row0, rows, w_in, layer):
    d = xb.shape[1]
    n = w_in.shape[2]
    tm = min(PROJ_TM, rows)
    assert xb_row0 % tm == 0
    per = PROJ_TN // PROJ_SRC
    src = jnp.asarray(_proj_source_blocks(), I32)

    def w_spec(c):
        return pl.BlockSpec((1, d, PROJ_SRC), lambda j, i, s: (layer, 0, s[per * j + c]))

    grid_spec = pltpu.PrefetchScalarGridSpec(
        num_scalar_prefetch=1,
        grid=(n // PROJ_TN, rows // tm),
        in_specs=[pl.BlockSpec((tm, d), lambda j, i, s: (i + xb_row0 // tm, 0))]
        + [w_spec(c) for c in range(per)],
        out_specs=pl.BlockSpec((tm, PROJ_TN), lambda j, i, s: (i, j)),
        scratch_shapes=[pltpu.VMEM((d, PROJ_TN), BF16)],
    )
    return pl.pallas_call(
        _in_proj_kernel,
        grid_spec=grid_spec,
        out_shape=jax.ShapeDtypeStruct((rows, n), BF16),
        compiler_params=_cparams("arbitrary", "arbitrary"),
        name="in_proj",
    )(src, xb, *([w_in] * per))


def _rope(x, cos, sin):
    w = x.shape[1]
    reps = w // LANES
    c = jnp.concatenate([cos] * reps, axis=1)
    s = jnp.concatenate([sin] * reps, axis=1)
    lane = lax.broadcasted_iota(I32, x.shape, 1)
    first_half = (lane % HEAD_DIM) < HALF_HEAD
    rot = jnp.where(first_half, pltpu.roll(x, w - HALF_HEAD, 1), pltpu.roll(x, HALF_HEAD, 1))
    return x * c + rot * s


def _attn_kernel(q_ref, k_ref, v_ref, kh_ref, vh_ref, cq_ref, sq_ref, ch_ref, sh_ref, sink_ref,
                 o_ref, s_buf, p_buf, r_buf, *, steps_per_seq):
    first = (pl.program_id(0) % steps_per_seq) == 0
    q = _rope(q_ref[...].astype(F32), cq_ref[...], sq_ref[...]) * (HEAD_DIM ** -0.5)
    q = q.astype(BF16)
    k_own = _rope(k_ref[...].astype(F32), cq_ref[...], sq_ref[...])
    k_halo = _rope(kh_ref[...].astype(F32), ch_ref[...], sh_ref[...])
    k_all = jnp.concatenate([k_halo, k_own], axis=0).astype(BF16)
    v_all = jnp.concatenate([vh_ref[...], v_ref[...]], axis=0)
    span = (WINDOW_CHUNKS + 1) * CHUNK
    rows = KV_GROUP * CHUNK
    chunk_rows = N_KV_HEADS * rows
    n_chunks = q.shape[0] // CHUNK
    win_chunk = lax.broadcasted_iota(I32, (rows, span), 1) // CHUNK
    oldest = jnp.where(first, 0, -WINDOW_CHUNKS)

    def tile_rows(c, g):
        return slice(c * chunk_rows + g * rows, c * chunk_rows + (g + 1) * rows)

    def kv_cols(g):
        return slice(g * HEAD_DIM, (g + 1) * HEAD_DIM)

    for c in range(n_chunks):
        q_c = q[c * CHUNK:(c + 1) * CHUNK, :]
        k_win = k_all[c * CHUNK:c * CHUNK + span, :]
        for g in range(N_KV_HEADS):
            qs = jnp.concatenate(
                [q_c[:, (g * KV_GROUP + i) * HEAD_DIM:(g * KV_GROUP + i + 1) * HEAD_DIM]
                 for i in range(KV_GROUP)], axis=0)
            s = lax.dot_general(qs, k_win[:, kv_cols(g)], (((1,), (1,)), ((), ())),
                                preferred_element_type=F32)
            if c < WINDOW_CHUNKS:
                s = jnp.where(win_chunk + (c - WINDOW_CHUNKS) >= oldest, s, NEG_BIG)
            s_buf[tile_rows(c, g), :] = s
    sink = sink_ref[...]
    for c in range(n_chunks):
        blk = slice(c * chunk_rows, (c + 1) * chunk_rows)
        s = s_buf[blk, :]
        m = jnp.maximum(jnp.max(s, axis=1, keepdims=True), sink)
        p = jnp.exp(s - m)
        p_buf[blk, :] = p.astype(BF16)
        r_buf[blk, :] = 1.0 / (jnp.sum(p, axis=1, keepdims=True) + jnp.exp(sink - m))
    for c in range(n_chunks):
        v_win = v_all[c * CHUNK:c * CHUNK + span, :]
        heads = []
        for g in range(N_KV_HEADS):
            o = jnp.dot(p_buf[tile_rows(c, g), :], v_win[:, kv_cols(g)], preferred_element_type=F32)
            o = o * r_buf[tile_rows(c, g), :]
            heads += [o[i * CHUNK:(i + 1) * CHUNK, :] for i in range(KV_GROUP)]
        o_ref[c * CHUNK:(c + 1) * CHUNK, :] = jnp.concatenate(heads, axis=1).astype(o_ref.dtype)


def _attention(proj, cos, sin, sinks, seq_len):
    t = proj.shape[0]
    tq = ATTN_TQ
    steps_per_seq = seq_len // tq
    halo_per_tq = tq // ATTN_HALO
    q_blk = COL_Q // ATTN_WIDTH
    k_blk = COL_K // KV_WIDTH
    v_blk = COL_V // KV_WIDTH

    def halo_row(i):
        return jnp.maximum(i * halo_per_tq - 1, 0)

    span = (WINDOW_CHUNKS + 1) * CHUNK
    tile_rows = (tq // CHUNK) * N_HEADS * CHUNK
    sink_rows = jnp.repeat(sinks, CHUNK).reshape(N_HEADS * CHUNK, 1)
    return pl.pallas_call(
        functools.partial(_attn_kernel, steps_per_seq=steps_per_seq),
        grid=(t // tq,),
        in_specs=[
            pl.BlockSpec((tq, ATTN_WIDTH), lambda i: (i, q_blk)),
            pl.BlockSpec((tq, KV_WIDTH), lambda i: (i, k_blk)),
            pl.BlockSpec((tq, KV_WIDTH), lambda i: (i, v_blk)),
            pl.BlockSpec((ATTN_HALO, KV_WIDTH), lambda i: (halo_row(i), k_blk)),
            pl.BlockSpec((ATTN_HALO, KV_WIDTH), lambda i: (halo_row(i), v_blk)),
            pl.BlockSpec((tq, LANES), lambda i: (i, 0)),
            pl.BlockSpec((tq, LANES), lambda i: (i, 0)),
            pl.BlockSpec((ATTN_HALO, LANES), lambda i: (halo_row(i), 0)),
            pl.BlockSpec((ATTN_HALO, LANES), lambda i: (halo_row(i), 0)),
            pl.BlockSpec((N_HEADS * CHUNK, 1), lambda i: (0, 0)),
        ],
        out_specs=pl.BlockSpec((tq, ATTN_WIDTH), lambda i: (i, 0)),
        out_shape=jax.ShapeDtypeStruct((t, ATTN_WIDTH), BF16),
        scratch_shapes=[pltpu.VMEM((tile_rows, span), F32), pltpu.VMEM((tile_rows, span), BF16),
                        pltpu.VMEM((tile_rows, 1), F32)],
        compiler_params=_cparams("parallel"),
        name="swa_attention",
    )(proj, proj, proj, proj, proj, cos, sin, cos, sin, sink_rows)


def _mix_kernel(up_ref, ul_ref, ug_ref, pw_ref, ps_ref, cw_ref, cb_ref, wa_ref, ba_ref, wx_ref,
                bx_ref, lam_ref, yp_ref, yl_ref, halo_p, halo_l, h_carry):
    s = pl.program_id(1)
    tb = up_ref.shape[0]

    @pl.when(s == 0)
    def _():
        halo_p[...] = jnp.zeros_like(halo_p)
        halo_l[...] = jnp.zeros_like(halo_l)
        h_carry[...] = jnp.zeros_like(h_carry)

    u = up_ref[...].astype(F32)
    ext = jnp.concatenate([halo_p[...], u], axis=0)
    sums = []
    ws = ext
    for g, win in enumerate(POOL_WINDOWS):
        shift = win // 2
        ws = ws + pltpu.roll(ws, shift, 0)
        sums.append(ws[POOL_HALO:, :POOL_GROUP])
        if g + 1 < len(POOL_WINDOWS):
            ws = ws[:, POOL_GROUP:]
    t_idx = (s * tb + lax.broadcasted_iota(I32, (tb, 1), 0)).astype(F32)
    for g, win in enumerate(POOL_WINDOWS):
        cols = slice(g * POOL_GROUP, (g + 1) * POOL_GROUP)
        count = jnp.minimum(t_idx + 1.0, float(win))
        pooled = sums[g] / count - u[:, cols]
        y = jnp.dot(pooled.astype(BF16), pw_ref[g], preferred_element_type=F32)
        yp_ref[:, cols] = (y * ps_ref[:, cols]).astype(yp_ref.dtype)
    halo_p[...] = u[tb - POOL_HALO:, :]

    ul = ul_ref[...].astype(F32)
    extl = jnp.concatenate([halo_l[...], ul], axis=0)
    xc = cb_ref[...] + cw_ref[CONV_WIDTH - 1:CONV_WIDTH, :] * ul
    for j in range(1, CONV_WIDTH):
        w_j = cw_ref[CONV_WIDTH - 1 - j:CONV_WIDTH - j, :]
        xc = xc + w_j * pltpu.roll(extl, j, 0)[CONV_HALO:, :]
    halo_l[...] = ul[tb - CONV_HALO:, :]
    xcb = xc.astype(BF16)
    r_parts, i_parts = [], []
    for j in range(LRU_WIDTH // MXU_DIM):
        cols = slice(j * MXU_DIM, (j + 1) * MXU_DIM)
        r_parts.append(jnp.dot(xcb[:, cols], wa_ref[j], preferred_element_type=F32))
        i_parts.append(jnp.dot(xcb[:, cols], wx_ref[j], preferred_element_type=F32))
    r = jax.nn.sigmoid(jnp.concatenate(r_parts, axis=1) + ba_ref[...])
    ig = jax.nn.sigmoid(jnp.concatenate(i_parts, axis=1) + bx_ref[...])
    z = -lam_ref[...]
    softplus = jnp.maximum(z, 0.0) + jnp.log1p(jnp.exp(-jnp.abs(z)))
    log_a = (-LRU_C) * r * softplus
    a = jnp.exp(log_a)
    one_minus_a2 = 1.0 - a * a
    root = jnp.where(one_minus_a2 > 0.0, one_minus_a2 * lax.rsqrt(one_minus_a2), 0.0)
    b = root * (ig * xc)
    row_in_group = lax.broadcasted_iota(I32, a.shape, 0) % SUBLANES
    d = 1
    while d < SUBLANES:
        keep = row_in_group >= d
        a_prev = jnp.where(keep, pltpu.roll(a, d, 0), 1.0)
        b_prev = jnp.where(keep, pltpu.roll(b, d, 0), 0.0)
        b = a * b_prev + b
        a = a * a_prev
        d *= 2
    gelu = jax.nn.gelu(ug_ref[...].astype(F32))
    h_prev = h_carry[...]
    pair = 2 * SUBLANES
    for g in range(tb // pair):
        hs = []
        for half in range(2):
            rows = slice(g * pair + half * SUBLANES, g * pair + (half + 1) * SUBLANES)
            h = a[rows, :] * h_prev + b[rows, :]
            h_prev = h[SUBLANES - 1:SUBLANES, :]
            hs.append(h)
        rows = slice(g * pair, (g + 1) * pair)
        yl_ref[rows, :] = (jnp.concatenate(hs, axis=0) * gelu[rows, :]).astype(yl_ref.dtype)
    h_carry[...] = h_prev


def _block_diag(w):
    per = MXU_DIM // LRU_BLOCK
    w = w.reshape(LRU_BLOCKS // per, per, LRU_BLOCK, LRU_BLOCK)
    eye = jnp.eye(per, dtype=w.dtype)
    out = jnp.einsum("gpij,pq->gpiqj", w, eye)
    return out.reshape(LRU_BLOCKS // per, MXU_DIM, MXU_DIM)


def _mixers(proj, p, batch, seq_len):
    t = proj.shape[0]
    tb = min(MIX_TB, seq_len)
    steps = seq_len // tb

    def rows(bi, si):
        return bi * steps + si

    def col_spec(col):
        blk = col // POOL_WIDTH
        return pl.BlockSpec((tb, POOL_WIDTH), lambda bi, si: (rows(bi, si), blk))

    def const(shape):
        nd = len(shape)
        return pl.BlockSpec(shape, lambda bi, si: (0,) * nd)

    out_spec = pl.BlockSpec((tb, POOL_WIDTH), lambda bi, si: (rows(bi, si), 0))
    return pl.pallas_call(
        _mix_kernel,
        grid=(batch, steps),
        in_specs=[col_spec(COL_POOL), col_spec(COL_LRU), col_spec(COL_GELU),
                  const((len(POOL_WINDOWS), POOL_GROUP, POOL_GROUP)), const((1, POOL_WIDTH)),
                  const((CONV_WIDTH, LRU_WIDTH)), const((1, LRU_WIDTH)),
                  const((LRU_WIDTH // MXU_DIM, MXU_DIM, MXU_DIM)), const((1, LRU_WIDTH)),
                  const((LRU_WIDTH // MXU_DIM, MXU_DIM, MXU_DIM)), const((1, LRU_WIDTH)),
                  const((1, LRU_WIDTH))],
        out_specs=[out_spec, out_spec],
        out_shape=[jax.ShapeDtypeStruct((t, POOL_WIDTH), BF16),
                   jax.ShapeDtypeStruct((t, LRU_WIDTH), BF16)],
        scratch_shapes=[pltpu.VMEM((POOL_HALO, POOL_WIDTH), F32),
                        pltpu.VMEM((CONV_HALO, LRU_WIDTH), F32),
                        pltpu.VMEM((1, LRU_WIDTH), F32)],
        compiler_params=_cparams("arbitrary", "arbitrary"),
        name="pool_rglru",
    )(proj, proj, proj, p["pool_w"], p["pool_scale"], p["conv_w"], p["conv_b"],
      p["lru_wa"], p["lru_ba"], p["lru_wx"], p["lru_bx"], p["lru_lambda"])


def _merge_kernel(gl_ref, ya_ref, yp_ref, yl_ref, x_ref, bg_ref, wb_ref, wo_ref, g_ref, b_ref,
                  xo_ref, xb_ref, xp_ref):
    merged = None
    for j, y_ref in enumerate((ya_ref, yp_ref, yl_ref)):
        cols = slice(j * D_MODEL, (j + 1) * D_MODEL)
        z = jnp.dot(y_ref[...], wb_ref[j], preferred_element_type=F32)
        gate = jax.nn.sigmoid(gl_ref[:, cols].astype(F32) + bg_ref[:, cols])
        merged = gate * z if merged is None else merged + gate * z
    h = jnp.dot(merged.astype(BF16), wo_ref[...], preferred_element_type=F32)
    y = _layer_norm(DEEPNORM_ALPHA * x_ref[...] + h, g_ref[...], b_ref[...])
    xo_ref[...] = y
    xb_ref[...] = y.astype(BF16)
    _store_pieces(xp_ref, y)


def _merge(proj, ya, yp, yl, x, x_row0, p):
    t = proj.shape[0]
    tm = min(MERGE_TM, t)
    assert x_row0 % tm == 0
    row = lambda i: (i, 0)
    return pl.pallas_call(
        _merge_kernel,
        grid=(t // tm,),
        in_specs=[pl.BlockSpec((tm, GATE_WIDTH), row),
                  pl.BlockSpec((tm, ATTN_WIDTH), row),
                  pl.BlockSpec((tm, POOL_WIDTH), row),
                  pl.BlockSpec((tm, LRU_WIDTH), row),
                  pl.BlockSpec((tm, D_MODEL), lambda i: (i + x_row0 // tm, 0)),
                  _resident((1, GATE_WIDTH), lambda i: (0, 0)),
                  _resident((N_BRANCH, ATTN_WIDTH, D_MODEL), lambda i: (0, 0, 0)),
                  _resident((D_MODEL, D_MODEL), lambda i: (0, 0)),
                  _resident((1, D_MODEL), lambda i: (0, 0)),
                  _resident((1, D_MODEL), lambda i: (0, 0))],
        out_specs=[pl.BlockSpec((tm, D_MODEL), row), pl.BlockSpec((tm, D_MODEL), row),
                   pl.BlockSpec((SUB_ROWS, tm, SUB_WORDS), lambda i: (0, i, 0))],
        out_shape=[jax.ShapeDtypeStruct((t, D_MODEL), F32),
                   jax.ShapeDtypeStruct((t, D_MODEL), BF16),
                   jax.ShapeDtypeStruct((SUB_ROWS, t, SUB_WORDS), U32)],
        compiler_params=_cparams("parallel"),
        name="merge_ln",
    )(proj, ya, yp, yl, x, p["b_gate"], p["w_branch"], p["w_out"], p["ln1_g"], p["ln1_b"])


def _router_kernel(x_ref, rw_ref, rb_ref, idx_ref, gate_ref, rank_ref, cnt_ref, carry):
    @pl.when(pl.program_id(0) == 0)
    def _():
        carry[...] = jnp.zeros_like(carry)

    tm = x_ref.shape[0]
    logits = lax.dot_general(rw_ref[...], x_ref[...], (((1,), (1,)), ((), ())),
                             preferred_element_type=F32)
    scores = jax.nn.sigmoid(logits)
    sel = scores + rb_ref[...]
    expert = lax.broadcasted_iota(I32, scores.shape, 0)
    idxs, tops = [], []
    chosen = jnp.zeros(scores.shape, F32)
    for _ in range(TOP_K):
        best = jnp.max(sel, axis=0, keepdims=True)
        ik = jnp.min(jnp.where(sel == best, expert, N_EXPERTS), axis=0, keepdims=True)
        hit = expert == ik
        tops.append(jnp.sum(jnp.where(hit, scores, 0.0), axis=0, keepdims=True))
        idxs.append(ik)
        chosen = jnp.where(hit, 1.0, chosen)
        sel = jnp.where(hit, -jnp.inf, sel)
    top = jnp.concatenate(tops, axis=0)
    idx = jnp.concatenate(idxs, axis=0)
    gates = top / jnp.sum(top, axis=0, keepdims=True) * ROUTED_SCALE
    gate_ref[...] = jnp.transpose(gates)
    idx_ref[...] = idx
    earlier = (lax.broadcasted_iota(I32, (tm, tm), 0) < lax.broadcasted_iota(I32, (tm, tm), 1))
    before = jnp.dot(chosen.astype(BF16), jnp.where(earlier, 1.0, 0.0).astype(BF16),
                     preferred_element_type=F32)
    base = before + carry[:, 0:1]
    ranks = [jnp.sum(jnp.where(expert == idxs[k], base, 0.0), axis=0, keepdims=True)
             for k in range(TOP_K)]
    rank_ref[...] = jnp.concatenate(ranks, axis=0).astype(I32)
    carry[...] = carry[...] + jnp.sum(chosen, axis=1, keepdims=True)
    cnt_ref[...] = carry[...].astype(I32)


def _router(x, rw_t, rbias):
    t = x.shape[0]
    tm = min(ROUTER_TM, t)
    tok = lambda i: (0, i)
    return pl.pallas_call(
        _router_kernel,
        grid=(t // tm,),
        in_specs=[pl.BlockSpec((tm, D_MODEL), lambda i: (i, 0)),
                  pl.BlockSpec((N_EXPERTS, D_MODEL), lambda i: (0, 0)),
                  pl.BlockSpec((N_EXPERTS, 1), lambda i: (0, 0))],
        out_specs=[pl.BlockSpec((TOP_K, tm), tok), pl.BlockSpec((tm, TOP_K), lambda i: (i, 0)),
                   pl.BlockSpec((TOP_K, tm), tok),
                   pl.BlockSpec((N_EXPERTS, LANES), lambda i: (0, 0))],
        out_shape=[jax.ShapeDtypeStruct((TOP_K, t), I32), jax.ShapeDtypeStruct((t, TOP_K), F32),
                   jax.ShapeDtypeStruct((TOP_K, t), I32),
                   jax.ShapeDtypeStruct((N_EXPERTS, LANES), I32)],
        scratch_shapes=[pltpu.VMEM((N_EXPERTS, LANES), F32)],
        compiler_params=_cparams("arbitrary"),
        name="router_topk",
    )(x, rw_t, rbias)


def _dest_kernel(start_ref, idx_ref, rank_ref, ids_ref, *, n_rows):
    idx = idx_ref[...]
    dest = rank_ref[...] + pl.program_id(0) * n_rows
    for e in range(N_EXPERTS):
        dest = dest + jnp.where(idx == e, start_ref[e], 0)
    ids_ref[...] = dest


def _piece_ids(pad_start, idx, rank, n_rows):
    t = idx.shape[1]
    tm = min(DEST_TM, t)
    spec = pl.BlockSpec((TOP_K, tm), lambda j, i, s: (0, i))
    return pl.pallas_call(
        functools.partial(_dest_kernel, n_rows=n_rows),
        grid_spec=pltpu.PrefetchScalarGridSpec(
            num_scalar_prefetch=1, grid=(SUB_ROWS, t // tm), in_specs=[spec, spec],
            out_specs=pl.BlockSpec((TOP_K, tm), lambda j, i, s: (j, i))),
        out_shape=jax.ShapeDtypeStruct((SUB_ROWS * TOP_K, t), I32),
        compiler_params=_cparams("parallel", "parallel"),
        name="piece_ids",
    )(pad_start, idx, rank)


def _swiglu(xb, w1_ref, w3_ref, w2_ref):
    a = jnp.dot(xb, w1_ref[...], preferred_element_type=F32)
    g = jnp.dot(xb, w3_ref[...], preferred_element_type=F32)
    hid = (a * jax.nn.sigmoid(a) * g).astype(BF16)
    return jnp.dot(hid, w2_ref[...], preferred_element_type=F32)


def _sc_mesh():
    return plsc.VectorSubcoreMesh(core_axis_name="core", subcore_axis_name="subcore")


def _sc_dispatch(xp, ids, n_rows):
    t = xp.shape[1]
    per_plane = t // SC_WINDOW

    @pl.kernel(out_type=jax.ShapeDtypeStruct((SUB_ROWS * n_rows, SUB_WORDS), U32), mesh=_sc_mesh(),
               scratch_types=[])
    def scatter(x_hbm, i_hbm, o_hbm):
        def body(x_vmem, i_vmem):
            for k in range(TOP_K):
                pltpu.sync_copy(x_vmem, o_hbm.at[i_vmem.at[k]])

        pltpu.emit_pipeline(
            body,
            grid=(SUB_ROWS * per_plane,),
            in_specs=[pl.BlockSpec((SC_WINDOW, SUB_WORDS), lambda w: (w, 0)),
                      pl.BlockSpec((TOP_K, SC_WINDOW), lambda w: (w // per_plane, w % per_plane))],
            out_specs=[],
            core_axis_name=("core", "subcore"),
            dimension_semantics=(pltpu.PARALLEL,),
        )(x_hbm, i_hbm)

    out = scatter(xp.reshape(SUB_ROWS * t, SUB_WORDS), ids)
    return out.reshape(SUB_ROWS, n_rows, SUB_WORDS)


def _sc_gather(ys, ids):
    n_rows = ys.shape[1]
    t = ids.shape[1]
    per_row = t // SC_WINDOW
    pieces = SUB_ROWS * TOP_K * t

    @pl.kernel(out_type=jax.ShapeDtypeStruct((pieces, SUB_WORDS), U32), mesh=_sc_mesh(),
               scratch_types=[])
    def gather(y_hbm, i_hbm, o_hbm):
        def body(i_vmem, o_vmem):
            pltpu.sync_copy(y_hbm.at[i_vmem.at[0]], o_vmem)

        pltpu.emit_pipeline(
            body,
            grid=(pieces // SC_WINDOW,),
            in_specs=[pl.BlockSpec((1, SC_WINDOW), lambda w: (w // per_row, w % per_row))],
            out_specs=[pl.BlockSpec((SC_WINDOW, SUB_WORDS), lambda w: (w, 0))],
            core_axis_name=("core", "subcore"),
            dimension_semantics=(pltpu.PARALLEL,),
        )(i_hbm, o_hbm)

    out = gather(ys.reshape(SUB_ROWS * n_rows, SUB_WORDS), ids)
    return out.reshape(SUB_ROWS, TOP_K * t, SUB_WORDS)


def _expert_kernel(bexp_ref, blive_ref, nused_ref, bfirst_ref, bslot_ref, bnext_ref, xs_ref, w1_hbm,
                   w3_hbm, w2_hbm, ys_ref, st1, st3, st2, w1b, w3b, w2b, xb, sem, *, layer):
    b = pl.program_id(0)
    used = b < nused_ref[0]

    def weight_copies(e, slot):
        return (pltpu.make_async_copy(w1_hbm.at[layer, e], st1.at[slot], sem.at[slot, 0]),
                pltpu.make_async_copy(w3_hbm.at[layer, e], st3.at[slot], sem.at[slot, 1]),
                pltpu.make_async_copy(w2_hbm.at[layer, e], st2.at[slot], sem.at[slot, 2]))

    @pl.when(b == 0)
    def _():
        for cp in weight_copies(bexp_ref[0], 0):
            cp.start()

    @pl.when(jnp.logical_and(used, bfirst_ref[b] == 1))
    def _():
        slot = bslot_ref[b]
        for cp in weight_copies(bexp_ref[b], slot):
            cp.wait()

        @pl.when(bnext_ref[b] >= 0)
        def _():
            for cp in weight_copies(bnext_ref[b], 1 - slot):
                cp.start()

        w1b[...] = st1[slot].astype(BF16)
        w3b[...] = st3[slot].astype(BF16)
        w2b[...] = st2[slot].astype(BF16)

    bm = xb.shape[0]
    live_rows = blive_ref[b]

    def run(m):
        live = lax.broadcasted_iota(I32, (m, SUB_WORDS), 0) < live_rows
        for j in range(SUB_ROWS):
            lo, hi = _unpack_rows(jnp.where(live, xs_ref[j, :m, :], jnp.uint32(0)))
            lo_cols, hi_cols = _piece_cols(j)
            xb[:m, lo_cols] = lo.astype(BF16)
            xb[:m, hi_cols] = hi.astype(BF16)
        y = _swiglu(xb[:m, :], w1b, w3b, w2b)
        for j in range(SUB_ROWS):
            lo_cols, hi_cols = _piece_cols(j)
            ys_ref[j, :m, :] = _pack_rows(y[:, lo_cols], y[:, hi_cols])

    @pl.when(jnp.logical_and(used, live_rows > bm // 2))
    def _():
        run(bm)

    @pl.when(jnp.logical_and(used, live_rows <= bm // 2))
    def _():
        run(bm // 2)
        ys_ref[:, bm // 2:, :] = jnp.zeros((SUB_ROWS, bm - bm // 2, SUB_WORDS), U32)

    @pl.when(jnp.logical_not(used))
    def _():
        ys_ref[...] = jnp.zeros_like(ys_ref)


def _experts(r, xs, w1, w3, w2, layer):
    n_blocks = xs.shape[1] // EXPERT_BM
    n_tables = 6

    def rows_spec(index):
        return pl.BlockSpec((SUB_ROWS, EXPERT_BM, SUB_WORDS), lambda b, *tables: (0, index(b, tables), 0))

    grid_spec = pltpu.PrefetchScalarGridSpec(
        num_scalar_prefetch=n_tables,
        grid=(n_blocks,),
        in_specs=[rows_spec(lambda b, tables: jnp.minimum(b, tables[2][0] - 1)),
                  pl.BlockSpec(memory_space=pl.ANY), pl.BlockSpec(memory_space=pl.ANY),
                  pl.BlockSpec(memory_space=pl.ANY)],
        out_specs=rows_spec(lambda b, tables: b),
        scratch_shapes=[pltpu.VMEM((2, D_MODEL, D_EXPERT), F32), pltpu.VMEM((2, D_MODEL, D_EXPERT), F32),
                        pltpu.VMEM((2, D_EXPERT, D_MODEL), F32),
                        pltpu.VMEM((D_MODEL, D_EXPERT), BF16), pltpu.VMEM((D_MODEL, D_EXPERT), BF16),
                        pltpu.VMEM((D_EXPERT, D_MODEL), BF16), pltpu.VMEM((EXPERT_BM, D_MODEL), BF16),
                        pltpu.SemaphoreType.DMA((2, 3))],
    )
    return pl.pallas_call(
        functools.partial(_expert_kernel, layer=layer),
        grid_spec=grid_spec,
        out_shape=jax.ShapeDtypeStruct(xs.shape, U32),
        compiler_params=_cparams("arbitrary"),
        name="moe_experts",
    )(r["block_exp"], r["block_live"], r["n_used"], r["block_first"], r["block_slot"], r["block_next"],
      xs, w1, w3, w2)


def _combine_kernel(gate_ref, x_ref, xb_ref, s1_ref, s3_ref, s2_ref, g_ref, b_ref, *refs):
    yk_refs = refs[:TOP_K]
    xo_ref, xbo_ref, hbuf = refs[-3:]
    tm = x_ref.shape[0]
    shared = _swiglu(xb_ref[...], s1_ref, s3_ref, s2_ref)
    gates = [jnp.broadcast_to(gate_ref[:, k:k + 1], (tm, SUB_WORDS)) for k in range(TOP_K)]
    for j in range(SUB_ROWS):
        lo_cols, hi_cols = _piece_cols(j)
        acc_lo = shared[:, lo_cols]
        acc_hi = shared[:, hi_cols]
        for k in range(TOP_K):
            lo, hi = _unpack_rows(yk_refs[k][j])
            acc_lo = acc_lo + gates[k] * lo
            acc_hi = acc_hi + gates[k] * hi
        hbuf[:, lo_cols] = acc_lo
        hbuf[:, hi_cols] = acc_hi
    y = _layer_norm(DEEPNORM_ALPHA * x_ref[...] + hbuf[...], g_ref[...], b_ref[...])
    xo_ref[...] = y
    xbo_ref[...] = y.astype(BF16)


def _combine(gate_t, x, xb, yg, p, out_rows=None, out_row0=0, out_buf=None):
    t = x.shape[0]
    tm = min(COMBINE_TM, t)
    steps = t // tm
    row = lambda i: (i, 0)
    assert out_row0 % tm == 0

    def slot_spec(k):
        return pl.BlockSpec((SUB_ROWS, tm, SUB_WORDS), lambda i: (0, k * steps + i, 0))

    in_specs = [pl.BlockSpec((tm, TOP_K), row),
                pl.BlockSpec((tm, D_MODEL), row),
                pl.BlockSpec((tm, D_MODEL), row),
                _resident((D_MODEL, D_EXPERT), lambda i: (0, 0)),
                _resident((D_MODEL, D_EXPERT), lambda i: (0, 0)),
                _resident((D_EXPERT, D_MODEL), lambda i: (0, 0)),
                _resident((1, D_MODEL), lambda i: (0, 0)),
                _resident((1, D_MODEL), lambda i: (0, 0))] + [slot_spec(k) for k in range(TOP_K)]
    args = [gate_t, x, xb, p["sh_w1"], p["sh_w3"], p["sh_w2"], p["ln2_g"], p["ln2_b"]] + [yg] * TOP_K
    aliases = {}
    if out_buf is not None:
        aliases = {len(args): 0}
        in_specs.append(pl.BlockSpec(memory_space=pl.ANY))
        args.append(out_buf)
    return pl.pallas_call(
        _combine_kernel,
        grid=(steps,),
        in_specs=in_specs,
        out_specs=[pl.BlockSpec((tm, D_MODEL), lambda i: (i + out_row0 // tm, 0)),
                   pl.BlockSpec((tm, D_MODEL), row)],
        out_shape=[jax.ShapeDtypeStruct((out_rows or t, D_MODEL), F32),
                   jax.ShapeDtypeStruct((t, D_MODEL), BF16)],
        scratch_shapes=[pltpu.VMEM((tm, D_MODEL), F32)],
        input_output_aliases=aliases,
        compiler_params=_cparams("parallel"),
        name="moe_combine_ln",
    )(*args)


def _n_expert_blocks(t):
    return -(-(t * TOP_K + N_EXPERTS * (EXPERT_BM - 1)) // EXPERT_BM)


def _route(xb, p):
    t = xb.shape[0]
    n_blocks = _n_expert_blocks(t)
    idx, gate_t, rank, cnt = _router(xb, p["router_w_t"], p["router_bias"])
    counts = cnt[:, 0]
    padded = (counts + EXPERT_BM - 1) // EXPERT_BM * EXPERT_BM
    pad_end = jnp.cumsum(padded)
    pad_start = pad_end - padded
    n_used = (pad_end[-1:] // EXPERT_BM).astype(I32)
    block = jnp.arange(n_blocks, dtype=I32)
    block_exp = jnp.minimum(jnp.sum(pad_end[None, :] <= block[:, None] * EXPERT_BM, axis=1),
                            N_EXPERTS - 1).astype(I32)
    expert = jnp.arange(N_EXPERTS, dtype=I32)
    of_block = block_exp[:, None] == expert[None, :]

    def per_block(table):
        return jnp.sum(jnp.where(of_block, table[None, :], 0), axis=1).astype(I32)

    block_live = jnp.clip(per_block(counts) - (block * EXPERT_BM - per_block(pad_start)), 0, EXPERT_BM)
    active = counts > 0
    ordinal = jnp.cumsum(active.astype(I32)) - 1
    later = active[None, :] & (expert[None, :] > expert[:, None])
    next_active = jnp.min(jnp.where(later, expert[None, :], N_EXPERTS), axis=1)
    next_active = jnp.where(next_active < N_EXPERTS, next_active, -1)
    used = block < n_used[0]
    first = used & (block * EXPERT_BM == per_block(pad_start))
    n_rows = n_blocks * EXPERT_BM
    ids = _piece_ids(pad_start.astype(I32), idx, rank, n_rows)
    return dict(ids=ids, gate_t=gate_t, block_exp=block_exp, block_live=block_live.astype(I32),
                n_used=n_used, block_first=first.astype(I32), block_slot=per_block(ordinal) % 2,
                block_next=per_block(next_active), n_rows=n_rows)


def _layer_params(l, b_gate, sinks, pool_w, pool_scale, conv_w, conv_b, lru_wa, lru_ba, lru_wx,
                  lru_bx, lru_lambda, w_branch, w_out, ln1_g, ln1_b, router_w, router_bias,
                  sh_w1, sh_w3, sh_w2, ln2_g, ln2_b):
    row = lambda a: a[l].reshape(1, -1).astype(F32)
    return dict(
        b_gate=row(b_gate), sinks=sinks[l].astype(F32),
        pool_w=pool_w[l].astype(BF16), pool_scale=row(pool_scale),
        conv_w=conv_w[l].astype(F32), conv_b=row(conv_b),
        lru_wa=_block_diag(lru_wa[l]).astype(BF16), lru_ba=row(lru_ba),
        lru_wx=_block_diag(lru_wx[l]).astype(BF16), lru_bx=row(lru_bx),
        lru_lambda=row(lru_lambda),
        w_branch=w_branch[l].astype(BF16), w_out=w_out[l].astype(BF16),
        ln1_g=row(ln1_g), ln1_b=row(ln1_b),
        router_w_t=router_w[l].T.astype(BF16), router_bias=router_bias[l].reshape(-1, 1).astype(F32),
        sh_w1=sh_w1[l].astype(BF16), sh_w3=sh_w3[l].astype(BF16), sh_w2=sh_w2[l].astype(BF16),
        ln2_g=row(ln2_g), ln2_b=row(ln2_b),
    )


def kernel(x, positions, w_in, b_gate, sinks, pool_w, pool_scale, conv_w, conv_b, lru_wa, lru_ba, lru_wx, lru_bx, lru_lambda, w_branch, w_out, ln1_g, ln1_b, router_w, router_bias, exp_w1, exp_w3, exp_w2, sh_w1, sh_w3, sh_w2, ln2_g, ln2_b):
    batch, seq_len, d = x.shape
    assert d == D_MODEL and seq_len % max(ATTN_TQ, MIX_TB) == 0 and seq_len % min(PROJ_TM, seq_len) == 0
    t = batch * seq_len
    cos_all, sin_all = _rope_tables(positions)
    x_all = x.reshape(t, d).astype(F32)
    xb_all = x_all.astype(BF16)
    seqs = []
    for s in range(batch):
        rows = slice(s * seq_len, (s + 1) * seq_len)
        seqs.append(dict(xf=x_all, xb=xb_all, row0=s * seq_len, cos=cos_all[rows], sin=sin_all[rows]))
    for l in range(DEPTH):
        p = _layer_params(l, b_gate, sinks, pool_w, pool_scale, conv_w, conv_b, lru_wa, lru_ba,
                          lru_wx, lru_bx, lru_lambda, w_branch, w_out, ln1_g, ln1_b, router_w,
                          router_bias, sh_w1, sh_w3, sh_w2, ln2_g, ln2_b)
        for sq in seqs:
            proj = _in_proj(sq["xb"], sq["row0"], seq_len, w_in, l)
            ya = _attention(proj, sq["cos"], sq["sin"], p["sinks"], seq_len)
            yp, yl = _mixers(proj, p, 1, seq_len)
            sq["xf"], sq["xb"], xp = _merge(proj, ya, yp, yl, sq["xf"], sq["row0"], p)
            sq["row0"] = 0
            sq["route"] = _route(sq["xb"], p)
            ids = sq["route"]["ids"]
            if l > 0 and sq is seqs[0]:
                ids, _ = lax.optimization_barrier((ids, seqs[-1]["xb"]))
            sq["xs"] = _sc_dispatch(xp, ids, sq["route"]["n_rows"])
        last = l + 1 == DEPTH
        out = None

        def finish(s):
            nonlocal out
            sq = seqs[s]
            if last:
                out, xb_new = _combine(sq["route"]["gate_t"], sq["xf"], sq["xb"], sq["yg"], p,
                                       out_rows=t, out_row0=s * seq_len, out_buf=out)
            else:
                sq["xf"], xb_new = _combine(sq["route"]["gate_t"], sq["xf"], sq["xb"], sq["yg"], p)
                sq["xb"] = xb_new
            return xb_new

        for s, sq in enumerate(seqs):
            ys = _experts(sq["route"], sq["xs"], exp_w1, exp_w3, exp_w2, l)
            ids = sq["route"]["ids"]
            if s > 0:
                done = finish(s - 1)
                if not last:
                    ids, _ = lax.optimization_barrier((ids, done))
            sq["yg"] = _sc_gather(ys, ids)
        finish(batch - 1)
    return out.reshape(batch, seq_len, d).astype(x.dtype)
```
